```python
import jax, jax.numpy as jnp
from jax import lax
import numpy as np

D_MODEL = 2048
BATCH = 8
SEQ = 8192
DEPTH = 4

GRID_W = 64
CTX_LEN = 256
HEAD_DIM = 128
ROPE_THETA = 10000.0

RET_HEADS = 4
RET_DIM = RET_HEADS * HEAD_DIM
RET_CHUNK = 128
SWA_HEADS = 6
SWA_KV_HEADS = 2
SWA_GROUP = SWA_HEADS // SWA_KV_HEADS
SWA_WINDOW = 128
SWA_BLOCK = 128
MLA_HEADS = 6
MLA_Q_RANK = 512
MLA_KV_RANK = 256
MLA_NOPE_DIM = 128
MLA_ROPE_DIM = 64
MLA_V_DIM = 128
MLA_Q_BLOCK = 128
MLA_SCALE = (MLA_NOPE_DIM + MLA_ROPE_DIM) ** -0.5

MIX_WIDTH = RET_DIM + SWA_HEADS * HEAD_DIM + MLA_HEADS * MLA_V_DIM
IN_SPLITS = (RET_DIM, RET_DIM, RET_DIM, RET_DIM,
             SWA_HEADS * HEAD_DIM, SWA_KV_HEADS * HEAD_DIM, SWA_KV_HEADS * HEAD_DIM,
             MLA_Q_RANK, MLA_KV_RANK, MLA_ROPE_DIM)
IN_WIDTH = 4 * RET_DIM + (SWA_HEADS + 2 * SWA_KV_HEADS) * HEAD_DIM + MLA_Q_RANK + MLA_KV_RANK + MLA_ROPE_DIM

D_FF = 5632
CONV_W = 3
N_MOD = 6
LN_EPS = 1e-5
RMS_EPS = 1e-6
NEG_INF = -1e30
DEEPNORM_ALPHA = (2 * DEPTH) ** 0.25
DEEPNORM_BETA = (8 * DEPTH) ** -0.25

kernel_name = "hybrid_ret_swa_mla_dit_trunk"


def layer_norm(x, g, b):
    xf = x.astype(jnp.float32)
    mu = jnp.mean(xf, -1, keepdims=True)
    var = jnp.mean(jnp.square(xf - mu), -1, keepdims=True)
    return ((xf - mu) * lax.rsqrt(var + LN_EPS)).astype(x.dtype) * g + b


def rms_norm(x, g=None):
    xf = x.astype(jnp.float32)
    y = (xf * lax.rsqrt(jnp.mean(jnp.square(xf), -1, keepdims=True) + RMS_EPS)).astype(x.dtype)
    return y if g is None else y * g


def heads(t, n):
    return t.reshape(t.shape[:-1] + (n, t.shape[-1] // n))


def axial_rope_tables(n_tokens, dim):
    rows = n_tokens // GRID_W
    r, cc = jnp.meshgrid(jnp.arange(rows, dtype=jnp.float32), jnp.arange(GRID_W, dtype=jnp.float32), indexing="ij")
    r = r.reshape(-1)
    cc = cc.reshape(-1)
    n_freq = dim // 4
    inv = ROPE_THETA ** (-jnp.arange(n_freq, dtype=jnp.float32) / n_freq)
    ang_r = r[:, None] * inv
    ang_c = cc[:, None] * inv
    ang = jnp.concatenate([ang_r, ang_r, ang_c, ang_c], axis=-1)
    return jnp.cos(ang), jnp.sin(ang)


def apply_rope(x, cos, sin):
    nq = x.shape[-1] // 4
    xr = x.reshape(x.shape[:-1] + (2, 2, nq))
    rot = jnp.stack([-xr[..., 1, :], xr[..., 0, :]], axis=-2).reshape(x.shape)
    return x * cos[:, None, :].astype(x.dtype) + rot * sin[:, None, :].astype(x.dtype)


def retention_scan(q, k, v, log_gamma, state0):
    B, L, H, dk = q.shape
    dv = v.shape[-1]
    C = RET_CHUNK
    N = L // C
    dt = q.dtype
    qc = q.reshape(B, N, C, H, dk)
    kc = k.reshape(B, N, C, H, dk)
    vc = v.reshape(B, N, C, H, dv)
    pos = jnp.arange(C, dtype=jnp.float32)
    diff = pos[:, None] - pos[None, :]
    intra = jnp.where(diff >= 0, jnp.exp(log_gamma[:, None, None] * jnp.maximum(diff, 0.0)), 0.0).astype(dt)
    q_decay = jnp.exp(log_gamma[:, None] * (pos + 1.0)).astype(dt)
    k_decay = jnp.exp(log_gamma[:, None] * (C - 1.0 - pos)).astype(dt)
    chunk_decay = jnp.exp(log_gamma * C).astype(dt)[None, :, None, None]
    scores = jnp.einsum("bnihd,bnjhd->bnhij", qc, kc) * intra
    inner = jnp.einsum("bnhij,bnjhe->bnihe", scores, vc)
    kv = jnp.einsum("bnjhd,bnjhe,hj->nbhde", kc, vc, k_decay)

    def step(s, kv_n):
        return chunk_decay * s + kv_n, s

    s_final, s_in = lax.scan(step, state0, kv)
    cross = jnp.einsum("bnihd,nbhde,hi->bnihe", qc, s_in, q_decay)
    return (inner + cross).reshape(B, L, H, dv), s_final


def retention_bidir(q, k, v, lg_f, lg_b, s0_f, s0_b):
    out_f, s_f = retention_scan(q, k, v, lg_f, s0_f)
    out_b, s_b = retention_scan(q[:, ::-1], k[:, ::-1], v[:, ::-1], lg_b, s0_b)
    return out_f + out_b[:, ::-1], s_f, s_b


def softmax_with_sink(s, sink_kg):
    col = jnp.broadcast_to(sink_kg[:, :, None, None], s.shape[:-1] + (1,))
    p = jax.nn.softmax(jnp.concatenate([s, col], axis=-1), axis=-1)
    return p[..., :-1]


def swa_latent(q, k, v, k_ctx, v_ctx, sink_kg):
    B, L, H, d = q.shape
    Cb = SWA_BLOCK
    N = L // Cb
    qb = q.reshape(B, N, Cb, SWA_KV_HEADS, SWA_GROUP, d)

    def band(t):
        tp = jnp.pad(t, ((0, 0), (Cb, Cb), (0, 0), (0, 0))).reshape(B, N + 2, Cb, SWA_KV_HEADS, d)
        return jnp.concatenate([tp[:, :-2], tp[:, 1:-1], tp[:, 2:]], axis=2)

    kw, vw = band(k), band(v)
    scale = d ** -0.5
    s_loc = jnp.einsum("bnqkgd,bnskd->bnkgqs", qb, kw).astype(jnp.float32) * scale
    n_i = jnp.arange(N)[:, None, None]
    q_i = jnp.arange(Cb)[None, :, None]
    s_j = jnp.arange(3 * Cb)[None, None, :]
    key_pos = n_i * Cb + s_j - Cb
    valid = (jnp.abs(key_pos - (n_i * Cb + q_i)) <= SWA_WINDOW) & (key_pos >= 0) & (key_pos < L)
    s_loc = jnp.where(valid[None, :, None, None], s_loc, NEG_INF)
    s_ctx = jnp.einsum("bnqkgd,btkd->bnkgqt", qb, k_ctx).astype(jnp.float32) * scale
    p = softmax_with_sink(jnp.concatenate([s_loc, s_ctx], axis=-1), sink_kg).astype(v.dtype)
    out = (jnp.einsum("bnkgqs,bnskd->bnqkgd", p[..., :3 * Cb], vw)
           + jnp.einsum("bnkgqt,btkd->bnqkgd", p[..., 3 * Cb:], v_ctx))
    return out.reshape(B, L, H * d)


def swa_context(q, k, v, sink_kg):
    B, T, H, d = q.shape
    qg = q.reshape(B, T, SWA_KV_HEADS, SWA_GROUP, d)
    s = jnp.einsum("btkgd,bskd->bkgts", qg, k).astype(jnp.float32) * d ** -0.5
    p = softmax_with_sink(s, sink_kg).astype(v.dtype)
    return jnp.einsum("bkgts,bskd->btkgd", p, v).reshape(B, T, H * d)


def mla_project(cq, ckv, kr, q_norm, w_uq, kv_norm, w_ukv, cos, sin):
    B, L, _ = cq.shape
    q = (rms_norm(cq, q_norm) @ w_uq).reshape(B, L, MLA_HEADS, MLA_NOPE_DIM + MLA_ROPE_DIM)
    q_nope, q_rope = q[..., :MLA_NOPE_DIM], q[..., MLA_NOPE_DIM:]
    kv = (rms_norm(ckv, kv_norm) @ w_ukv).reshape(B, L, MLA_HEADS, MLA_NOPE_DIM + MLA_V_DIM)
    k_nope, v = kv[..., :MLA_NOPE_DIM], kv[..., MLA_NOPE_DIM:]
    k_rope = kr[:, :, None, :]
    if cos is not None:
        q_rope = apply_rope(q_rope, cos, sin)
        k_rope = apply_rope(k_rope, cos, sin)
    return q_nope, q_rope, k_nope, k_rope[:, :, 0], v


def mla_attend(q_nope, q_rope, k_nope, k_rope, v):
    s = (jnp.einsum("bqhd,bshd->bhqs", q_nope, k_nope)
         + jnp.einsum("bqhr,bsr->bhqs", q_rope, k_rope)).astype(jnp.float32) * MLA_SCALE
    p = jax.nn.softmax(s, axis=-1).astype(v.dtype)
    return jnp.einsum("bhqs,bshd->bqhd", p, v)


def mla_latent(qn, qr, kn, kr, v):
    B, L, H, _ = qn.shape
    N = L // MLA_Q_BLOCK

    def blocks(t):
        return jnp.moveaxis(t.reshape((B, N, MLA_Q_BLOCK) + t.shape[2:]), 1, 0)

    out = lax.map(lambda qs: mla_attend(qs[0], qs[1], kn, kr, v), (blocks(qn), blocks(qr)))
    return jnp.moveaxis(out, 0, 1).reshape(B, L, H * MLA_V_DIM)


def conv_ffn(h, w_up, conv_w, conv_b, w_down):
    L = h.shape[1]
    u, g = jnp.split(h @ w_up, 2, axis=-1)
    pad = CONV_W // 2
    gp = jnp.pad(g, ((0, 0), (pad, pad), (0, 0)))
    g = sum(gp[:, j:j + L] * conv_w[j] for j in range(CONV_W)) + conv_b
    return (jax.nn.silu(g) * u) @ w_down


def _fwd_setup_inputs(seed: int = 0) -> dict:
    key = jax.random.key(seed)
    ks = jax.random.split(key, 24)
    f32 = jnp.float32

    def nrm(k, shape, scale):
        return jax.random.normal(k, shape, f32) * scale

    gam = 1.0 - 2.0 ** (-5.0 - np.arange(RET_HEADS))
    logit = jnp.asarray(np.log(gam / (1.0 - gam)), f32)
    return {
        "x": nrm(ks[0], (BATCH, SEQ, D_MODEL), 1.0),
        "c": nrm(ks[1], (BATCH, D_MODEL), 1.0),
        "ctx": nrm(ks[2], (BATCH, CTX_LEN, D_MODEL), 1.0),
        "c_ctx": nrm(ks[3], (D_MODEL,), 1.0),
        "ada_w": nrm(ks[4], (DEPTH, D_MODEL, N_MOD * D_MODEL), 0.5 * D_MODEL ** -0.5),
        "ada_b": nrm(ks[5], (DEPTH, N_MOD * D_MODEL), 0.02),
        "w_in": nrm(ks[6], (DEPTH, D_MODEL, IN_WIDTH), D_MODEL ** -0.5),
        "ret_decay_fwd": logit + nrm(ks[7], (DEPTH, RET_HEADS), 0.1),
        "ret_decay_bwd": logit + nrm(ks[8], (DEPTH, RET_HEADS), 0.1),
        "swa_sink": nrm(ks[9], (DEPTH, SWA_HEADS), 0.5),
        "mla_q_norm": 1.0 + nrm(ks[10], (DEPTH, MLA_Q_RANK), 0.02),
        "mla_w_uq": nrm(ks[11], (DEPTH, MLA_Q_RANK, MLA_HEADS * (MLA_NOPE_DIM + MLA_ROPE_DIM)), MLA_Q_RANK ** -0.5),
        "mla_kv_norm": 1.0 + nrm(ks[12], (DEPTH, MLA_KV_RANK), 0.02),
        "mla_w_ukv": nrm(ks[13], (DEPTH, MLA_KV_RANK, MLA_HEADS * (MLA_NOPE_DIM + MLA_V_DIM)), MLA_KV_RANK ** -0.5),
        "w_o": nrm(ks[14], (DEPTH, MIX_WIDTH, D_MODEL), DEEPNORM_BETA * MIX_WIDTH ** -0.5),
        "ln1_g": 1.0 + nrm(ks[15], (DEPTH, D_MODEL), 0.02),
        "ln1_b": nrm(ks[16], (DEPTH, D_MODEL), 0.02),
        "ffn_w_up": nrm(ks[17], (DEPTH, D_MODEL, 2 * D_FF), D_MODEL ** -0.5),
        "ffn_conv_w": nrm(ks[18], (DEPTH, CONV_W, D_FF), CONV_W ** -0.5),
        "ffn_conv_b": nrm(ks[19], (DEPTH, D_FF), 0.02),
        "ffn_w_down": nrm(ks[20], (DEPTH, D_FF, D_MODEL), DEEPNORM_BETA * D_FF ** -0.5),
        "ln2_g": 1.0 + nrm(ks[21], (DEPTH, D_MODEL), 0.02),
        "ln2_b": nrm(ks[22], (DEPTH, D_MODEL), 0.02),
    }


def _fwd_reference(x, c, ctx, c_ctx, ada_w, ada_b, w_in, ret_decay_fwd, ret_decay_bwd, swa_sink,
              mla_q_norm, mla_w_uq, mla_kv_norm, mla_w_ukv, w_o, ln1_g, ln1_b,
              ffn_w_up, ffn_conv_w, ffn_conv_b, ffn_w_down, ln2_g, ln2_b):
    B, L, _ = x.shape
    T = ctx.shape[1]
    cos_h, sin_h = axial_rope_tables(L, HEAD_DIM)
    cos_m, sin_m = axial_rope_tables(L, MLA_ROPE_DIM)
    split_idx = np.cumsum(IN_SPLITS)[:-1].tolist()
    k_scale = HEAD_DIM ** -0.5
    sc = jax.nn.silu(c)
    scc = jax.nn.silu(c_ctx)
    xc = ctx
    for l in range(DEPTH):
        last = l == DEPTH - 1
        mx = [m[:, None, :] for m in jnp.split(sc @ ada_w[l] + ada_b[l], N_MOD, axis=-1)]
        mc = jnp.split(scc @ ada_w[l] + ada_b[l], N_MOD, axis=-1)
        hx = x * (1.0 + mx[1]) + mx[0]
        hc = xc * (1.0 + mc[1]) + mc[0]
        (rq, rk, rv, rg, sq, sk, sv, mcq, mckv, mkr) = jnp.split(hx @ w_in[l], split_idx, axis=-1)
        (rq_c, rk_c, rv_c, rg_c, sq_c, sk_c, sv_c, mcq_c, mckv_c, mkr_c) = jnp.split(hc @ w_in[l], split_idx, axis=-1)

        lg_f = jax.nn.log_sigmoid(ret_decay_fwd[l].astype(jnp.float32))
        lg_b = jax.nn.log_sigmoid(ret_decay_bwd[l].astype(jnp.float32))
        s0 = jnp.zeros((B, RET_HEADS, HEAD_DIM, HEAD_DIM), x.dtype)
        ret_c, st_f, st_b = retention_bidir(heads(rq_c, RET_HEADS), heads(rk_c, RET_HEADS) * k_scale,
                                            heads(rv_c, RET_HEADS), lg_f, lg_b, s0, s0)
        ret_x, _, _ = retention_bidir(apply_rope(heads(rq, RET_HEADS), cos_h, sin_h),
                                      apply_rope(heads(rk, RET_HEADS), cos_h, sin_h) * k_scale,
                                      heads(rv, RET_HEADS), lg_f, lg_b, st_f, st_b)
        y_ret = jax.nn.silu(rg) * rms_norm(ret_x).reshape(B, L, RET_DIM)

        sink_kg = swa_sink[l].astype(jnp.float32).reshape(SWA_KV_HEADS, SWA_GROUP)
        k_sc = heads(sk_c, SWA_KV_HEADS)
        v_sc = heads(sv_c, SWA_KV_HEADS)
        y_swa = swa_latent(apply_rope(heads(sq, SWA_HEADS), cos_h, sin_h),
                           apply_rope(heads(sk, SWA_KV_HEADS), cos_h, sin_h),
                           heads(sv, SWA_KV_HEADS), k_sc, v_sc, sink_kg)

        qn_x, qr_x, kn_x, kr_x, v_x = mla_project(mcq, mckv, mkr, mla_q_norm[l], mla_w_uq[l],
                                                  mla_kv_norm[l], mla_w_ukv[l], cos_m, sin_m)
        qn_c, qr_c, kn_c, kr_c, v_mc = mla_project(mcq_c, mckv_c, mkr_c, mla_q_norm[l], mla_w_uq[l],
                                                   mla_kv_norm[l], mla_w_ukv[l], None, None)
        y_mla = mla_latent(qn_x, qr_x, jnp.concatenate([kn_x, kn_c], axis=1),
                           jnp.concatenate([kr_x, kr_c], axis=1), jnp.concatenate([v_x, v_mc], axis=1))

        mix_x = jnp.concatenate([y_ret, y_swa, y_mla], axis=-1) @ w_o[l]
        x_a = layer_norm(DEEPNORM_ALPHA * x + (1.0 + mx[2]) * mix_x, ln1_g[l], ln1_b[l])
        f_x = conv_ffn(x_a * (1.0 + mx[4]) + mx[3], ffn_w_up[l], ffn_conv_w[l], ffn_conv_b[l], ffn_w_down[l])
        x_new = layer_norm(DEEPNORM_ALPHA * x_a + (1.0 + mx[5]) * f_x, ln2_g[l], ln2_b[l])

        if not last:
            y_ret_c = jax.nn.silu(rg_c) * rms_norm(ret_c).reshape(B, T, RET_DIM)
            y_swa_c = swa_context(heads(sq_c, SWA_HEADS), k_sc, v_sc, sink_kg)
            y_mla_c = mla_attend(qn_c, qr_c, kn_c, kr_c, v_mc).reshape(B, T, MLA_HEADS * MLA_V_DIM)
            mix_c = jnp.concatenate([y_ret_c, y_swa_c, y_mla_c], axis=-1) @ w_o[l]
            xc_a = layer_norm(DEEPNORM_ALPHA * xc + (1.0 + mc[2]) * mix_c, ln1_g[l], ln1_b[l])
            f_c = conv_ffn(xc_a * (1.0 + mc[4]) + mc[3], ffn_w_up[l], ffn_conv_w[l], ffn_conv_b[l], ffn_w_down[l])
            xc = layer_norm(DEEPNORM_ALPHA * xc_a + (1.0 + mc[5]) * f_c, ln2_g[l], ln2_b[l])
        x = x_new
    return x


import jax as _jax
import jax.numpy as _jnp

TWIN_FORMAT = 'train_step'
FWD_PARAMS = ['x', 'c', 'ctx', 'c_ctx', 'ada_w', 'ada_b', 'w_in', 'ret_decay_fwd', 'ret_decay_bwd', 'swa_sink', 'mla_q_norm', 'mla_w_uq', 'mla_kv_norm', 'mla_w_ukv', 'w_o', 'ln1_g', 'ln1_b', 'ffn_w_up', 'ffn_conv_w', 'ffn_conv_b', 'ffn_w_down', 'ln2_g', 'ln2_b']
TWIN_WEIGHTS = ['c_ctx', 'ada_w', 'ada_b', 'w_in', 'ret_decay_fwd', 'ret_decay_bwd', 'swa_sink', 'mla_q_norm', 'mla_w_uq', 'mla_kv_norm', 'mla_w_ukv', 'w_o', 'ln1_g', 'ln1_b', 'ffn_w_up', 'ffn_conv_w', 'ffn_conv_b', 'ffn_w_down', 'ln2_g', 'ln2_b']
TWIN_DIFF_INPUT = 'x'
TWIN_INPUTS = ['x', 'c', 'ctx', 'c_ctx', 'ada_w', 'ada_b', 'w_in', 'ret_decay_fwd', 'ret_decay_bwd', 'swa_sink', 'mla_q_norm', 'mla_w_uq', 'mla_kv_norm', 'mla_w_ukv', 'w_o', 'ln1_g', 'ln1_b', 'ffn_w_up', 'ffn_conv_w', 'ffn_conv_b', 'ffn_w_down', 'ln2_g', 'ln2_b', 'loss_target', 'm_c_ctx', 'm_ada_w', 'm_ada_b', 'm_w_in', 'm_ret_decay_fwd', 'm_ret_decay_bwd', 'm_swa_sink', 'm_mla_q_norm', 'm_mla_w_uq', 'm_mla_kv_norm', 'm_mla_w_ukv', 'm_w_o', 'm_ln1_g', 'm_ln1_b', 'm_ffn_w_up', 'm_ffn_conv_w', 'm_ffn_conv_b', 'm_ffn_w_down', 'm_ln2_g', 'm_ln2_b', 'v_c_ctx', 'v_ada_w', 'v_ada_b', 'v_w_in', 'v_ret_decay_fwd', 'v_ret_decay_bwd', 'v_swa_sink', 'v_mla_q_norm', 'v_mla_w_uq', 'v_mla_kv_norm', 'v_mla_w_ukv', 'v_w_o', 'v_ln1_g', 'v_ln1_b', 'v_ffn_w_up', 'v_ffn_conv_w', 'v_ffn_conv_b', 'v_ffn_w_down', 'v_ln2_g', 'v_ln2_b']
TWIN_OUTPUTS = ['loss', 'grad_x', 'grad_c_ctx', 'grad_ada_w', 'grad_ada_b', 'grad_w_in', 'grad_ret_decay_fwd', 'grad_ret_decay_bwd', 'grad_swa_sink', 'grad_mla_q_norm', 'grad_mla_w_uq', 'grad_mla_kv_norm', 'grad_mla_w_ukv', 'grad_w_o', 'grad_ln1_g', 'grad_ln1_b', 'grad_ffn_w_up', 'grad_ffn_conv_w', 'grad_ffn_conv_b', 'grad_ffn_w_down', 'grad_ln2_g', 'grad_ln2_b', 'delta_c_ctx', 'delta_ada_w', 'delta_ada_b', 'delta_w_in', 'delta_ret_decay_fwd', 'delta_ret_decay_bwd', 'delta_swa_sink', 'delta_mla_q_norm', 'delta_mla_w_uq', 'delta_mla_kv_norm', 'delta_mla_w_ukv', 'delta_w_o', 'delta_ln1_g', 'delta_ln1_b', 'delta_ffn_w_up', 'delta_ffn_conv_w', 'delta_ffn_conv_b', 'delta_ffn_w_down', 'delta_ln2_g', 'delta_ln2_b', 'new_m_c_ctx', 'new_m_ada_w', 'new_m_ada_b', 'new_m_w_in', 'new_m_ret_decay_fwd', 'new_m_ret_decay_bwd', 'new_m_swa_sink', 'new_m_mla_q_norm', 'new_m_mla_w_uq', 'new_m_mla_kv_norm', 'new_m_mla_w_ukv', 'new_m_w_o', 'new_m_ln1_g', 'new_m_ln1_b', 'new_m_ffn_w_up', 'new_m_ffn_conv_w', 'new_m_ffn_conv_b', 'new_m_ffn_w_down', 'new_m_ln2_g', 'new_m_ln2_b', 'new_v_c_ctx', 'new_v_ada_w', 'new_v_ada_b', 'new_v_w_in', 'new_v_ret_decay_fwd', 'new_v_ret_decay_bwd', 'new_v_swa_sink', 'new_v_mla_q_norm', 'new_v_mla_w_uq', 'new_v_mla_kv_norm', 'new_v_mla_w_ukv', 'new_v_w_o', 'new_v_ln1_g', 'new_v_ln1_b', 'new_v_ffn_w_up', 'new_v_ffn_conv_w', 'new_v_ffn_conv_b', 'new_v_ffn_w_down', 'new_v_ln2_g', 'new_v_ln2_b']
TWIN_LEAF_KINDS = {'loss': 'loss', 'grad_x': 'grad_x', 'grad_c_ctx': 'grad_w', 'grad_ada_w': 'grad_w', 'grad_ada_b': 'grad_w', 'grad_w_in': 'grad_w', 'grad_ret_decay_fwd': 'grad_w', 'grad_ret_decay_bwd': 'grad_w', 'grad_swa_sink': 'grad_w', 'grad_mla_q_norm': 'grad_w', 'grad_mla_w_uq': 'grad_w', 'grad_mla_kv_norm': 'grad_w', 'grad_mla_w_ukv': 'grad_w', 'grad_w_o': 'grad_w', 'grad_ln1_g': 'grad_w', 'grad_ln1_b': 'grad_w', 'grad_ffn_w_up': 'grad_w', 'grad_ffn_conv_w': 'grad_w', 'grad_ffn_conv_b': 'grad_w', 'grad_ffn_w_down': 'grad_w', 'grad_ln2_g': 'grad_w', 'grad_ln2_b': 'grad_w', 'delta_c_ctx': 'delta_w', 'delta_ada_w': 'delta_w', 'delta_ada_b': 'delta_w', 'delta_w_in': 'delta_w', 'delta_ret_decay_fwd': 'delta_w', 'delta_ret_decay_bwd': 'delta_w', 'delta_swa_sink': 'delta_w', 'delta_mla_q_norm': 'delta_w', 'delta_mla_w_uq': 'delta_w', 'delta_mla_kv_norm': 'delta_w', 'delta_mla_w_ukv': 'delta_w', 'delta_w_o': 'delta_w', 'delta_ln1_g': 'delta_w', 'delta_ln1_b': 'delta_w', 'delta_ffn_w_up': 'delta_w', 'delta_ffn_conv_w': 'delta_w', 'delta_ffn_conv_b': 'delta_w', 'delta_ffn_w_down': 'delta_w', 'delta_ln2_g': 'delta_w', 'delta_ln2_b': 'delta_w', 'new_m_c_ctx': 'new_m', 'new_m_ada_w': 'new_m', 'new_m_ada_b': 'new_m', 'new_m_w_in': 'new_m', 'new_m_ret_decay_fwd': 'new_m', 'new_m_ret_decay_bwd': 'new_m', 'new_m_swa_sink': 'new_m', 'new_m_mla_q_norm': 'new_m', 'new_m_mla_w_uq': 'new_m', 'new_m_mla_kv_norm': 'new_m', 'new_m_mla_w_ukv': 'new_m', 'new_m_w_o': 'new_m', 'new_m_ln1_g': 'new_m', 'new_m_ln1_b': 'new_m', 'new_m_ffn_w_up': 'new_m', 'new_m_ffn_conv_w': 'new_m', 'new_m_ffn_conv_b': 'new_m', 'new_m_ffn_w_down': 'new_m', 'new_m_ln2_g': 'new_m', 'new_m_ln2_b': 'new_m', 'new_v_c_ctx': 'new_v', 'new_v_ada_w': 'new_v', 'new_v_ada_b': 'new_v', 'new_v_w_in': 'new_v', 'new_v_ret_decay_fwd': 'new_v', 'new_v_ret_decay_bwd': 'new_v', 'new_v_swa_sink': 'new_v', 'new_v_mla_q_norm': 'new_v', 'new_v_mla_w_uq': 'new_v', 'new_v_mla_kv_norm': 'new_v', 'new_v_mla_w_ukv': 'new_v', 'new_v_w_o': 'new_v', 'new_v_ln1_g': 'new_v', 'new_v_ln1_b': 'new_v', 'new_v_ffn_w_up': 'new_v', 'new_v_ffn_conv_w': 'new_v', 'new_v_ffn_conv_b': 'new_v', 'new_v_ffn_w_down': 'new_v', 'new_v_ln2_g': 'new_v', 'new_v_ln2_b': 'new_v'}


def _forward(args):
    return _fwd_reference(*[args[k] for k in FWD_PARAMS])


def _output_shape():
    def fwd():
        inp = _fwd_setup_inputs(0)
        return _fwd_reference(*[inp[k] for k in FWD_PARAMS])
    out = _jax.eval_shape(fwd)
    return out.shape, out.dtype

N_MICROBATCH = 1
ADAM_LR = 0.001
ADAM_B1 = 0.9
ADAM_B2 = 0.999
ADAM_EPS = 1e-08
ADAM_WD = 0.01
ADAM_STEP = 10
PER_EXAMPLE_BATCH_AXIS = {'x': 0, 'c': 0, 'ctx': 0, 'loss_target': 0}
SHARED_INPUTS = []
_WEIGHT_DTYPES = {'c_ctx': _jnp.float32, 'ada_w': _jnp.float32, 'ada_b': _jnp.float32, 'w_in': _jnp.float32, 'ret_decay_fwd': _jnp.float32, 'ret_decay_bwd': _jnp.float32, 'swa_sink': _jnp.float32, 'mla_q_norm': _jnp.float32, 'mla_w_uq': _jnp.float32, 'mla_kv_norm': _jnp.float32, 'mla_w_ukv': _jnp.float32, 'w_o': _jnp.float32, 'ln1_g': _jnp.float32, 'ln1_b': _jnp.float32, 'ffn_w_up': _jnp.float32, 'ffn_conv_w': _jnp.float32, 'ffn_conv_b': _jnp.float32, 'ffn_w_down': _jnp.float32, 'ln2_g': _jnp.float32, 'ln2_b': _jnp.float32}
MOMENT_SCALE = {'c_ctx': 1.846933e-02, 'ada_w': 2.158592e-02, 'ada_b': 4.713548e-02, 'w_in': 2.057858e-02, 'ret_decay_fwd': 1.881171e-01, 'ret_decay_bwd': 1.714804e-01, 'swa_sink': 2.724575e-04, 'mla_q_norm': 3.920551e-03, 'mla_w_uq': 2.630234e-03, 'mla_kv_norm': 2.903364e-02, 'mla_w_ukv': 1.219636e-02, 'w_o': 4.530378e-02, 'ln1_g': 1.025664e+00, 'ln1_b': 4.987383e-01, 'ffn_w_up': 1.493727e-02, 'ffn_conv_w': 1.541015e-02, 'ffn_conv_b': 1.401816e-02, 'ffn_w_down': 5.836503e-02, 'ln2_g': 1.607237e+01, 'ln2_b': 1.286796e+00}


def _to_microbatches(a, axis):
    t = _jnp.moveaxis(a, axis, 0)
    t = t.reshape((N_MICROBATCH, t.shape[0] // N_MICROBATCH) + t.shape[1:])
    return _jnp.moveaxis(t, 1, axis + 1)


def setup_inputs(seed: int = 0) -> dict:
    inp = _fwd_setup_inputs(seed)
    key = _jax.random.fold_in(_jax.random.key(seed), 7919)
    shape, _ = _output_shape()
    out = dict(inp)
    out["loss_target"] = _jax.random.normal(_jax.random.fold_in(key, 0), shape, _jnp.float32)
    for i, name in enumerate(TWIN_WEIGHTS):
        w = inp[name].astype(_jnp.float32)
        if MOMENT_SCALE is None:
            s = _jnp.sqrt(_jnp.mean(_jnp.square(w)) + 1e-30)
        else:
            s = MOMENT_SCALE[name]
        km, kv = _jax.random.split(_jax.random.fold_in(key, i + 1))
        out[name] = w
        out["m_" + name] = s * _jax.random.normal(km, w.shape, _jnp.float32)
        out["v_" + name] = (s * s) * _jax.random.uniform(kv, w.shape, _jnp.float32, 0.5, 1.5)
    if N_MICROBATCH > 1:
        for name, axis in PER_EXAMPLE_BATCH_AXIS.items():
            out[name] = _to_microbatches(out[name], axis)
    return {'x': out['x'], 'c': out['c'], 'ctx': out['ctx'], 'c_ctx': out['c_ctx'], 'ada_w': out['ada_w'], 'ada_b': out['ada_b'], 'w_in': out['w_in'], 'ret_decay_fwd': out['ret_decay_fwd'], 'ret_decay_bwd': out['ret_decay_bwd'], 'swa_sink': out['swa_sink'], 'mla_q_norm': out['mla_q_norm'], 'mla_w_uq': out['mla_w_uq'], 'mla_kv_norm': out['mla_kv_norm'], 'mla_w_ukv': out['mla_w_ukv'], 'w_o': out['w_o'], 'ln1_g': out['ln1_g'], 'ln1_b': out['ln1_b'], 'ffn_w_up': out['ffn_w_up'], 'ffn_conv_w': out['ffn_conv_w'], 'ffn_conv_b': out['ffn_conv_b'], 'ffn_w_down': out['ffn_w_down'], 'ln2_g': out['ln2_g'], 'ln2_b': out['ln2_b'], 'loss_target': out['loss_target'], 'm_c_ctx': out['m_c_ctx'], 'm_ada_w': out['m_ada_w'], 'm_ada_b': out['m_ada_b'], 'm_w_in': out['m_w_in'], 'm_ret_decay_fwd': out['m_ret_decay_fwd'], 'm_ret_decay_bwd': out['m_ret_decay_bwd'], 'm_swa_sink': out['m_swa_sink'], 'm_mla_q_norm': out['m_mla_q_norm'], 'm_mla_w_uq': out['m_mla_w_uq'], 'm_mla_kv_norm': out['m_mla_kv_norm'], 'm_mla_w_ukv': out['m_mla_w_ukv'], 'm_w_o': out['m_w_o'], 'm_ln1_g': out['m_ln1_g'], 'm_ln1_b': out['m_ln1_b'], 'm_ffn_w_up': out['m_ffn_w_up'], 'm_ffn_conv_w': out['m_ffn_conv_w'], 'm_ffn_conv_b': out['m_ffn_conv_b'], 'm_ffn_w_down': out['m_ffn_w_down'], 'm_ln2_g': out['m_ln2_g'], 'm_ln2_b': out['m_ln2_b'], 'v_c_ctx': out['v_c_ctx'], 'v_ada_w': out['v_ada_w'], 'v_ada_b': out['v_ada_b'], 'v_w_in': out['v_w_in'], 'v_ret_decay_fwd': out['v_ret_decay_fwd'], 'v_ret_decay_bwd': out['v_ret_decay_bwd'], 'v_swa_sink': out['v_swa_sink'], 'v_mla_q_norm': out['v_mla_q_norm'], 'v_mla_w_uq': out['v_mla_w_uq'], 'v_mla_kv_norm': out['v_mla_kv_norm'], 'v_mla_w_ukv': out['v_mla_w_ukv'], 'v_w_o': out['v_w_o'], 'v_ln1_g': out['v_ln1_g'], 'v_ln1_b': out['v_ln1_b'], 'v_ffn_w_up': out['v_ffn_w_up'], 'v_ffn_conv_w': out['v_ffn_conv_w'], 'v_ffn_conv_b': out['v_ffn_conv_b'], 'v_ffn_w_down': out['v_ffn_w_down'], 'v_ln2_g': out['v_ln2_g'], 'v_ln2_b': out['v_ln2_b']}


def _loss(weights, diff, rest, loss_target):
    with _jax.named_scope("forward"):
        args = {**rest, TWIN_DIFF_INPUT: diff, **{k: w.astype(_WEIGHT_DTYPES[k]) for k, w in weights.items()}}
        y = _forward(args)
    with _jax.named_scope("loss_head"):
        err = _jnp.square(y.astype(_jnp.float32) - loss_target)
        return 0.5 * _jnp.sum(_jnp.mean(err, axis=-1)) if err.ndim else 0.5 * err


def _adamw(w, g, m, v):
    m = ADAM_B1 * m + (1.0 - ADAM_B1) * g
    v = ADAM_B2 * v + (1.0 - ADAM_B2) * _jnp.square(g)
    m_hat = m / (1.0 - ADAM_B1 ** ADAM_STEP)
    v_hat = v / (1.0 - ADAM_B2 ** ADAM_STEP)
    delta = -ADAM_LR * (m_hat / (_jnp.sqrt(v_hat) + ADAM_EPS) + ADAM_WD * w)
    return delta, m, v


def reference(x, c, ctx, c_ctx, ada_w, ada_b, w_in, ret_decay_fwd, ret_decay_bwd, swa_sink, mla_q_norm, mla_w_uq, mla_kv_norm, mla_w_ukv, w_o, ln1_g, ln1_b, ffn_w_up, ffn_conv_w, ffn_conv_b, ffn_w_down, ln2_g, ln2_b, loss_target, m_c_ctx, m_ada_w, m_ada_b, m_w_in, m_ret_decay_fwd, m_ret_decay_bwd, m_swa_sink, m_mla_q_norm, m_mla_w_uq, m_mla_kv_norm, m_mla_w_ukv, m_w_o, m_ln1_g, m_ln1_b, m_ffn_w_up, m_ffn_conv_w, m_ffn_conv_b, m_ffn_w_down, m_ln2_g, m_ln2_b, v_c_ctx, v_ada_w, v_ada_b, v_w_in, v_ret_decay_fwd, v_ret_decay_bwd, v_swa_sink, v_mla_q_norm, v_mla_w_uq, v_mla_kv_norm, v_mla_w_ukv, v_w_o, v_ln1_g, v_ln1_b, v_ffn_w_up, v_ffn_conv_w, v_ffn_conv_b, v_ffn_w_down, v_ln2_g, v_ln2_b):
    given = dict(x=x, c=c, ctx=ctx, c_ctx=c_ctx, ada_w=ada_w, ada_b=ada_b, w_in=w_in, ret_decay_fwd=ret_decay_fwd, ret_decay_bwd=ret_decay_bwd, swa_sink=swa_sink, mla_q_norm=mla_q_norm, mla_w_uq=mla_w_uq, mla_kv_norm=mla_kv_norm, mla_w_ukv=mla_w_ukv, w_o=w_o, ln1_g=ln1_g, ln1_b=ln1_b, ffn_w_up=ffn_w_up, ffn_conv_w=ffn_conv_w, ffn_conv_b=ffn_conv_b, ffn_w_down=ffn_w_down, ln2_g=ln2_g, ln2_b=ln2_b, loss_target=loss_target, m_c_ctx=m_c_ctx, m_ada_w=m_ada_w, m_ada_b=m_ada_b, m_w_in=m_w_in, m_ret_decay_fwd=m_ret_decay_fwd, m_ret_decay_bwd=m_ret_decay_bwd, m_swa_sink=m_swa_sink, m_mla_q_norm=m_mla_q_norm, m_mla_w_uq=m_mla_w_uq, m_mla_kv_norm=m_mla_kv_norm, m_mla_w_ukv=m_mla_w_ukv, m_w_o=m_w_o, m_ln1_g=m_ln1_g, m_ln1_b=m_ln1_b, m_ffn_w_up=m_ffn_w_up, m_ffn_conv_w=m_ffn_conv_w, m_ffn_conv_b=m_ffn_conv_b, m_ffn_w_down=m_ffn_w_down, m_ln2_g=m_ln2_g, m_ln2_b=m_ln2_b, v_c_ctx=v_c_ctx, v_ada_w=v_ada_w, v_ada_b=v_ada_b, v_w_in=v_w_in, v_ret_decay_fwd=v_ret_decay_fwd, v_ret_decay_bwd=v_ret_decay_bwd, v_swa_sink=v_swa_sink, v_mla_q_norm=v_mla_q_norm, v_mla_w_uq=v_mla_w_uq, v_mla_kv_norm=v_mla_kv_norm, v_mla_w_ukv=v_mla_w_ukv, v_w_o=v_w_o, v_ln1_g=v_ln1_g, v_ln1_b=v_ln1_b, v_ffn_w_up=v_ffn_w_up, v_ffn_conv_w=v_ffn_conv_w, v_ffn_conv_b=v_ffn_conv_b, v_ffn_w_down=v_ffn_w_down, v_ln2_g=v_ln2_g, v_ln2_b=v_ln2_b)
    weights = {n: given[n] for n in TWIN_WEIGHTS}
    shared = {n: given[n] for n in SHARED_INPUTS}
    per_example = {n: given[n] for n in ['x', 'c', 'ctx']}
    grad_fn = _jax.value_and_grad(_loss, argnums=(0, 1))

    def one_microbatch(ex, loss_target):
        ex = dict(ex)
        diff = ex.pop(TWIN_DIFF_INPUT)
        return grad_fn(weights, diff, {**shared, **ex}, loss_target)

    if N_MICROBATCH == 1:
        loss, (grad_w, grad_x) = one_microbatch(per_example, given["loss_target"])
    else:
        def body(carry, xs):
            loss_sum, grad_sum = carry
            l_k, (gw_k, gx_k) = one_microbatch(xs[0], xs[1])
            with _jax.named_scope("update"):
                return (loss_sum + l_k, _jax.tree.map(_jnp.add, grad_sum, gw_k)), gx_k

        init = (_jnp.zeros((), _jnp.float32), _jax.tree.map(_jnp.zeros_like, weights))
        (loss, grad_w), grad_x = _jax.lax.scan(body, init, (per_example, given["loss_target"]))
    with _jax.named_scope("update"):
        delta_w, new_m, new_v = {}, {}, {}
        for n in TWIN_WEIGHTS:
            delta_w[n], new_m[n], new_v[n] = _adamw(weights[n], grad_w[n], given["m_" + n], given["v_" + n])
    return (loss, grad_x, *[grad_w[n] for n in TWIN_WEIGHTS], *[delta_w[n] for n in TWIN_WEIGHTS],
            *[new_m[n] for n in TWIN_WEIGHTS], *[new_v[n] for n in TWIN_WEIGHTS])
```

```python
import functools
import math

import numpy as np
import jax
import jax.numpy as jnp
from jax import lax
from jax.experimental import pallas as pl
from jax.experimental.pallas import tpu as pltpu

F32 = jnp.float32
_MXU_DTYPE = jnp.bfloat16
_VMEM_LIMIT_BYTES = 56 * 1024 * 1024
_LANES = 128
_N_DEV = 8
_MOD_ROWS = 128

DEPTH = 4
HEAD_DIM = 128
ROPE_THETA = 10000.0
GRID_W = 64
RET_HEADS = 4
RET_DIM = RET_HEADS * HEAD_DIM
RET_CHUNK = 128
SWA_HEADS = 6
SWA_KV_HEADS = 2
SWA_WINDOW = 128
MLA_HEADS = 6
MLA_Q_RANK = 512
MLA_KV_RANK = 256
MLA_NOPE = 128
MLA_ROPE = 64
MLA_V = 128
MLA_SCALE = (MLA_NOPE + MLA_ROPE) ** -0.5
N_MOD = 6
LN_EPS = 1e-5
RMS_EPS = 1e-6
NEG_INF = -1e30
ALPHA = (2 * DEPTH) ** 0.25
IN_WIDTH = 4160
IN_WIDTH_PAD = 4224

ADAM_LR = 0.001
ADAM_B1 = 0.9
ADAM_B2 = 0.999
ADAM_EPS = 1e-08
ADAM_WD = 0.01
ADAM_STEP = 10

WEIGHTS = ['c_ctx', 'ada_w', 'ada_b', 'w_in', 'ret_decay_fwd', 'ret_decay_bwd', 'swa_sink', 'mla_q_norm',
           'mla_w_uq', 'mla_kv_norm', 'mla_w_ukv', 'w_o', 'ln1_g', 'ln1_b', 'ffn_w_up', 'ffn_conv_w',
           'ffn_conv_b', 'ffn_w_down', 'ln2_g', 'ln2_b']
GATHERED = ['w_in', 'mla_w_uq', 'mla_w_ukv', 'w_o', 'ffn_w_up', 'ffn_conv_w', 'ffn_w_down']
GATHER_AXIS = {'w_in': 1, 'mla_w_uq': 1, 'mla_w_ukv': 1, 'w_o': 0, 'ffn_w_up': 1, 'ffn_conv_w': 1, 'ffn_w_down': 0}
REPLICATED = ['c_ctx', 'ada_b', 'ret_decay_fwd', 'ret_decay_bwd', 'swa_sink', 'mla_q_norm', 'mla_kv_norm',
              'ln1_g', 'ln1_b', 'ffn_conv_b', 'ln2_g', 'ln2_b']


def _cparams(semantics):
    return pltpu.CompilerParams(dimension_semantics=semantics, vmem_limit_bytes=_VMEM_LIMIT_BYTES)


def _pick(n, target, align):
    best = None
    d = align
    while d <= min(n, target):
        if n % d == 0:
            best = d
        d += align
    return n if best is None else best


def _matmul(a, b, mode, name):
    if mode == 'nn':
        (m, k), (k2, n) = a.shape, b.shape
    elif mode == 'nt':
        (m, k), (n, k2) = a.shape, b.shape
    else:
        (k, m), (k2, n) = a.shape, b.shape
    assert k == k2, (a.shape, b.shape, mode)
    tm = _pick(m, 1024, 128)
    tn = _pick(n, 1536, 128)
    tk = _pick(k, 768 if mode == 'tn' else 512, 128)
    nk = k // tk

    def body(a_ref, b_ref, o_ref, acc_ref):
        kk = pl.program_id(2)

        @pl.when(kk == 0)
        def _():
            acc_ref[...] = jnp.zeros_like(acc_ref)

        if mode == 'nn':
            part = jnp.dot(a_ref[...].astype(_MXU_DTYPE), b_ref[...].astype(_MXU_DTYPE),
                           preferred_element_type=F32)
        elif mode == 'nt':
            part = lax.dot_general(a_ref[...].astype(_MXU_DTYPE), b_ref[...].astype(_MXU_DTYPE),
                                   (((1,), (1,)), ((), ())), preferred_element_type=F32)
        else:
            at = a_ref[...].astype(F32).T.astype(_MXU_DTYPE)
            part = jnp.dot(at, b_ref[...].astype(_MXU_DTYPE), preferred_element_type=F32)
        acc_ref[...] += part

        @pl.when(kk == nk - 1)
        def _():
            o_ref[...] = acc_ref[...]

    if mode == 'nn':
        a_spec = pl.BlockSpec((tm, tk), lambda i, j, kk: (i, kk))
        b_spec = pl.BlockSpec((tk, tn), lambda i, j, kk: (kk, j))
    elif mode == 'nt':
        a_spec = pl.BlockSpec((tm, tk), lambda i, j, kk: (i, kk))
        b_spec = pl.BlockSpec((tn, tk), lambda i, j, kk: (j, kk))
    else:
        a_spec = pl.BlockSpec((tk, tm), lambda i, j, kk: (kk, i))
        b_spec = pl.BlockSpec((tk, tn), lambda i, j, kk: (kk, j))
    return pl.pallas_call(
        body, name=name,
        grid=(m // tm, n // tn, nk),
        in_specs=[a_spec, b_spec],
        out_specs=pl.BlockSpec((tm, tn), lambda i, j, kk: (i, j)),
        out_shape=jax.ShapeDtypeStruct((m, n), F32),
        scratch_shapes=[pltpu.VMEM((tm, tn), F32)],
        compiler_params=_cparams(("parallel", "parallel", "arbitrary")),
    )(a, b)


def _linear(name):
    @jax.custom_vjp
    def op(a, w):
        return _matmul(a.astype(_MXU_DTYPE), w.astype(_MXU_DTYPE), 'nn', name + "_fwd")

    def fwd(a, w):
        ab, wb = a.astype(_MXU_DTYPE), w.astype(_MXU_DTYPE)
        return _matmul(ab, wb, 'nn', name + "_fwd"), (ab, wb)

    def bwd(res, g):
        ab, wb = res
        gb = g.astype(_MXU_DTYPE)
        da = _matmul(gb, wb, 'nt', name + "_da")
        dw = _matmul(ab, gb, 'tn', name + "_dw")
        return da, dw

    op.defvjp(fwd, bwd)
    return op


def _pieces(parts):
    out = []
    for p, (start, width) in enumerate(parts):
        pw = math.gcd(start, width) if start else width
        assert pw % _LANES == 0, (start, width)
        for t in range(width // pw):
            out.append((p, pw, start // pw + t))
    return out


def _rowwise(fn, name, rows, parts, diff, pdiff, out_widths, tile, n_ctx_tiles, col_tile=None):
    pieces = [_pieces(p) for p in parts]

    def sel_of(i, n_sel):
        return jnp.where(i >= n_ctx_tiles, n_sel - 1, 0)

    def in_specs_for(row_arrays, params):
        specs, operands = [], []
        for r in range(rows):
            for (_, pw, blk) in pieces[r]:
                if col_tile is None:
                    specs.append(pl.BlockSpec((tile, pw), lambda j, i, blk=blk: (i, blk)))
                else:
                    nb = pw // col_tile
                    specs.append(pl.BlockSpec((tile, col_tile), lambda j, i, blk=blk, nb=nb: (i, blk * nb + j)))
                operands.append(row_arrays[r])
        for p in params:
            n_sel, _, w = p.shape
            cw = w if col_tile is None else col_tile
            if col_tile is None:
                specs.append(pl.BlockSpec((None, 1, cw), lambda j, i, n_sel=n_sel: (sel_of(i, n_sel), 0, 0)))
            else:
                specs.append(pl.BlockSpec((None, 1, cw), lambda j, i, n_sel=n_sel: (sel_of(i, n_sel), 0, j)))
            operands.append(p)
        return specs, operands

    def load_inputs(refs):
        k = 0
        vals = []
        for r in range(rows):
            got = [[] for _ in parts[r]]
            for (p, _, _) in pieces[r]:
                got[p].append(refs[k][...].astype(F32))
                k += 1
            vals.append([g[0] if len(g) == 1 else jnp.concatenate(g, axis=1) for g in got])
        return vals, k

    def forward(row_arrays, params):
        s = row_arrays[0].shape[0]
        ncol = 1 if col_tile is None else out_widths[0] // col_tile
        n_par = len(params)

        def body(*refs):
            vals, k = load_inputs(refs)
            pvals = [refs[k + q][...].astype(F32) for q in range(n_par)]
            outs = fn(vals, pvals)
            for o_ref, o in zip(refs[k + n_par:], outs):
                o_ref[...] = o.astype(o_ref.dtype)

        specs, operands = in_specs_for(row_arrays, params)
        if col_tile is None:
            out_specs = [pl.BlockSpec((tile, w), lambda j, i: (i, 0)) for w in out_widths]
        else:
            out_specs = [pl.BlockSpec((tile, col_tile), lambda j, i: (i, j)) for _ in out_widths]
        return pl.pallas_call(
            body, name=name + "_fwd",
            grid=(ncol, s // tile),
            in_specs=specs, out_specs=out_specs,
            out_shape=[jax.ShapeDtypeStruct((s, w), F32) for w in out_widths],
            compiler_params=_cparams(("arbitrary", "arbitrary")),
        )(*operands)

    def backward(row_arrays, params, cts):
        s = row_arrays[0].shape[0]
        ncol = 1 if col_tile is None else out_widths[0] // col_tile
        n_par = len(params)
        n_out = len(out_widths)
        d_rows = [r for r in range(rows) if diff[r]]
        d_pars = [q for q in range(n_par) if pdiff[q]]

        def body(*refs):
            i = pl.program_id(1)
            vals, k = load_inputs(refs)
            pvals = [refs[k + q][...].astype(F32) for q in range(n_par)]
            k += n_par
            ct_vals = [refs[k + o][...].astype(F32) for o in range(n_out)]
            k += n_out
            drow_refs = refs[k:k + len(d_rows)]
            dpar_refs = refs[k + len(d_rows):]

            def f(dv, dp):
                full_v = list(vals)
                for r, v in zip(d_rows, dv):
                    full_v[r] = v
                full_p = list(pvals)
                for q, v in zip(d_pars, dp):
                    full_p[q] = v
                return fn(full_v, full_p)

            _, vjp = jax.vjp(f, [vals[r] for r in d_rows], [pvals[q] for q in d_pars])
            g_rows, g_pars = vjp(ct_vals)
            for ref, r, g in zip(drow_refs, d_rows, g_rows):
                covered = sum(w for (_, w) in parts[r])
                if col_tile is None:
                    if covered != ref.shape[1]:
                        ref[...] = jnp.zeros_like(ref)
                    for (start, width), gp in zip(parts[r], g):
                        ref[:, start:start + width] = gp
                else:
                    ref[...] = g[0]
            for ref, q, g in zip(dpar_refs, d_pars, g_pars):
                n_sel = params[q].shape[0]
                first = (i == 0) if n_sel == 1 else ((i == 0) | (i == n_ctx_tiles))

                @pl.when(first)
                def _(ref=ref):
                    ref[...] = jnp.zeros_like(ref)

                ref[...] += g

        specs, operands = in_specs_for(row_arrays, params)
        for o, w in enumerate(out_widths):
            if col_tile is None:
                specs.append(pl.BlockSpec((tile, w), lambda j, i: (i, 0)))
            else:
                specs.append(pl.BlockSpec((tile, col_tile), lambda j, i: (i, j)))
            operands.append(cts[o])
        out_specs, out_shape = [], []
        for r in d_rows:
            w = row_arrays[r].shape[1]
            if col_tile is None:
                out_specs.append(pl.BlockSpec((tile, w), lambda j, i: (i, 0)))
            else:
                assert len(parts[r]) == 1 and parts[r][0] == (0, w)
                out_specs.append(pl.BlockSpec((tile, col_tile), lambda j, i: (i, j)))
            out_shape.append(jax.ShapeDtypeStruct((s, w), F32))
        for q in d_pars:
            n_sel, _, w = params[q].shape
            cw = w if col_tile is None else col_tile
            if col_tile is None:
                out_specs.append(pl.BlockSpec((None, 1, cw), lambda j, i, n_sel=n_sel: (sel_of(i, n_sel), 0, 0)))
            else:
                out_specs.append(pl.BlockSpec((None, 1, cw), lambda j, i, n_sel=n_sel: (sel_of(i, n_sel), 0, j)))
            out_shape.append(jax.ShapeDtypeStruct((n_sel, 1, w), F32))
        res = pl.pallas_call(
            body, name=name + "_bwd",
            grid=(ncol, s // tile),
            in_specs=specs, out_specs=out_specs, out_shape=out_shape,
            compiler_params=_cparams(("arbitrary", "arbitrary")),
        )(*operands)
        g_rows = [None] * rows
        for r, g in zip(d_rows, res[:len(d_rows)]):
            g_rows[r] = g
        g_pars = [None] * n_par
        for q, g in zip(d_pars, res[len(d_rows):]):
            g_pars[q] = g
        return g_rows, g_pars

    @jax.custom_vjp
    def op(row_arrays, params):
        return tuple(forward(list(row_arrays), list(params)))

    def op_fwd(row_arrays, params):
        return tuple(forward(list(row_arrays), list(params))), (row_arrays, params)

    def op_bwd(res, cts):
        row_arrays, params = res
        g_rows, g_pars = backward(list(row_arrays), list(params), list(cts))
        g_rows = tuple(jnp.zeros_like(a) if g is None else g for a, g in zip(row_arrays, g_rows))
        g_pars = tuple(jnp.zeros_like(a) if g is None else g for a, g in zip(params, g_pars))
        return g_rows, g_pars

    op.defvjp(op_fwd, op_bwd)
    return op


def _rot_impl(x, quarter):
    lane = lax.broadcasted_iota(jnp.int32, (x.shape[0], _LANES), 1)
    even = ((lane // quarter) % 2) == 0
    outs = []
    for k in range(x.shape[1] // _LANES):
        xs = x[:, k * _LANES:(k + 1) * _LANES]
        left = pltpu.roll(xs, _LANES - quarter, 1)
        right = pltpu.roll(xs, quarter, 1)
        outs.append(jnp.where(even, -left, right))
    return outs[0] if len(outs) == 1 else jnp.concatenate(outs, axis=1)


def _make_rot(quarter):
    @jax.custom_vjp
    def rot(x):
        return _rot_impl(x, quarter)

    rot.defvjp(lambda x: (_rot_impl(x, quarter), None), lambda _, g: (-_rot_impl(g, quarter),))
    return rot


_rot32 = _make_rot(32)
_rot16 = _make_rot(16)


def _tile_lanes(t, n):
    return t if n == 1 else jnp.concatenate([t] * n, axis=1)


def _rope(x, cos, sin, rot):
    n = x.shape[1] // _LANES
    return x * _tile_lanes(cos, n) + rot(x) * _tile_lanes(sin, n)


def _rms(x):
    return x * lax.rsqrt(jnp.mean(x * x, axis=-1, keepdims=True) + RMS_EPS)


def _ln(x):
    mu = jnp.mean(x, axis=-1, keepdims=True)
    xc = x - mu
    var = jnp.mean(xc * xc, axis=-1, keepdims=True)
    return xc * lax.rsqrt(var + LN_EPS)


def _sum_all(x):
    return jnp.sum(jnp.sum(x, axis=1, keepdims=True), axis=0, keepdims=True)


def _silu(x):
    return x * (1.0 / (1.0 + jnp.exp(-x)))


def _fn_modulate(vals, pars):
    (s,), = vals
    shift, scale = pars
    return [s * (1.0 + scale) + shift]


def _fn_postproj(vals, pars):
    (rq, rk, rv, rg, sq, sk, sv, mcq, mckv, mkr), (cos_h,), (sin_h,), (cos_m,), (sin_m,) = vals
    q_norm, kv_norm = pars
    k_scale = HEAD_DIM ** -0.5
    return [_rope(rq, cos_h, sin_h, _rot32), _rope(rk, cos_h, sin_h, _rot32) * k_scale, rv, rg,
            _rope(sq, cos_h, sin_h, _rot32), _rope(sk, cos_h, sin_h, _rot32), sv,
            _rms(mcq) * q_norm, _rms(mckv) * kv_norm, _rope(mkr, cos_m, sin_m, _rot16)]


def _fn_mla_assemble(vals, pars):
    (q_lin,), kn_v, (kr,), (cos_m,), (sin_m,) = vals
    kn, vv = kn_v[:MLA_HEADS], kn_v[MLA_HEADS]
    ones, zeros = jnp.ones_like(cos_m), jnp.zeros_like(sin_m)
    cos_q = jnp.concatenate([ones, cos_m] * MLA_HEADS, axis=1)
    sin_q = jnp.concatenate([zeros, sin_m] * MLA_HEADS, axis=1)
    q_full = q_lin * cos_q + _rot16(q_lin) * sin_q
    k_full = jnp.concatenate([t for h in range(MLA_HEADS) for t in (kn[h], kr)], axis=1)
    return [q_full, k_full, vv]


def _fn_mix(vals, pars):
    ret_f, ret_b, rg, (y_swa,), (y_mla,) = vals
    heads = [_silu(rg[h]) * _rms(ret_f[h] + ret_b[h]) for h in range(RET_HEADS)]
    return [jnp.concatenate(heads + [y_swa, y_mla], axis=1)]


def _fn_ln1(vals, pars):
    (s,), (mix,) = vals
    gate, g, b, shift_f, scale_f = pars
    x_a = _ln(ALPHA * s + (1.0 + gate) * mix) * g + b
    return [x_a, x_a * (1.0 + scale_f) + shift_f]


def _fn_ln2(vals, pars):
    (x_a,), (f,) = vals
    gate, g, b = pars
    return [_ln(ALPHA * x_a + (1.0 + gate) * f) * g + b]


def _fn_conv_gate(vals, pars):
    (u,), (g,), (gp,), (gn,) = vals
    w0, w1, w2, b = pars
    return [_silu(w0 * gp + w1 * g + w2 * gn + b) * u]


class _AttnCfg:
    def __init__(self, name, s, t, heads, kv_heads, dk, dv, blk, band, scale, has_sink):
        self.name, self.s, self.t = name, s, t
        self.heads, self.kv_heads, self.group = heads, kv_heads, heads // kv_heads
        self.dk, self.dv, self.blk, self.band, self.scale, self.has_sink = dk, dv, blk, band, scale, has_sink
        self.nq = s // blk
        self.n_ctx = t // blk if band else 0
        self.ks = self.n_ctx + 3 if band else s // blk
        assert s % blk == 0 and (not band or (t % blk == 0 and blk >= SWA_WINDOW))

    def kblock(self, i, st):
        if not self.band:
            return st
        kb = jnp.clip(i + st - self.n_ctx - 1, self.n_ctx, self.nq - 1)
        return jnp.where(st < self.n_ctx, st, kb)

    def valid(self, i, st):
        if not self.band:
            return st >= 0
        kb = i + st - self.n_ctx - 1
        return (st < self.n_ctx) | ((i >= self.n_ctx) & (kb >= self.n_ctx) & (kb <= self.nq - 1))

    def masked(self, i, st):
        if self.band:
            return st >= self.n_ctx
        return i * self.blk < self.t

    def visible(self, i, kb):
        b = self.blk
        qpos = i * b + lax.broadcasted_iota(jnp.int32, (b, b), 0)
        kpos = kb * b + lax.broadcasted_iota(jnp.int32, (b, b), 1)
        if self.band:
            return jnp.abs(qpos - kpos) <= SWA_WINDOW
        return (kpos < self.t) | (qpos >= self.t)


def _attn_fwd(cfg, q, k, v, sink):
    b, dk, dv, g = cfg.blk, cfg.dk, cfg.dv, cfg.group

    def body(q_ref, k_ref, v_ref, sink_ref, o_ref, lse_ref, m_sc, l_sc, acc_sc):
        i, st = pl.program_id(1), pl.program_id(2)

        @pl.when(st == 0)
        def _():
            if cfg.has_sink:
                m_sc[...] = jnp.broadcast_to(sink_ref[0:1, :], m_sc.shape)
                l_sc[...] = jnp.ones_like(l_sc)
            else:
                m_sc[...] = jnp.full_like(m_sc, NEG_INF)
                l_sc[...] = jnp.zeros_like(l_sc)
            acc_sc[...] = jnp.zeros_like(acc_sc)

        def step(use_mask):
            sc = lax.dot_general(q_ref[...].astype(_MXU_DTYPE), k_ref[...].astype(_MXU_DTYPE),
                                 (((1,), (1,)), ((), ())), preferred_element_type=F32) * cfg.scale
            if use_mask:
                sc = jnp.where(cfg.visible(i, cfg.kblock(i, st)), sc, NEG_INF)
            m_prev = m_sc[...]
            m_new = jnp.maximum(m_prev, jnp.max(sc, axis=1, keepdims=True))
            alpha = jnp.exp(m_prev - m_new)
            p = jnp.exp(sc - m_new[:, 0:1])
            l_sc[...] = alpha * l_sc[...] + jnp.sum(p, axis=1, keepdims=True)
            acc_sc[...] = acc_sc[...] * alpha[:, 0:1] + jnp.dot(
                p.astype(_MXU_DTYPE), v_ref[...].astype(_MXU_DTYPE), preferred_element_type=F32)
            m_sc[...] = m_new

        ok = cfg.valid(i, st)
        msk = cfg.masked(i, st)
        pl.when(ok & msk)(lambda: step(True))
        pl.when(ok & jnp.logical_not(msk))(lambda: step(False))

        @pl.when(st == cfg.ks - 1)
        def _():
            o_ref[...] = acc_sc[...] / l_sc[:, 0:1]
            lse_ref[...] = m_sc[...] + jnp.log(l_sc[...])

    return pl.pallas_call(
        body, name=cfg.name + "_fwd",
        grid=(cfg.heads, cfg.nq, cfg.ks),
        in_specs=[pl.BlockSpec((b, dk), lambda h, i, st: (i, h)),
                  pl.BlockSpec((b, dk), lambda h, i, st: (cfg.kblock(i, st), h // g)),
                  pl.BlockSpec((b, dv), lambda h, i, st: (cfg.kblock(i, st), h // g)),
                  pl.BlockSpec((None, 8, _LANES), lambda h, i, st: (h, 0, 0))],
        out_specs=[pl.BlockSpec((b, dv), lambda h, i, st: (i, h)),
                   pl.BlockSpec((None, b, _LANES), lambda h, i, st: (h, i, 0))],
        out_shape=[jax.ShapeDtypeStruct((cfg.s, cfg.heads * dv), F32),
                   jax.ShapeDtypeStruct((cfg.heads, cfg.s, _LANES), F32)],
        scratch_shapes=[pltpu.VMEM((b, _LANES), F32), pltpu.VMEM((b, _LANES), F32), pltpu.VMEM((b, dv), F32)],
        compiler_params=_cparams(("parallel", "parallel", "arbitrary")),
    )(q, k, v, sink)


def _attn_bwd(cfg, q, k, v, sink, o, lse, do):
    b, dk, dv, g = cfg.blk, cfg.dk, cfg.dv, cfg.group

    def body(q_ref, k_ref, v_ref, sink_ref, o_ref, lse_ref, do_ref, dq_ref, dk_ref, dv_ref, dsink_ref,
             dq_sc, delta_sc):
        i, st = pl.program_id(1), pl.program_id(2)

        @pl.when((i == 0) & (st == 0))
        def _():
            dk_ref[...] = jnp.zeros_like(dk_ref)
            dv_ref[...] = jnp.zeros_like(dv_ref)

        @pl.when(st == 0)
        def _():
            delta = jnp.sum(do_ref[...] * o_ref[...], axis=1, keepdims=True)
            delta_sc[...] = jnp.broadcast_to(delta, delta_sc.shape)
            dq_sc[...] = jnp.zeros_like(dq_sc)
            if cfg.has_sink:
                ps = jnp.exp(sink_ref[0:1, :] - lse_ref[...]) * delta_sc[...]
                dsink_ref[...] = jnp.broadcast_to(-jnp.sum(ps, axis=0, keepdims=True), dsink_ref.shape)
            else:
                dsink_ref[...] = jnp.zeros_like(dsink_ref)

        def step(use_mask):
            kb = cfg.kblock(i, st)
            qb = q_ref[...].astype(_MXU_DTYPE)
            kbv = k_ref[...].astype(_MXU_DTYPE)
            dob = do_ref[...].astype(_MXU_DTYPE)
            sc = lax.dot_general(qb, kbv, (((1,), (1,)), ((), ())), preferred_element_type=F32) * cfg.scale
            if use_mask:
                sc = jnp.where(cfg.visible(i, kb), sc, NEG_INF)
            p = jnp.exp(sc - lse_ref[:, 0:1])
            dp = lax.dot_general(dob, v_ref[...].astype(_MXU_DTYPE), (((1,), (1,)), ((), ())),
                                 preferred_element_type=F32)
            ds = p * (dp - delta_sc[:, 0:1]) * cfg.scale
            dq_sc[...] += jnp.dot(ds.astype(_MXU_DTYPE), kbv, preferred_element_type=F32)
            rows = pl.ds(pl.multiple_of(kb * b, b), b)
            dk_ref[rows, :] += jnp.dot(ds.T.astype(_MXU_DTYPE), qb, preferred_element_type=F32)
            dv_ref[rows, :] += jnp.dot(p.T.astype(_MXU_DTYPE), dob, preferred_element_type=F32)

        ok = cfg.valid(i, st)
        msk = cfg.masked(i, st)
        pl.when(ok & msk)(lambda: step(True))
        pl.when(ok & jnp.logical_not(msk))(lambda: step(False))

        @pl.when(st == cfg.ks - 1)
        def _():
            dq_ref[...] = dq_sc[...]

    return pl.pallas_call(
        body, name=cfg.name + "_bwd",
        grid=(cfg.heads, cfg.nq, cfg.ks),
        in_specs=[pl.BlockSpec((b, dk), lambda h, i, st: (i, h)),
                  pl.BlockSpec((b, dk), lambda h, i, st: (cfg.kblock(i, st), h // g)),
                  pl.BlockSpec((b, dv), lambda h, i, st: (cfg.kblock(i, st), h // g)),
                  pl.BlockSpec((None, 8, _LANES), lambda h, i, st: (h, 0, 0)),
                  pl.BlockSpec((b, dv), lambda h, i, st: (i, h)),
                  pl.BlockSpec((None, b, _LANES), lambda h, i, st: (h, i, 0)),
                  pl.BlockSpec((b, dv), lambda h, i, st: (i, h))],
        out_specs=[pl.BlockSpec((b, dk), lambda h, i, st: (i, h)),
                   pl.BlockSpec((cfg.s, dk), lambda h, i, st: (0, h)),
                   pl.BlockSpec((cfg.s, dv), lambda h, i, st: (0, h)),
                   pl.BlockSpec((None, None, 8, _LANES), lambda h, i, st: (h, i, 0, 0))],
        out_shape=[jax.ShapeDtypeStruct((cfg.s, cfg.heads * dk), F32),
                   jax.ShapeDtypeStruct((cfg.s, cfg.heads * dk), F32),
                   jax.ShapeDtypeStruct((cfg.s, cfg.heads * dv), F32),
                   jax.ShapeDtypeStruct((cfg.heads, cfg.nq, 8, _LANES), F32)],
        scratch_shapes=[pltpu.VMEM((b, dk), F32), pltpu.VMEM((b, _LANES), F32)],
        compiler_params=_cparams(("parallel", "arbitrary", "arbitrary")),
    )(q, k, v, sink, o, lse, do)


def _attention(cfg):
    def sink_rows(sink):
        return jnp.broadcast_to(sink.astype(F32)[:, None, None], (cfg.heads, 8, _LANES))

    @jax.custom_vjp
    def op(q, k, v, sink):
        return _attn_fwd(cfg, q.astype(_MXU_DTYPE), k.astype(_MXU_DTYPE), v.astype(_MXU_DTYPE), sink_rows(sink))[0]

    def fwd(q, k, v, sink):
        qb, kb, vb = q.astype(_MXU_DTYPE), k.astype(_MXU_DTYPE), v.astype(_MXU_DTYPE)
        sr = sink_rows(sink)
        o, lse = _attn_fwd(cfg, qb, kb, vb, sr)
        return o, (qb, kb, vb, sr, o, lse)

    def bwd(res, do):
        qb, kb, vb, sr, o, lse = res
        dq, dk_h, dv_h, dsink = _attn_bwd(cfg, qb, kb, vb, sr, o, lse, do)
        dk = dk_h.reshape(cfg.s, cfg.kv_heads, cfg.group, cfg.dk).sum(2).reshape(cfg.s, cfg.kv_heads * cfg.dk)
        dv = dv_h.reshape(cfg.s, cfg.kv_heads, cfg.group, cfg.dv).sum(2).reshape(cfg.s, cfg.kv_heads * cfg.dv)
        return dq, dk, dv, dsink[:, :, 0, 0].sum(1)

    op.defvjp(fwd, bwd)
    return op


def _ret_chunk_order(n, d, n_ctx, n_all):
    fwd = n
    bwd = jnp.where(n < n_ctx, n_ctx - 1 - n, n_all - 1 - (n - n_ctx))
    return jnp.where(d == 0, fwd, bwd)


def _ret_decays(lg, d):
    c = RET_CHUNK
    i = lax.broadcasted_iota(jnp.int32, (c, c), 0)
    j = lax.broadcasted_iota(jnp.int32, (c, c), 1)
    sign = (1 - 2 * d)
    diff = ((i - j) * sign).astype(F32)
    intra = jnp.where(diff >= 0, jnp.exp(lg * jnp.maximum(diff, 0.0)), 0.0)
    pos = lax.broadcasted_iota(jnp.int32, (c, 1), 0)
    r = (pos + d * (c - 1 - 2 * pos)).astype(F32)
    qd = jnp.exp(lg * (r + 1.0))
    kd = jnp.exp(lg * (c - 1.0 - r))
    cd = jnp.exp(lg * c)
    return intra, qd, kd, cd, diff, r


def _mxu_dot(a, b, dims=None):
    a, b = a.astype(_MXU_DTYPE), b.astype(_MXU_DTYPE)
    if dims is None:
        return jnp.dot(a, b, preferred_element_type=F32)
    return lax.dot_general(a, b, dims, preferred_element_type=F32)


_NT = (((1,), (1,)), ((), ()))


def _ret_fwd(q, k, v, lg, n_ctx, name):
    s = q.shape[0]
    c, hd = RET_CHUNK, HEAD_DIM
    n_all = s // c

    def body(lg_ref, q_ref, k_ref, v_ref, o_ref, st_ref, state_sc):
        d, h, n = pl.program_id(0), pl.program_id(1), pl.program_id(2)

        @pl.when(n == 0)
        def _():
            state_sc[...] = jnp.zeros_like(state_sc)

        intra, qd, kd, cd, _, _ = _ret_decays(lg_ref[d, h], d)
        qv, kv, vv = q_ref[...], k_ref[...], v_ref[...]
        s_in = state_sc[...]
        st_ref[...] = s_in
        p = _mxu_dot(qv, kv, _NT) * intra
        o_ref[...] = _mxu_dot(p, vv) + _mxu_dot(qv * qd, s_in)
        state_sc[...] = cd * s_in + _mxu_dot((kv * kd).T, vv)

    def chunk_spec():
        return pl.BlockSpec((c, hd), lambda d, h, n: (_ret_chunk_order(n, d, n_ctx, n_all), h))

    return pl.pallas_call(
        body, name=name + "_fwd",
        grid=(2, RET_HEADS, n_all),
        in_specs=[pl.BlockSpec(memory_space=pltpu.SMEM), chunk_spec(), chunk_spec(), chunk_spec()],
        out_specs=[pl.BlockSpec((None, c, hd), lambda d, h, n: (d, _ret_chunk_order(n, d, n_ctx, n_all), h)),
                   pl.BlockSpec((None, None, None, hd, hd), lambda d, h, n: (d, h, n, 0, 0))],
        out_shape=[jax.ShapeDtypeStruct((2, s, RET_DIM), F32),
                   jax.ShapeDtypeStruct((2, RET_HEADS, n_all, hd, hd), F32)],
        scratch_shapes=[pltpu.VMEM((hd, hd), F32)],
        compiler_params=_cparams(("arbitrary", "arbitrary", "arbitrary")),
    )(lg, q, k, v)


def _ret_bwd(q, k, v, lg, states, dout, n_ctx, name):
    s = q.shape[0]
    c, hd = RET_CHUNK, HEAD_DIM
    n_all = s // c

    def body(lg_ref, q_ref, k_ref, v_ref, st_ref, do_ref, dq_ref, dk_ref, dv_ref, dlg_ref, ds_sc):
        d, h, n = pl.program_id(0), pl.program_id(1), pl.program_id(2)

        @pl.when(n == 0)
        def _():
            ds_sc[...] = jnp.zeros_like(ds_sc)
            dlg_ref[...] = jnp.zeros_like(dlg_ref)

        intra, qd, kd, cd, diff, r = _ret_decays(lg_ref[d, h], d)
        qv, kv, vv, do = q_ref[...], k_ref[...], v_ref[...], do_ref[...]
        s_in = st_ref[...]
        ds_out = ds_sc[...]
        sc = _mxu_dot(qv, kv, _NT)
        p = sc * intra
        dp = _mxu_dot(do, vv, _NT)
        dsc = dp * intra
        dqs = _mxu_dot(do, s_in, _NT)
        dkk = _mxu_dot(vv, ds_out, _NT)
        dq_ref[...] = _mxu_dot(dsc, kv) + dqs * qd
        dk_ref[...] = _mxu_dot(dsc.T, qv) + dkk * kd
        dv_ref[...] = _mxu_dot(p.T, do) + _mxu_dot(kv * kd, ds_out)
        ds_sc[...] = cd * ds_out + _mxu_dot((qv * qd).T, do)
        dlg = (_sum_all(dp * p * diff)
               + _sum_all(jnp.sum(dqs * qv, axis=1, keepdims=True) * qd * (r + 1.0))
               + _sum_all(jnp.sum(dkk * kv, axis=1, keepdims=True) * kd * (c - 1.0 - r))
               + _sum_all(ds_out * s_in) * (cd * c))
        dlg_ref[...] += jnp.broadcast_to(dlg, dlg_ref.shape)

    def order(n, d):
        return _ret_chunk_order(n_all - 1 - n, d, n_ctx, n_all)

    def chunk_spec():
        return pl.BlockSpec((c, hd), lambda d, h, n: (order(n, d), h))

    def dir_spec():
        return pl.BlockSpec((None, c, hd), lambda d, h, n: (d, order(n, d), h))

    return pl.pallas_call(
        body, name=name + "_bwd",
        grid=(2, RET_HEADS, n_all),
        in_specs=[pl.BlockSpec(memory_space=pltpu.SMEM), chunk_spec(), chunk_spec(), chunk_spec(),
                  pl.BlockSpec((None, None, None, hd, hd), lambda d, h, n: (d, h, n_all - 1 - n, 0, 0)),
                  dir_spec()],
        out_specs=[dir_spec(), dir_spec(), dir_spec(),
                   pl.BlockSpec((None, None, 8, _LANES), lambda d, h, n: (d, h, 0, 0))],
        out_shape=[jax.ShapeDtypeStruct((2, s, RET_DIM), F32)] * 3
        + [jax.ShapeDtypeStruct((2, RET_HEADS, 8, _LANES), F32)],
        scratch_shapes=[pltpu.VMEM((hd, hd), F32)],
        compiler_params=_cparams(("arbitrary", "arbitrary", "arbitrary")),
    )(lg, q, k, v, states, dout)


def _retention(n_ctx, name):
    @jax.custom_vjp
    def op(q, k, v, lg):
        return _ret_fwd(q, k, v, lg, n_ctx, name)[0]

    def fwd(q, k, v, lg):
        out, states = _ret_fwd(q, k, v, lg, n_ctx, name)
        return out, (q, k, v, lg, states)

    def bwd(res, dout):
        q, k, v, lg, states = res
        dq, dk, dv, dlg = _ret_bwd(q, k, v, lg, states, dout, n_ctx, name)
        return dq[0] + dq[1], dk[0] + dk[1], dv[0] + dv[1], dlg[:, :, 0, 0]

    op.defvjp(fwd, bwd)
    return op


def _loss_call(y, target, n_ctx_tiles, tile, name):
    s, dm = y.shape

    def body(y_ref, t_ref, loss_ref, dy_ref):
        i = pl.program_id(0)

        @pl.when(i == 0)
        def _():
            loss_ref[...] = jnp.zeros_like(loss_ref)

        @pl.when(i < n_ctx_tiles)
        def _():
            dy_ref[...] = jnp.zeros_like(dy_ref)

        @pl.when(i >= n_ctx_tiles)
        def _():
            err = y_ref[...] - t_ref[...]
            dy_ref[...] = err * (1.0 / dm)
            loss_ref[...] += jnp.broadcast_to(_sum_all(err * err) * (0.5 / dm), loss_ref.shape)

    return pl.pallas_call(
        body, name=name,
        grid=(s // tile,),
        in_specs=[pl.BlockSpec((tile, dm), lambda i: (i, 0)),
                  pl.BlockSpec((tile, dm), lambda i: (jnp.maximum(i - n_ctx_tiles, 0), 0))],
        out_specs=[pl.BlockSpec((8, _LANES), lambda i: (0, 0)),
                   pl.BlockSpec((tile, dm), lambda i: (i, 0))],
        out_shape=[jax.ShapeDtypeStruct((8, _LANES), F32), jax.ShapeDtypeStruct((s, dm), F32)],
        compiler_params=_cparams(("arbitrary",)),
    )(y, target)


def _loss_op(n_ctx_tiles, tile):
    @jax.custom_vjp
    def op(y, target):
        return _loss_call(y, target, n_ctx_tiles, tile, "loss_head")[0][0, 0]

    def fwd(y, target):
        loss, dy = _loss_call(y, target, n_ctx_tiles, tile, "loss_head")
        return loss[0, 0], (dy, target)

    def bwd(res, g):
        dy, target = res
        return dy * g, jnp.zeros_like(target)

    op.defvjp(fwd, bwd)
    return op


def _exchange(x, gather, name):
    blk_shape = x.shape if gather else x.shape[1:]

    def body(x_ref, o_ref, send_sems, recv_sems, local_sem):
        mx, my, mc = lax.axis_index("x"), lax.axis_index("y"), lax.axis_index("c")
        me = 4 * mx + 2 * my + mc
        copies = []
        for rel in range(1, _N_DEV):
            px = mx ^ ((rel >> 2) & 1)
            py = my ^ ((rel >> 1) & 1)
            pc = mc ^ (rel & 1)
            src = x_ref if gather else x_ref.at[4 * px + 2 * py + pc]
            cp = pltpu.make_async_remote_copy(
                src_ref=src, dst_ref=o_ref.at[me],
                send_sem=send_sems.at[rel - 1], recv_sem=recv_sems.at[rel - 1],
                device_id=(px, py, pc), device_id_type=pl.DeviceIdType.MESH)
            cp.start()
            copies.append(cp)
        mine = pltpu.make_async_copy(x_ref if gather else x_ref.at[me], o_ref.at[me], local_sem)
        mine.start()
        for cp in copies:
            cp.wait()
        mine.wait()

    return pl.pallas_call(
        body, name=name,
        in_specs=[pl.BlockSpec(memory_space=pltpu.HBM)],
        out_specs=pl.BlockSpec(memory_space=pltpu.HBM),
        out_shape=jax.ShapeDtypeStruct((_N_DEV,) + tuple(blk_shape), x.dtype),
        scratch_shapes=[pltpu.SemaphoreType.DMA((_N_DEV - 1,)), pltpu.SemaphoreType.DMA((_N_DEV - 1,)),
                        pltpu.SemaphoreType.DMA(())],
    )(x)


def _sum_parts(parts, name):
    _, r, w = parts.shape
    tile = _pick(r, 512, 8)

    def body(p_ref, o_ref):
        acc = p_ref[0]
        for j in range(1, _N_DEV):
            acc = acc + p_ref[j]
        o_ref[...] = acc

    return pl.pallas_call(
        body, name=name, grid=(r // tile,),
        in_specs=[pl.BlockSpec((_N_DEV, tile, w), lambda i: (0, i, 0))],
        out_specs=pl.BlockSpec((tile, w), lambda i: (i, 0)),
        out_shape=jax.ShapeDtypeStruct((r, w), F32),
        compiler_params=_cparams(("arbitrary",)),
    )(parts)


def _all_gather_op(name, payload_dtype):
    def impl(x):
        return _exchange(x.astype(payload_dtype), True, name + "_gather").astype(F32)

    @jax.custom_vjp
    def op(x):
        return impl(x)

    def fwd(x):
        return impl(x), None

    def bwd(_, g):
        return (_sum_parts(_exchange(g, False, name + "_scatter"), name + "_sum"),)

    op.defvjp(fwd, bwd)
    return op


def _adamw(w, g, m, v, partial, name):
    r, wd = w.shape
    tile = _pick(r, max(8, (1 << 20) // (4 * wd) // 8 * 8), 8)
    c1 = 1.0 / (1.0 - ADAM_B1 ** ADAM_STEP)
    c2 = 1.0 / (1.0 - ADAM_B2 ** ADAM_STEP)

    def body(w_ref, g_ref, m_ref, v_ref, go_ref, d_ref, mo_ref, vo_ref):
        if partial:
            g = g_ref[0]
            for j in range(1, _N_DEV):
                g = g + g_ref[j]
        else:
            g = g_ref[...]
        m_new = ADAM_B1 * m_ref[...] + (1.0 - ADAM_B1) * g
        v_new = ADAM_B2 * v_ref[...] + (1.0 - ADAM_B2) * (g * g)
        m_hat = m_new * c1
        v_hat = v_new * c2
        go_ref[...] = g
        d_ref[...] = -ADAM_LR * (m_hat / (jnp.sqrt(v_hat) + ADAM_EPS) + ADAM_WD * w_ref[...])
        mo_ref[...] = m_new
        vo_ref[...] = v_new

    spec = pl.BlockSpec((tile, wd), lambda i: (i, 0))
    g_spec = pl.BlockSpec((_N_DEV, tile, wd), lambda i: (0, i, 0)) if partial else spec
    return pl.pallas_call(
        body, name=name, grid=(r // tile,),
        in_specs=[spec, g_spec, spec, spec],
        out_specs=[spec] * 4,
        out_shape=[jax.ShapeDtypeStruct((r, wd), F32)] * 4,
        compiler_params=_cparams(("arbitrary",)),
    )(w, g, m, v)


def _pack(arrays):
    flat, meta, off = [], [], 0
    for a in arrays:
        n = int(np.prod(a.shape))
        pad = (-n) % _LANES
        flat.append(a.reshape(-1))
        if pad:
            flat.append(jnp.zeros((pad,), a.dtype))
        meta.append((off, a.shape))
        off += n + pad
    pad = (-off) % (8 * _LANES)
    if pad:
        flat.append(jnp.zeros((pad,), arrays[0].dtype))
    return jnp.concatenate(flat).reshape(-1, _LANES), meta


def _unpack(packed, meta):
    flat = packed.reshape(-1)
    return [flat[off:off + int(np.prod(shape))].reshape(shape) for off, shape in meta]


def _rope_tables(t, l, dim, width):
    rows = l // GRID_W
    r = np.repeat(np.arange(rows, dtype=np.float32), GRID_W)
    cc = np.tile(np.arange(GRID_W, dtype=np.float32), rows)
    n_freq = dim // 4
    inv = jnp.asarray(ROPE_THETA, F32) ** (-jnp.arange(n_freq, dtype=F32) / n_freq)
    ang_r = jnp.asarray(r)[:, None] * inv
    ang_c = jnp.asarray(cc)[:, None] * inv
    ang = jnp.concatenate([ang_r, ang_r, ang_c, ang_c], axis=-1)
    cos, sin = jnp.cos(ang), jnp.sin(ang)
    if width > dim:
        cos = jnp.concatenate([cos, jnp.ones((l, width - dim), F32)], axis=1)
        sin = jnp.concatenate([sin, jnp.zeros((l, width - dim), F32)], axis=1)
    cos = jnp.concatenate([jnp.ones((t, width), F32), cos], axis=0)
    sin = jnp.concatenate([jnp.zeros((t, width), F32), sin], axis=0)
    return cos, sin


def _shift_rows(g, t, up):
    def seg(a):
        z = jnp.zeros((1, a.shape[1]), a.dtype)
        return jnp.concatenate([a[1:], z], axis=0) if up else jnp.concatenate([z, a[:-1]], axis=0)
    return jnp.concatenate([seg(g[:t]), seg(g[t:])], axis=0)


def _full_weights(gathered, l, shard_shapes):
    meta = gathered['meta']
    per_dev = [_unpack(gathered['data'][j], meta) for j in range(_N_DEV)]
    full = {}
    for idx, n in enumerate(GATHERED):
        full[n] = jnp.concatenate([per_dev[j][idx] for j in range(_N_DEV)], axis=GATHER_AXIS[n])
    return full


def _layer(l, stream, mod, wts, small, tables, dims, last):
    s, t, dm, dff = dims
    tile = min(256, t)
    n_ctx_tiles = t // tile
    tile_w = min(128, t)
    n_ctx_tiles_w = t // tile_w
    cos_h, sin_h, cos_m, sin_m = tables
    tag = "l%d_" % l

    def mrow(k):
        return mod[:, k:k + 1, :]

    def one(a):
        return a.reshape(1, 1, -1)

    w_in = jnp.concatenate([wts['w_in'], jnp.zeros((dm, IN_WIDTH_PAD - IN_WIDTH), F32)], axis=1)
    modulate = _rowwise(_fn_modulate, tag + "modulate", 1, [[(0, dm)]], [True], [True, True], [dm],
                        tile, n_ctx_tiles)
    (h,) = modulate((stream,), (mrow(0), mrow(1)))
    proj = _linear(tag + "w_in")(h, w_in)

    in_parts = [(0, 512), (512, 512), (1024, 512), (1536, 512), (2048, 768), (2816, 256), (3072, 256),
                (3328, 512), (3840, 256), (4096, 128)]
    postproj = _rowwise(_fn_postproj, tag + "postproj", 5,
                        [in_parts, [(0, 128)], [(0, 128)], [(0, 128)], [(0, 128)]],
                        [True, False, False, False, False], [True, True],
                        [512, 512, 512, 512, 768, 256, 256, 512, 256, 128], tile_w, n_ctx_tiles_w)
    (rq, rk, rv, rg, sq, sk, sv, cqn, ckvn, kr) = postproj(
        (proj, cos_h, sin_h, cos_m, sin_m), (one(small['mla_q_norm'][l]), one(small['mla_kv_norm'][l])))

    lg = jnp.stack([jax.nn.log_sigmoid(small['ret_decay_fwd'][l]), jax.nn.log_sigmoid(small['ret_decay_bwd'][l])])
    ret = _retention(t // RET_CHUNK, tag + "retention")(rq, rk, rv, lg)

    swa_blk = 256 if (t % 256 == 0 and s % 256 == 0) else 128
    swa_cfg = _AttnCfg(tag + "swa", s, t, SWA_HEADS, SWA_KV_HEADS, HEAD_DIM, HEAD_DIM, swa_blk, True,
                       HEAD_DIM ** -0.5, True)
    y_swa = _attention(swa_cfg)(sq, sk, sv, small['swa_sink'][l])

    w_uq = wts['mla_w_uq'].reshape(MLA_Q_RANK, MLA_HEADS, MLA_NOPE + MLA_ROPE)
    w_uq = jnp.concatenate([w_uq, jnp.zeros((MLA_Q_RANK, MLA_HEADS, 256 - MLA_NOPE - MLA_ROPE), F32)], axis=2)
    w_uq = w_uq.reshape(MLA_Q_RANK, MLA_HEADS * 256)
    w_ukv = wts['mla_w_ukv'].reshape(MLA_KV_RANK, MLA_HEADS, MLA_NOPE + MLA_V)
    w_ukv = jnp.concatenate([w_ukv[:, :, :MLA_NOPE].reshape(MLA_KV_RANK, -1),
                             w_ukv[:, :, MLA_NOPE:].reshape(MLA_KV_RANK, -1)], axis=1)
    q_lin = _linear(tag + "w_uq")(cqn, w_uq)
    kv_lin = _linear(tag + "w_ukv")(ckvn, w_ukv)
    kv_parts = [(hh * 128, 128) for hh in range(MLA_HEADS)] + [(MLA_HEADS * 128, MLA_HEADS * 128)]
    assemble = _rowwise(_fn_mla_assemble, tag + "mla_assemble", 5,
                        [[(0, MLA_HEADS * 256)], kv_parts, [(0, 128)], [(0, 128)], [(0, 128)]],
                        [True, True, True, False, False], [],
                        [MLA_HEADS * 256, MLA_HEADS * 256, MLA_HEADS * 128], tile_w, n_ctx_tiles_w)
    (q_full, k_full, v_mla) = assemble((q_lin, kv_lin, kr, cos_m, sin_m), ())
    mla_cfg = _AttnCfg(tag + "mla", s, t, MLA_HEADS, MLA_HEADS, 256, MLA_V, _pick(s, 768, 128), False,
                       MLA_SCALE, False)
    y_mla = _attention(mla_cfg)(q_full, k_full, v_mla, jnp.zeros((MLA_HEADS,), F32))

    hparts = [(hh * 128, 128) for hh in range(RET_HEADS)]
    mix_op = _rowwise(_fn_mix, tag + "mix", 5, [hparts, hparts, hparts, [(0, 768)], [(0, 768)]],
                      [True] * 5, [], [dm_mix()], tile, n_ctx_tiles)
    (mix_in,) = mix_op((ret[0], ret[1], rg, y_swa, y_mla), ())
    mix = _linear(tag + "w_o")(mix_in, wts['w_o'])
    ln1 = _rowwise(_fn_ln1, tag + "ln1", 2, [[(0, dm)], [(0, dm)]], [True, True], [True] * 5, [dm, dm],
                   tile, n_ctx_tiles)
    x_a, h2 = ln1((stream, mix), (mrow(2), one(small['ln1_g'][l]), one(small['ln1_b'][l]), mrow(3), mrow(4)))
    ug = _linear(tag + "w_up")(h2, wts['ffn_w_up'])
    u_lin, g_lin = ug[:, :dff], ug[:, dff:]
    conv = _rowwise(_fn_conv_gate, tag + "conv_gate", 4, [[(0, dff)]] * 4,
                    [True] * 4, [True] * 4, [dff], tile, n_ctx_tiles, col_tile=_pick(dff, 512, 128))
    cw = wts['ffn_conv_w']
    (y,) = conv((u_lin, g_lin, _shift_rows(g_lin, t, False), _shift_rows(g_lin, t, True)),
                (one(cw[0]), one(cw[1]), one(cw[2]), one(small['ffn_conv_b'][l])))
    f = _linear(tag + "w_down")(y, wts['ffn_w_down'])
    ln2 = _rowwise(_fn_ln2, tag + "ln2", 2, [[(0, dm)], [(0, dm)]], [True, True], [True] * 3, [dm],
                   tile, n_ctx_tiles)
    (out,) = ln2((x_a, f), (mrow(5), one(small['ln2_g'][l]), one(small['ln2_b'][l])))
    return out


def dm_mix():
    return RET_DIM + SWA_HEADS * HEAD_DIM + MLA_HEADS * MLA_V


def kernel(x, c, ctx, c_ctx, ada_w, ada_b, w_in, ret_decay_fwd, ret_decay_bwd, swa_sink, mla_q_norm, mla_w_uq, mla_kv_norm, mla_w_ukv, w_o, ln1_g, ln1_b, ffn_w_up, ffn_conv_w, ffn_conv_b, ffn_w_down, ln2_g, ln2_b, loss_target, m_c_ctx, m_ada_w, m_ada_b, m_w_in, m_ret_decay_fwd, m_ret_decay_bwd, m_swa_sink, m_mla_q_norm, m_mla_w_uq, m_mla_kv_norm, m_mla_w_ukv, m_w_o, m_ln1_g, m_ln1_b, m_ffn_w_up, m_ffn_conv_w, m_ffn_conv_b, m_ffn_w_down, m_ln2_g, m_ln2_b, v_c_ctx, v_ada_w, v_ada_b, v_w_in, v_ret_decay_fwd, v_ret_decay_bwd, v_swa_sink, v_mla_q_norm, v_mla_w_uq, v_mla_kv_norm, v_mla_w_ukv, v_w_o, v_ln1_g, v_ln1_b, v_ffn_w_up, v_ffn_conv_w, v_ffn_conv_b, v_ffn_w_down, v_ln2_g, v_ln2_b):
    weights = dict(c_ctx=c_ctx, ada_w=ada_w, ada_b=ada_b, w_in=w_in, ret_decay_fwd=ret_decay_fwd,
                   ret_decay_bwd=ret_decay_bwd, swa_sink=swa_sink, mla_q_norm=mla_q_norm, mla_w_uq=mla_w_uq,
                   mla_kv_norm=mla_kv_norm, mla_w_ukv=mla_w_ukv, w_o=w_o, ln1_g=ln1_g, ln1_b=ln1_b,
                   ffn_w_up=ffn_w_up, ffn_conv_w=ffn_conv_w, ffn_conv_b=ffn_conv_b, ffn_w_down=ffn_w_down,
                   ln2_g=ln2_g, ln2_b=ln2_b)
    m_in = dict(c_ctx=m_c_ctx, ada_w=m_ada_w, ada_b=m_ada_b, w_in=m_w_in, ret_decay_fwd=m_ret_decay_fwd,
                ret_decay_bwd=m_ret_decay_bwd, swa_sink=m_swa_sink, mla_q_norm=m_mla_q_norm, mla_w_uq=m_mla_w_uq,
                mla_kv_norm=m_mla_kv_norm, mla_w_ukv=m_mla_w_ukv, w_o=m_w_o, ln1_g=m_ln1_g, ln1_b=m_ln1_b,
                ffn_w_up=m_ffn_w_up, ffn_conv_w=m_ffn_conv_w, ffn_conv_b=m_ffn_conv_b, ffn_w_down=m_ffn_w_down,
                ln2_g=m_ln2_g, ln2_b=m_ln2_b)
    v_in = dict(c_ctx=v_c_ctx, ada_w=v_ada_w, ada_b=v_ada_b, w_in=v_w_in, ret_decay_fwd=v_ret_decay_fwd,
                ret_decay_bwd=v_ret_decay_bwd, swa_sink=v_swa_sink, mla_q_norm=v_mla_q_norm, mla_w_uq=v_mla_w_uq,
                mla_kv_norm=v_mla_kv_norm, mla_w_ukv=v_mla_w_ukv, w_o=v_w_o, ln1_g=v_ln1_g, ln1_b=v_ln1_b,
                ffn_w_up=v_ffn_w_up, ffn_conv_w=v_ffn_conv_w, ffn_conv_b=v_ffn_conv_b, ffn_w_down=v_ffn_w_down,
                ln2_g=v_ln2_g, ln2_b=v_ln2_b)

    l_tok, dm = x.shape[1], x.shape[2]
    t = ctx.shape[1]
    s = t + l_tok
    dff = ffn_w_down.shape[1] * _N_DEV
    dims = (s, t, dm, dff)
    me = 4 * lax.axis_index("x") + 2 * lax.axis_index("y") + lax.axis_index("c")
    cos_h, sin_h = _rope_tables(t, l_tok, HEAD_DIM, HEAD_DIM)
    cos_m, sin_m = _rope_tables(t, l_tok, MLA_ROPE, _LANES)
    tables = (cos_h, sin_h, cos_m, sin_m)
    c_all = _exchange(c, True, "gather_cond").reshape(_N_DEV, dm)
    tile = min(256, t)

    def loss_fn(wd, xin):
        mod_rows = jnp.concatenate([jax.nn.silu(c_all), jax.nn.silu(wd['c_ctx'])[None, :],
                                  jnp.zeros((_MOD_ROWS - _N_DEV - 1, dm), F32)], axis=0)
        mods_shard = jnp.stack([_linear("ada_l%d" % l)(mod_rows, wd['ada_w'][l]) for l in range(DEPTH)])
        n_sh = mods_shard.shape[-1]
        mods_all = _all_gather_op("mods", F32)(mods_shard.reshape(DEPTH * _MOD_ROWS, n_sh))
        mods_all = mods_all.reshape(_N_DEV, DEPTH, _MOD_ROWS, n_sh).transpose(1, 2, 0, 3).reshape(DEPTH, _MOD_ROWS, _N_DEV * n_sh)
        mods_all = mods_all + wd['ada_b'][:, None, :]
        mod_x = lax.dynamic_slice_in_dim(mods_all, me, 1, axis=1)[:, 0]
        mod_c = mods_all[:, _N_DEV]
        stream = jnp.concatenate([ctx[0], xin[0]], axis=0)
        for l in range(DEPTH):
            packed, meta = _pack([wd[n][l] for n in GATHERED])
            gathered = _all_gather_op("weights_l%d" % l, _MXU_DTYPE)(packed)
            full = _full_weights({'data': gathered, 'meta': meta}, l, None)
            mod = jnp.stack([mod_c[l].reshape(N_MOD, dm), mod_x[l].reshape(N_MOD, dm)])
            stream = _layer(l, stream, mod, full, wd, tables, dims, l == DEPTH - 1)
        return _loss_op(t // tile, tile)(stream, loss_target[0])

    loss_local, (gw, gx) = jax.value_and_grad(loss_fn, argnums=(0, 1))(weights, x)
    loss = lax.psum(loss_local, ("x", "y", "c"))

    grads, deltas, new_m, new_v = {}, {}, {}, {}

    def as2d(a):
        return a.reshape(-1, a.shape[-1])

    for n in ['ada_w'] + GATHERED:
        g2, d2, m2, v2 = _adamw(as2d(weights[n]), as2d(gw[n]), as2d(m_in[n]), as2d(v_in[n]), False, "adamw_" + n)
        shp = weights[n].shape
        grads[n], deltas[n], new_m[n], new_v[n] = g2.reshape(shp), d2.reshape(shp), m2.reshape(shp), v2.reshape(shp)

    w_pack, meta = _pack([weights[n] for n in REPLICATED])
    g_pack, _ = _pack([gw[n] for n in REPLICATED])
    m_pack, _ = _pack([m_in[n] for n in REPLICATED])
    v_pack, _ = _pack([v_in[n] for n in REPLICATED])
    g_parts = _exchange(g_pack, True, "gather_small_grads")
    outs = _adamw(w_pack, g_parts, m_pack, v_pack, True, "adamw_replicated")
    for dst, packed in zip((grads, deltas, new_m, new_v), outs):
        for n, a in zip(REPLICATED, _unpack(packed, meta)):
            dst[n] = a

    return (loss, gx, *[grads[n] for n in WEIGHTS], *[deltas[n] for n in WEIGHTS],
            *[new_m[n] for n in WEIGHTS], *[new_v[n] for n in WEIGHTS])
```

```python
import functools
import math

import numpy as np
import jax
import jax.numpy as jnp
from jax import lax
from jax.experimental import pallas as pl
from jax.experimental.pallas import tpu as pltpu

F32 = jnp.float32
_MXU_DTYPE = jnp.bfloat16
_VMEM_LIMIT_BYTES = 56 * 1024 * 1024
_LANES = 128
_N_DEV = 8
_MOD_ROWS = 128

DEPTH = 4
HEAD_DIM = 128
ROPE_THETA = 10000.0
GRID_W = 64
RET_HEADS = 4
RET_DIM = RET_HEADS * HEAD_DIM
RET_CHUNK = 128
SWA_HEADS = 6
SWA_KV_HEADS = 2
SWA_WINDOW = 128
MLA_HEADS = 6
MLA_Q_RANK = 512
MLA_KV_RANK = 256
MLA_NOPE = 128
MLA_ROPE = 64
MLA_V = 128
MLA_SCALE = (MLA_NOPE + MLA_ROPE) ** -0.5
N_MOD = 6
LN_EPS = 1e-5
RMS_EPS = 1e-6
NEG_INF = -1e30
ALPHA = (2 * DEPTH) ** 0.25
IN_WIDTH = 4160
IN_WIDTH_PAD = 4224

ADAM_LR = 0.001
ADAM_B1 = 0.9
ADAM_B2 = 0.999
ADAM_EPS = 1e-08
ADAM_WD = 0.01
ADAM_STEP = 10

WEIGHTS = ['c_ctx', 'ada_w', 'ada_b', 'w_in', 'ret_decay_fwd', 'ret_decay_bwd', 'swa_sink', 'mla_q_norm',
           'mla_w_uq', 'mla_kv_norm', 'mla_w_ukv', 'w_o', 'ln1_g', 'ln1_b', 'ffn_w_up', 'ffn_conv_w',
           'ffn_conv_b', 'ffn_w_down', 'ln2_g', 'ln2_b']
GATHERED = ['w_in', 'mla_w_uq', 'mla_w_ukv', 'w_o', 'ffn_w_up', 'ffn_conv_w', 'ffn_w_down']
GATHER_AXIS = {'w_in': 1, 'mla_w_uq': 1, 'mla_w_ukv': 1, 'w_o': 0, 'ffn_w_up': 1, 'ffn_conv_w': 1, 'ffn_w_down': 0}
REPLICATED = ['c_ctx', 'ada_b', 'ret_decay_fwd', 'ret_decay_bwd', 'swa_sink', 'mla_q_norm', 'mla_kv_norm',
              'ln1_g', 'ln1_b', 'ffn_conv_b', 'ln2_g', 'ln2_b']


def _cparams(semantics):
    return pltpu.CompilerParams(dimension_semantics=semantics, vmem_limit_bytes=_VMEM_LIMIT_BYTES)


def _pick(n, target, align):
    best = None
    d = align
    while d <= min(n, target):
        if n % d == 0:
            best = d
        d += align
    return n if best is None else best


def _matmul(a, b, mode, name):
    if mode == 'nn':
        (m, k), (k2, n) = a.shape, b.shape
    elif mode == 'nt':
        (m, k), (n, k2) = a.shape, b.shape
    else:
        (k, m), (k2, n) = a.shape, b.shape
    assert k == k2, (a.shape, b.shape, mode)
    tm = _pick(m, 1024, 128)
    tn = _pick(n, 1408, 128)
    tk = _pick(k, 1408 if mode == 'tn' else 2048, 128)
    nk = k // tk

    def body(a_ref, b_ref, o_ref, *scratch):
        kk = pl.program_id(2)
        if mode == 'nn':
            part = jnp.dot(a_ref[...].astype(_MXU_DTYPE), b_ref[...].astype(_MXU_DTYPE),
                           preferred_element_type=F32)
        elif mode == 'nt':
            part = lax.dot_general(a_ref[...].astype(_MXU_DTYPE), b_ref[...].astype(_MXU_DTYPE),
                                   (((1,), (1,)), ((), ())), preferred_element_type=F32)
        else:
            at = a_ref[...].astype(F32).T.astype(_MXU_DTYPE)
            part = jnp.dot(at, b_ref[...].astype(_MXU_DTYPE), preferred_element_type=F32)
        if nk == 1:
            o_ref[...] = part
            return
        acc_ref, = scratch

        @pl.when(kk == 0)
        def _():
            acc_ref[...] = part

        @pl.when((kk > 0) & (kk < nk - 1))
        def _():
            acc_ref[...] += part

        @pl.when(kk == nk - 1)
        def _():
            o_ref[...] = acc_ref[...] + part

    if mode == 'nn':
        a_spec = pl.BlockSpec((tm, tk), lambda i, j, kk: (i, kk))
        b_spec = pl.BlockSpec((tk, tn), lambda i, j, kk: (kk, j))
    elif mode == 'nt':
        a_spec = pl.BlockSpec((tm, tk), lambda i, j, kk: (i, kk))
        b_spec = pl.BlockSpec((tn, tk), lambda i, j, kk: (j, kk))
    else:
        a_spec = pl.BlockSpec((tk, tm), lambda i, j, kk: (kk, i))
        b_spec = pl.BlockSpec((tk, tn), lambda i, j, kk: (kk, j))
    return pl.pallas_call(
        body, name=name,
        grid=(m // tm, n // tn, nk),
        in_specs=[a_spec, b_spec],
        out_specs=pl.BlockSpec((tm, tn), lambda i, j, kk: (i, j)),
        out_shape=jax.ShapeDtypeStruct((m, n), F32),
        scratch_shapes=[pltpu.VMEM((tm, tn), F32)] if nk > 1 else [],
        compiler_params=_cparams(("parallel", "parallel", "arbitrary")),
    )(a, b)


def _linear(name):
    @jax.custom_vjp
    def op(a, w):
        return _matmul(a.astype(_MXU_DTYPE), w.astype(_MXU_DTYPE), 'nn', name + "_fwd")

    def fwd(a, w):
        ab, wb = a.astype(_MXU_DTYPE), w.astype(_MXU_DTYPE)
        return _matmul(ab, wb, 'nn', name + "_fwd"), (ab, wb)

    def bwd(res, g):
        ab, wb = res
        gb = g.astype(_MXU_DTYPE)
        da = _matmul(gb, wb, 'nt', name + "_da")
        dw = _matmul(ab, gb, 'tn', name + "_dw")
        return da, dw

    op.defvjp(fwd, bwd)
    return op


def _pieces(parts):
    out = []
    for p, (start, width) in enumerate(parts):
        pw = math.gcd(start, width) if start else width
        assert pw % _LANES == 0, (start, width)
        for t in range(width // pw):
            out.append((p, pw, start // pw + t))
    return out


def _rowwise(fn, name, rows, parts, diff, pdiff, out_widths, tile, n_ctx_tiles, col_tile=None):
    pieces = [_pieces(p) for p in parts]

    def sel_of(i, n_sel):
        return jnp.where(i >= n_ctx_tiles, n_sel - 1, 0)

    def in_specs_for(row_arrays, params):
        specs, operands = [], []
        for r in range(rows):
            for (_, pw, blk) in pieces[r]:
                if col_tile is None:
                    specs.append(pl.BlockSpec((tile, pw), lambda j, i, blk=blk: (i, blk)))
                else:
                    nb = pw // col_tile
                    specs.append(pl.BlockSpec((tile, col_tile), lambda j, i, blk=blk, nb=nb: (i, blk * nb + j)))
                operands.append(row_arrays[r])
        for p in params:
            n_sel, _, w = p.shape
            cw = w if col_tile is None else col_tile
            if col_tile is None:
                specs.append(pl.BlockSpec((None, 1, cw), lambda j, i, n_sel=n_sel: (sel_of(i, n_sel), 0, 0)))
            else:
                specs.append(pl.BlockSpec((None, 1, cw), lambda j, i, n_sel=n_sel: (sel_of(i, n_sel), 0, j)))
            operands.append(p)
        return specs, operands

    def load_inputs(refs):
        k = 0
        vals = []
        for r in range(rows):
            got = [[] for _ in parts[r]]
            for (p, _, _) in pieces[r]:
                got[p].append(refs[k][...].astype(F32))
                k += 1
            vals.append([g[0] if len(g) == 1 else jnp.concatenate(g, axis=1) for g in got])
        return vals, k

    def forward(row_arrays, params):
        s = row_arrays[0].shape[0]
        ncol = 1 if col_tile is None else out_widths[0] // col_tile
        n_par = len(params)

        def body(*refs):
            vals, k = load_inputs(refs)
            pvals = [refs[k + q][...].astype(F32) for q in range(n_par)]
            outs = fn(vals, pvals)
            for o_ref, o in zip(refs[k + n_par:], outs):
                o_ref[...] = o.astype(o_ref.dtype)

        specs, operands = in_specs_for(row_arrays, params)
        if col_tile is None:
            out_specs = [pl.BlockSpec((tile, w), lambda j, i: (i, 0)) for w in out_widths]
        else:
            out_specs = [pl.BlockSpec((tile, col_tile), lambda j, i: (i, j)) for _ in out_widths]
        return pl.pallas_call(
            body, name=name + "_fwd",
            grid=(ncol, s // tile),
            in_specs=specs, out_specs=out_specs,
            out_shape=[jax.ShapeDtypeStruct((s, w), F32) for w in out_widths],
            compiler_params=_cparams(("arbitrary", "arbitrary")),
        )(*operands)

    def backward(row_arrays, params, cts):
        s = row_arrays[0].shape[0]
        ncol = 1 if col_tile is None else out_widths[0] // col_tile
        n_par = len(params)
        n_out = len(out_widths)
        d_rows = [r for r in range(rows) if diff[r]]
        d_pars = [q for q in range(n_par) if pdiff[q]]

        def body(*refs):
            i = pl.program_id(1)
            vals, k = load_inputs(refs)
            pvals = [refs[k + q][...].astype(F32) for q in range(n_par)]
            k += n_par
            ct_vals = [refs[k + o][...].astype(F32) for o in range(n_out)]
            k += n_out
            drow_refs = refs[k:k + len(d_rows)]
            dpar_refs = refs[k + len(d_rows):]

            def f(dv, dp):
                full_v = list(vals)
                for r, v in zip(d_rows, dv):
                    full_v[r] = v
                full_p = list(pvals)
                for q, v in zip(d_pars, dp):
                    full_p[q] = v
                return fn(full_v, full_p)

            _, vjp = jax.vjp(f, [vals[r] for r in d_rows], [pvals[q] for q in d_pars])
            g_rows, g_pars = vjp(ct_vals)
            for ref, r, g in zip(drow_refs, d_rows, g_rows):
                covered = sum(w for (_, w) in parts[r])
                if col_tile is None:
                    if covered != ref.shape[1]:
                        ref[...] = jnp.zeros_like(ref)
                    for (start, width), gp in zip(parts[r], g):
                        ref[:, start:start + width] = gp
                else:
                    ref[...] = g[0]
            for ref, q, g in zip(dpar_refs, d_pars, g_pars):
                n_sel = params[q].shape[0]
                first = (i == 0) if n_sel == 1 else ((i == 0) | (i == n_ctx_tiles))

                @pl.when(first)
                def _(ref=ref):
                    ref[...] = jnp.zeros_like(ref)

                ref[...] += g

        specs, operands = in_specs_for(row_arrays, params)
        for o, w in enumerate(out_widths):
            if col_tile is None:
                specs.append(pl.BlockSpec((tile, w), lambda j, i: (i, 0)))
            else:
                specs.append(pl.BlockSpec((tile, col_tile), lambda j, i: (i, j)))
            operands.append(cts[o])
        out_specs, out_shape = [], []
        for r in d_rows:
            w = row_arrays[r].shape[1]
            if col_tile is None:
                out_specs.append(pl.BlockSpec((tile, w), lambda j, i: (i, 0)))
            else:
                assert len(parts[r]) == 1 and parts[r][0] == (0, w)
                out_specs.append(pl.BlockSpec((tile, col_tile), lambda j, i: (i, j)))
            out_shape.append(jax.ShapeDtypeStruct((s, w), F32))
        for q in d_pars:
            n_sel, _, w = params[q].shape
            cw = w if col_tile is None else col_tile
            if col_tile is None:
                out_specs.append(pl.BlockSpec((None, 1, cw), lambda j, i, n_sel=n_sel: (sel_of(i, n_sel), 0, 0)))
            else:
                out_specs.append(pl.BlockSpec((None, 1, cw), lambda j, i, n_sel=n_sel: (sel_of(i, n_sel), 0, j)))
            out_shape.append(jax.ShapeDtypeStruct((n_sel, 1, w), F32))
        res = pl.pallas_call(
            body, name=name + "_bwd",
            grid=(ncol, s // tile),
            in_specs=specs, out_specs=out_specs, out_shape=out_shape,
            compiler_params=_cparams(("arbitrary", "arbitrary")),
        )(*operands)
        g_rows = [None] * rows
        for r, g in zip(d_rows, res[:len(d_rows)]):
            g_rows[r] = g
        g_pars = [None] * n_par
        for q, g in zip(d_pars, res[len(d_rows):]):
            g_pars[q] = g
        return g_rows, g_pars

    @jax.custom_vjp
    def op(row_arrays, params):
        return tuple(forward(list(row_arrays), list(params)))

    def op_fwd(row_arrays, params):
        return tuple(forward(list(row_arrays), list(params))), (row_arrays, params)

    def op_bwd(res, cts):
        row_arrays, params = res
        g_rows, g_pars = backward(list(row_arrays), list(params), list(cts))
        g_rows = tuple(jnp.zeros_like(a) if g is None else g for a, g in zip(row_arrays, g_rows))
        g_pars = tuple(jnp.zeros_like(a) if g is None else g for a, g in zip(params, g_pars))
        return g_rows, g_pars

    op.defvjp(op_fwd, op_bwd)
    return op


def _rot_impl(x, quarter):
    lane = lax.broadcasted_iota(jnp.int32, (x.shape[0], _LANES), 1)
    even = ((lane // quarter) % 2) == 0
    outs = []
    for k in range(x.shape[1] // _LANES):
        xs = x[:, k * _LANES:(k + 1) * _LANES]
        left = pltpu.roll(xs, _LANES - quarter, 1)
        right = pltpu.roll(xs, quarter, 1)
        outs.append(jnp.where(even, -left, right))
    return outs[0] if len(outs) == 1 else jnp.concatenate(outs, axis=1)


def _make_rot(quarter):
    @jax.custom_vjp
    def rot(x):
        return _rot_impl(x, quarter)

    rot.defvjp(lambda x: (_rot_impl(x, quarter), None), lambda _, g: (-_rot_impl(g, quarter),))
    return rot


_rot32 = _make_rot(32)
_rot16 = _make_rot(16)


def _tile_lanes(t, n):
    return t if n == 1 else jnp.concatenate([t] * n, axis=1)


def _rope(x, cos, sin, rot):
    n = x.shape[1] // _LANES
    return x * _tile_lanes(cos, n) + rot(x) * _tile_lanes(sin, n)


def _rms(x):
    return x * lax.rsqrt(jnp.mean(x * x, axis=-1, keepdims=True) + RMS_EPS)


def _ln(x):
    mu = jnp.mean(x, axis=-1, keepdims=True)
    xc = x - mu
    var = jnp.mean(xc * xc, axis=-1, keepdims=True)
    return xc * lax.rsqrt(var + LN_EPS)


def _sum_all(x):
    return jnp.sum(jnp.sum(x, axis=1, keepdims=True), axis=0, keepdims=True)


def _silu(x):
    return x * (1.0 / (1.0 + jnp.exp(-x)))


def _fn_modulate(vals, pars):
    (s,), = vals
    shift, scale = pars
    return [s * (1.0 + scale) + shift]


def _fn_postproj(vals, pars):
    (rq, rk, rv, rg, sq, sk, sv, mcq, mckv, mkr), (cos_h,), (sin_h,), (cos_m,), (sin_m,) = vals
    q_norm, kv_norm = pars
    k_scale = HEAD_DIM ** -0.5
    return [_rope(rq, cos_h, sin_h, _rot32), _rope(rk, cos_h, sin_h, _rot32) * k_scale, rv, rg,
            _rope(sq, cos_h, sin_h, _rot32), _rope(sk, cos_h, sin_h, _rot32), sv,
            _rms(mcq) * q_norm, _rms(mckv) * kv_norm, _rope(mkr, cos_m, sin_m, _rot16)]


def _fn_mla_assemble(vals, pars):
    (q_lin,), kn_v, (kr,), (cos_m,), (sin_m,) = vals
    kn, vv = kn_v[:MLA_HEADS], kn_v[MLA_HEADS]
    ones, zeros = jnp.ones_like(cos_m), jnp.zeros_like(sin_m)
    cos_q = jnp.concatenate([ones, cos_m] * MLA_HEADS, axis=1)
    sin_q = jnp.concatenate([zeros, sin_m] * MLA_HEADS, axis=1)
    q_full = q_lin * cos_q + _rot16(q_lin) * sin_q
    k_full = jnp.concatenate([t for h in range(MLA_HEADS) for t in (kn[h], kr)], axis=1)
    return [q_full, k_full, vv]


def _fn_mix(vals, pars):
    ret_f, ret_b, rg, (y_swa,), (y_mla,) = vals
    heads = [_silu(rg[h]) * _rms(ret_f[h] + ret_b[h]) for h in range(RET_HEADS)]
    return [jnp.concatenate(heads + [y_swa, y_mla], axis=1)]


def _fn_ln1(vals, pars):
    (s,), (mix,) = vals
    gate, g, b, shift_f, scale_f = pars
    x_a = _ln(ALPHA * s + (1.0 + gate) * mix) * g + b
    return [x_a, x_a * (1.0 + scale_f) + shift_f]


def _fn_ln2(vals, pars):
    (x_a,), (f,) = vals
    gate, g, b = pars
    return [_ln(ALPHA * x_a + (1.0 + gate) * f) * g + b]


def _fn_conv_gate(vals, pars):
    (u,), (g,), (gp,), (gn,) = vals
    w0, w1, w2, b = pars
    return [_silu(w0 * gp + w1 * g + w2 * gn + b) * u]


class _AttnCfg:
    def __init__(self, name, s, t, heads, kv_heads, dk, dv, blk, band, scale, has_sink):
        self.name, self.s, self.t = name, s, t
        self.heads, self.kv_heads, self.group = heads, kv_heads, heads // kv_heads
        self.dk, self.dv, self.blk, self.band, self.scale, self.has_sink = dk, dv, blk, band, scale, has_sink
        self.nq = s // blk
        self.n_ctx = t // blk if band else 0
        self.ks = self.n_ctx + 3 if band else s // blk
        assert s % blk == 0 and (not band or (t % blk == 0 and blk >= SWA_WINDOW))

    def kblock(self, i, st):
        if not self.band:
            return st
        kb = jnp.clip(i + st - self.n_ctx - 1, self.n_ctx, self.nq - 1)
        return jnp.where(st < self.n_ctx, st, kb)

    def valid(self, i, st):
        if not self.band:
            return st >= 0
        kb = i + st - self.n_ctx - 1
        return (st < self.n_ctx) | ((i >= self.n_ctx) & (kb >= self.n_ctx) & (kb <= self.nq - 1))

    def masked(self, i, st):
        if self.band:
            return st >= self.n_ctx
        return i * self.blk < self.t

    def visible(self, i, kb, keys_first=False):
        b = self.blk
        qpos = i * b + lax.broadcasted_iota(jnp.int32, (b, b), 1 if keys_first else 0)
        kpos = kb * b + lax.broadcasted_iota(jnp.int32, (b, b), 0 if keys_first else 1)
        if self.band:
            return jnp.abs(qpos - kpos) <= SWA_WINDOW
        return (kpos < self.t) | (qpos >= self.t)


def _attn_fwd(cfg, q, k, vt, sink):
    b, dk, dv, g = cfg.blk, cfg.dk, cfg.dv, cfg.group

    def body(q_ref, k_ref, vt_ref, sink_ref, o_ref, lse_ref, m_sc, l_sc, acc_sc):
        i, st = pl.program_id(1), pl.program_id(2)

        @pl.when(st == 0)
        def _():
            if cfg.has_sink:
                m_sc[...] = jnp.broadcast_to(sink_ref[0:1, 0:1], m_sc.shape)
                l_sc[...] = jnp.ones_like(l_sc)
            else:
                m_sc[...] = jnp.full_like(m_sc, NEG_INF)
                l_sc[...] = jnp.zeros_like(l_sc)
            acc_sc[...] = jnp.zeros_like(acc_sc)

        def step(use_mask):
            sc = lax.dot_general(k_ref[...], q_ref[...], _NT, preferred_element_type=F32)
            if use_mask:
                sc = jnp.where(cfg.visible(i, cfg.kblock(i, st), keys_first=True), sc, NEG_INF)
            m_prev = m_sc[0:1, :]
            m_new = jnp.maximum(m_prev, jnp.max(sc, axis=0, keepdims=True))
            alpha = jnp.exp2(m_prev - m_new)
            p = jnp.exp2(sc - m_new)
            l_new = alpha * l_sc[0:1, :] + jnp.sum(p, axis=0, keepdims=True)
            acc_sc[...] = acc_sc[...] * alpha + jnp.dot(vt_ref[...], p.astype(_MXU_DTYPE),
                                                       preferred_element_type=F32)
            m_sc[...] = jnp.broadcast_to(m_new, m_sc.shape)
            l_sc[...] = jnp.broadcast_to(l_new, l_sc.shape)

        ok = cfg.valid(i, st)
        msk = cfg.masked(i, st)
        pl.when(ok & msk)(lambda: step(True))
        pl.when(ok & jnp.logical_not(msk))(lambda: step(False))

        @pl.when(st == cfg.ks - 1)
        def _():
            o_ref[...] = acc_sc[...] / l_sc[0:1, :]
            lse_ref[...] = m_sc[...] + jnp.log2(l_sc[...])

    return pl.pallas_call(
        body, name=cfg.name + "_fwd",
        grid=(cfg.heads, cfg.nq, cfg.ks),
        in_specs=[pl.BlockSpec((b, dk), lambda h, i, st: (i, h)),
                  pl.BlockSpec((b, dk), lambda h, i, st: (cfg.kblock(i, st), h // g)),
                  pl.BlockSpec((dv, b), lambda h, i, st: (h // g, cfg.kblock(i, st))),
                  pl.BlockSpec((None, 8, _LANES), lambda h, i, st: (h, 0, 0))],
        out_specs=[pl.BlockSpec((dv, b), lambda h, i, st: (h, i)),
                   pl.BlockSpec((None, 8, b), lambda h, i, st: (h, 0, i))],
        out_shape=[jax.ShapeDtypeStruct((cfg.heads * dv, cfg.s), F32),
                   jax.ShapeDtypeStruct((cfg.heads, 8, cfg.s), F32)],
        scratch_shapes=[pltpu.VMEM((8, b), F32), pltpu.VMEM((8, b), F32), pltpu.VMEM((dv, b), F32)],
        compiler_params=_cparams(("parallel", "parallel", "arbitrary")),
    )(q, k, vt, sink)


def _attn_bwd(cfg, q, k, v, sink, o, lse, do):
    b, dk, dv, g = cfg.blk, cfg.dk, cfg.dv, cfg.group

    def body(q_ref, k_ref, v_ref, sink_ref, o_ref, lse_ref, do_ref, dq_ref, dk_ref, dv_ref, dsink_ref,
             dq_sc, delta_sc):
        i, st = pl.program_id(1), pl.program_id(2)

        @pl.when((i == 0) & (st == 0))
        def _():
            dk_ref[...] = jnp.zeros_like(dk_ref)
            dv_ref[...] = jnp.zeros_like(dv_ref)

        @pl.when(st == 0)
        def _():
            delta = jnp.sum(do_ref[...] * o_ref[...], axis=1, keepdims=True)
            delta_sc[...] = jnp.broadcast_to(delta, delta_sc.shape)
            dq_sc[...] = jnp.zeros_like(dq_sc)
            if cfg.has_sink:
                ps = jnp.exp2(sink_ref[0:1, :] - lse_ref[...]) * delta_sc[...]
                dsink_ref[...] = jnp.broadcast_to(-jnp.sum(ps, axis=0, keepdims=True), dsink_ref.shape)
            else:
                dsink_ref[...] = jnp.zeros_like(dsink_ref)

        def step(use_mask):
            kb = cfg.kblock(i, st)
            qb = q_ref[...].astype(_MXU_DTYPE)
            kbv = k_ref[...].astype(_MXU_DTYPE)
            dob = do_ref[...].astype(_MXU_DTYPE)
            sc = lax.dot_general(qb, kbv, (((1,), (1,)), ((), ())), preferred_element_type=F32)
            if use_mask:
                sc = jnp.where(cfg.visible(i, kb), sc, NEG_INF)
            p = jnp.exp2(sc - lse_ref[:, 0:1])
            dp = lax.dot_general(dob, v_ref[...].astype(_MXU_DTYPE), (((1,), (1,)), ((), ())),
                                 preferred_element_type=F32)
            ds = p * (dp - delta_sc[:, 0:1])
            dq_sc[...] += jnp.dot(ds.astype(_MXU_DTYPE), kbv, preferred_element_type=F32)
            rows = pl.ds(pl.multiple_of(kb * b, b), b)
            dk_ref[rows, :] += jnp.dot(ds.T.astype(_MXU_DTYPE), qb, preferred_element_type=F32)
            dv_ref[rows, :] += jnp.dot(p.T.astype(_MXU_DTYPE), dob, preferred_element_type=F32)

        ok = cfg.valid(i, st)
        msk = cfg.masked(i, st)
        pl.when(ok & msk)(lambda: step(True))
        pl.when(ok & jnp.logical_not(msk))(lambda: step(False))

        @pl.when(st == cfg.ks - 1)
        def _():
            dq_ref[...] = dq_sc[...] * cfg.scale

    return pl.pallas_call(
        body, name=cfg.name + "_bwd",
        grid=(cfg.heads, cfg.nq, cfg.ks),
        in_specs=[pl.BlockSpec((b, dk), lambda h, i, st: (i, h)),
                  pl.BlockSpec((b, dk), lambda h, i, st: (cfg.kblock(i, st), h // g)),
                  pl.BlockSpec((b, dv), lambda h, i, st: (cfg.kblock(i, st), h // g)),
                  pl.BlockSpec((None, 8, _LANES), lambda h, i, st: (h, 0, 0)),
                  pl.BlockSpec((b, dv), lambda h, i, st: (i, h)),
                  pl.BlockSpec((None, b, _LANES), lambda h, i, st: (h, i, 0)),
                  pl.BlockSpec((b, dv), lambda h, i, st: (i, h))],
        out_specs=[pl.BlockSpec((b, dk), lambda h, i, st: (i, h)),
                   pl.BlockSpec((cfg.s, dk), lambda h, i, st: (0, h)),
                   pl.BlockSpec((cfg.s, dv), lambda h, i, st: (0, h)),
                   pl.BlockSpec((None, None, 8, _LANES), lambda h, i, st: (h, i, 0, 0))],
        out_shape=[jax.ShapeDtypeStruct((cfg.s, cfg.heads * dk), F32),
                   jax.ShapeDtypeStruct((cfg.s, cfg.heads * dk), F32),
                   jax.ShapeDtypeStruct((cfg.s, cfg.heads * dv), F32),
                   jax.ShapeDtypeStruct((cfg.heads, cfg.nq, 8, _LANES), F32)],
        scratch_shapes=[pltpu.VMEM((b, dk), F32), pltpu.VMEM((b, _LANES), F32)],
        compiler_params=_cparams(("parallel", "arbitrary", "arbitrary")),
    )(q, k, v, sink, o, lse, do)


def _attention(cfg):
    log2e = math.log2(math.e)

    def run(q, k, v, sink):
        qb = (q * (cfg.scale * log2e)).astype(_MXU_DTYPE)
        kb, vb = k.astype(_MXU_DTYPE), v.astype(_MXU_DTYPE)
        sr = jnp.broadcast_to((sink.astype(F32) * log2e)[:, None, None], (cfg.heads, 8, _LANES))
        ot, lse = _attn_fwd(cfg, qb, kb, vb.T, sr)
        return ot.T, (qb, kb, vb, sr, lse)

    @jax.custom_vjp
    def op(q, k, v, sink):
        return run(q, k, v, sink)[0]

    def fwd(q, k, v, sink):
        o, (qb, kb, vb, sr, lse) = run(q, k, v, sink)
        return o, (qb, kb, vb, sr, o, lse)

    def bwd(res, do):
        qb, kb, vb, sr, o, lse = res
        lse_rows = jnp.broadcast_to(lse[:, 0, :, None], (cfg.heads, cfg.s, _LANES))
        dq, dk_h, dv_h, dsink = _attn_bwd(cfg, qb, kb, vb, sr, o, lse_rows, do)
        dk = dk_h.reshape(cfg.s, cfg.kv_heads, cfg.group, cfg.dk).sum(2).reshape(cfg.s, cfg.kv_heads * cfg.dk)
        dv = dv_h.reshape(cfg.s, cfg.kv_heads, cfg.group, cfg.dv).sum(2).reshape(cfg.s, cfg.kv_heads * cfg.dv)
        return dq, dk * (1.0 / log2e), dv, dsink[:, :, 0, 0].sum(1)

    op.defvjp(fwd, bwd)
    return op


def _ret_chunk_order(n, d, n_ctx, n_all):
    fwd = n
    bwd = jnp.where(n < n_ctx, n_ctx - 1 - n, n_all - 1 - (n - n_ctx))
    return jnp.where(d == 0, fwd, bwd)


def _ret_decays(lg, d):
    c = RET_CHUNK
    i = lax.broadcasted_iota(jnp.int32, (c, c), 0)
    j = lax.broadcasted_iota(jnp.int32, (c, c), 1)
    sign = (1 - 2 * d)
    diff = ((i - j) * sign).astype(F32)
    intra = jnp.where(diff >= 0, jnp.exp(lg * jnp.maximum(diff, 0.0)), 0.0)
    pos = lax.broadcasted_iota(jnp.int32, (c, 1), 0)
    r = (pos + d * (c - 1 - 2 * pos)).astype(F32)
    qd = jnp.exp(lg * (r + 1.0))
    kd = jnp.exp(lg * (c - 1.0 - r))
    cd = jnp.exp(lg * c)
    return intra, qd, kd, cd, diff, r


def _mxu_dot(a, b, dims=None):
    a, b = a.astype(_MXU_DTYPE), b.astype(_MXU_DTYPE)
    if dims is None:
        return jnp.dot(a, b, preferred_element_type=F32)
    return lax.dot_general(a, b, dims, preferred_element_type=F32)


_NT = (((1,), (1,)), ((), ()))


def _ret_fwd(q, k, v, lg, n_ctx, name):
    s = q.shape[0]
    c, hd = RET_CHUNK, HEAD_DIM
    n_all = s // c

    def body(lg_ref, q_ref, k_ref, v_ref, o_ref, st_ref, state_sc):
        d, n = pl.program_id(0), pl.program_id(1)

        @pl.when(n == 0)
        def _():
            state_sc[...] = jnp.zeros_like(state_sc)

        for h in range(RET_HEADS):
            cols = slice(h * hd, (h + 1) * hd)
            intra, qd, kd, cd, _, _ = _ret_decays(lg_ref[d, h], d)
            qv, kv, vv = q_ref[:, cols], k_ref[:, cols], v_ref[:, cols]
            s_in = state_sc[h]
            st_ref[h] = s_in
            p = _mxu_dot(qv, kv, _NT) * intra
            o_ref[:, cols] = _mxu_dot(p, vv) + _mxu_dot(qv * qd, s_in)
            state_sc[h] = cd * s_in + _mxu_dot((kv * kd).T, vv)

    def chunk_spec():
        return pl.BlockSpec((c, RET_DIM), lambda d, n: (_ret_chunk_order(n, d, n_ctx, n_all), 0))

    return pl.pallas_call(
        body, name=name + "_fwd",
        grid=(2, n_all),
        in_specs=[pl.BlockSpec(memory_space=pltpu.SMEM), chunk_spec(), chunk_spec(), chunk_spec()],
        out_specs=[pl.BlockSpec((None, c, RET_DIM), lambda d, n: (d, _ret_chunk_order(n, d, n_ctx, n_all), 0)),
                   pl.BlockSpec((None, RET_HEADS, None, hd, hd), lambda d, n: (d, 0, n, 0, 0))],
        out_shape=[jax.ShapeDtypeStruct((2, s, RET_DIM), F32),
                   jax.ShapeDtypeStruct((2, RET_HEADS, n_all, hd, hd), F32)],
        scratch_shapes=[pltpu.VMEM((RET_HEADS, hd, hd), F32)],
        compiler_params=_cparams(("arbitrary", "arbitrary")),
    )(lg, q, k, v)


def _ret_bwd(q, k, v, lg, states, dout, n_ctx, name):
    s = q.shape[0]
    c, hd = RET_CHUNK, HEAD_DIM
    n_all = s // c

    def body(lg_ref, q_ref, k_ref, v_ref, st_ref, do_ref, dq_ref, dk_ref, dv_ref, dlg_ref, ds_sc):
        d, n = pl.program_id(0), pl.program_id(1)

        @pl.when(n == 0)
        def _():
            ds_sc[...] = jnp.zeros_like(ds_sc)
            dlg_ref[...] = jnp.zeros_like(dlg_ref)

        for h in range(RET_HEADS):
            cols = slice(h * hd, (h + 1) * hd)
            intra, qd, kd, cd, diff, r = _ret_decays(lg_ref[d, h], d)
            qv, kv, vv, do = q_ref[:, cols], k_ref[:, cols], v_ref[:, cols], do_ref[:, cols]
            s_in = st_ref[h]
            ds_out = ds_sc[h]
            sc = _mxu_dot(qv, kv, _NT)
            p = sc * intra
            dp = _mxu_dot(do, vv, _NT)
            dsc = dp * intra
            dqs = _mxu_dot(do, s_in, _NT)
            dkk = _mxu_dot(vv, ds_out, _NT)
            dq_ref[:, cols] = _mxu_dot(dsc, kv) + dqs * qd
            dk_ref[:, cols] = _mxu_dot(dsc.T, qv) + dkk * kd
            dv_ref[:, cols] = _mxu_dot(p.T, do) + _mxu_dot(kv * kd, ds_out)
            ds_sc[h] = cd * ds_out + _mxu_dot((qv * qd).T, do)
            dlg = (_sum_all(dp * p * diff)
                   + _sum_all(jnp.sum(dqs * qv, axis=1, keepdims=True) * qd * (r + 1.0))
                   + _sum_all(jnp.sum(dkk * kv, axis=1, keepdims=True) * kd * (c - 1.0 - r))
                   + _sum_all(ds_out * s_in) * (cd * c))
            dlg_ref[h] += jnp.broadcast_to(dlg, (8, _LANES))

    def order(n, d):
        return _ret_chunk_order(n_all - 1 - n, d, n_ctx, n_all)

    def chunk_spec():
        return pl.BlockSpec((c, RET_DIM), lambda d, n: (order(n, d), 0))

    def dir_spec():
        return pl.BlockSpec((None, c, RET_DIM), lambda d, n: (d, order(n, d), 0))

    return pl.pallas_call(
        body, name=name + "_bwd",
        grid=(2, n_all),
        in_specs=[pl.BlockSpec(memory_space=pltpu.SMEM), chunk_spec(), chunk_spec(), chunk_spec(),
                  pl.BlockSpec((None, RET_HEADS, None, hd, hd), lambda d, n: (d, 0, n_all - 1 - n, 0, 0)),
                  dir_spec()],
        out_specs=[dir_spec(), dir_spec(), dir_spec(),
                   pl.BlockSpec((None, RET_HEADS, 8, _LANES), lambda d, n: (d, 0, 0, 0))],
        out_shape=[jax.ShapeDtypeStruct((2, s, RET_DIM), F32)] * 3
        + [jax.ShapeDtypeStruct((2, RET_HEADS, 8, _LANES), F32)],
        scratch_shapes=[pltpu.VMEM((RET_HEADS, hd, hd), F32)],
        compiler_params=_cparams(("arbitrary", "arbitrary")),
    )(lg, q, k, v, states, dout)


def _retention(n_ctx, name):
    @jax.custom_vjp
    def op(q, k, v, lg):
        return _ret_fwd(q, k, v, lg, n_ctx, name)[0]

    def fwd(q, k, v, lg):
        out, states = _ret_fwd(q, k, v, lg, n_ctx, name)
        return out, (q, k, v, lg, states)

    def bwd(res, dout):
        q, k, v, lg, states = res
        dq, dk, dv, dlg = _ret_bwd(q, k, v, lg, states, dout, n_ctx, name)
        return dq[0] + dq[1], dk[0] + dk[1], dv[0] + dv[1], dlg[:, :, 0, 0]

    op.defvjp(fwd, bwd)
    return op


def _loss_call(y, target, n_ctx_tiles, tile, name):
    s, dm = y.shape

    def body(y_ref, t_ref, loss_ref, dy_ref):
        i = pl.program_id(0)

        @pl.when(i == 0)
        def _():
            loss_ref[...] = jnp.zeros_like(loss_ref)

        @pl.when(i < n_ctx_tiles)
        def _():
            dy_ref[...] = jnp.zeros_like(dy_ref)

        @pl.when(i >= n_ctx_tiles)
        def _():
            err = y_ref[...] - t_ref[...]
            dy_ref[...] = err * (1.0 / dm)
            loss_ref[...] += jnp.broadcast_to(_sum_all(err * err) * (0.5 / dm), loss_ref.shape)

    return pl.pallas_call(
        body, name=name,
        grid=(s // tile,),
        in_specs=[pl.BlockSpec((tile, dm), lambda i: (i, 0)),
                  pl.BlockSpec((tile, dm), lambda i: (jnp.maximum(i - n_ctx_tiles, 0), 0))],
        out_specs=[pl.BlockSpec((8, _LANES), lambda i: (0, 0)),
                   pl.BlockSpec((tile, dm), lambda i: (i, 0))],
        out_shape=[jax.ShapeDtypeStruct((8, _LANES), F32), jax.ShapeDtypeStruct((s, dm), F32)],
        compiler_params=_cparams(("arbitrary",)),
    )(y, target)


def _loss_op(n_ctx_tiles, tile):
    @jax.custom_vjp
    def op(y, target):
        return _loss_call(y, target, n_ctx_tiles, tile, "loss_head")[0][0, 0]

    def fwd(y, target):
        loss, dy = _loss_call(y, target, n_ctx_tiles, tile, "loss_head")
        return loss[0, 0], (dy, target)

    def bwd(res, g):
        dy, target = res
        return dy * g, jnp.zeros_like(target)

    op.defvjp(fwd, bwd)
    return op


def _exchange(x, gather, name):
    blk_shape = x.shape if gather else x.shape[1:]

    def body(x_ref, o_ref, send_sems, recv_sems, local_sem):
        mx, my, mc = lax.axis_index("x"), lax.axis_index("y"), lax.axis_index("c")
        me = 4 * mx + 2 * my + mc
        copies = []
        for rel in range(1, _N_DEV):
            px = mx ^ ((rel >> 2) & 1)
            py = my ^ ((rel >> 1) & 1)
            pc = mc ^ (rel & 1)
            src = x_ref if gather else x_ref.at[4 * px + 2 * py + pc]
            cp = pltpu.make_async_remote_copy(
                src_ref=src, dst_ref=o_ref.at[me],
                send_sem=send_sems.at[rel - 1], recv_sem=recv_sems.at[rel - 1],
                device_id=(px, py, pc), device_id_type=pl.DeviceIdType.MESH)
            cp.start()
            copies.append(cp)
        mine = pltpu.make_async_copy(x_ref if gather else x_ref.at[me], o_ref.at[me], local_sem)
        mine.start()
        for cp in copies:
            cp.wait()
        mine.wait()

    return pl.pallas_call(
        body, name=name,
        in_specs=[pl.BlockSpec(memory_space=pltpu.HBM)],
        out_specs=pl.BlockSpec(memory_space=pltpu.HBM),
        out_shape=jax.ShapeDtypeStruct((_N_DEV,) + tuple(blk_shape), x.dtype),
        scratch_shapes=[pltpu.SemaphoreType.DMA((_N_DEV - 1,)), pltpu.SemaphoreType.DMA((_N_DEV - 1,)),
                        pltpu.SemaphoreType.DMA(())],
    )(x)


def _sum_parts(parts, name, out_dtype=F32):
    n, r, w = parts.shape
    tile = _pick(r, max(8, (1 << 19) // (w * n) // 16 * 16), 16)

    def body(p_ref, o_ref):
        acc = p_ref[0].astype(F32)
        for j in range(1, n):
            acc = acc + p_ref[j].astype(F32)
        o_ref[...] = acc.astype(o_ref.dtype)

    return pl.pallas_call(
        body, name=name, grid=(r // tile,),
        in_specs=[pl.BlockSpec((n, tile, w), lambda i: (0, i, 0))],
        out_specs=pl.BlockSpec((tile, w), lambda i: (i, 0)),
        out_shape=jax.ShapeDtypeStruct((r, w), out_dtype),
        compiler_params=_cparams(("arbitrary",)),
    )(parts)


def _chip_peers(mx, my):
    return [(1 - mx, my), (mx, 1 - my), (1 - mx, 1 - my)]


def _gather_two_level(xs, name):
    n_ops = len(xs)

    def body(*refs):
        x_refs, o_refs = refs[:n_ops], refs[n_ops:2 * n_ops]
        send_sems, recv_sems, local_sems = refs[2 * n_ops:]
        mx, my, mc = lax.axis_index("x"), lax.axis_index("y"), lax.axis_index("c")
        sibling = (mx, my, 1 - mc)
        chips = _chip_peers(mx, my)

        def copy(op, k, block, to, src=None):
            idx = 4 * block[0] + 2 * block[1] + block[2]
            dst = o_refs[op].at[idx]
            return pltpu.make_async_remote_copy(
                src_ref=dst if src is None else src, dst_ref=dst,
                send_sem=send_sems.at[op * 7 + k], recv_sem=recv_sems.at[op * 7 + k],
                device_id=to, device_id_type=pl.DeviceIdType.MESH)

        me = (mx, my, mc)
        mine, first, passed = [], [], []
        for op in range(n_ops):
            cp = pltpu.make_async_copy(x_refs[op], o_refs[op].at[4 * mx + 2 * my + mc], local_sems.at[op])
            cp.start()
            mine.append(cp)
            first.append(copy(op, 0, me, sibling, src=x_refs[op]))
            for j, chip in enumerate(chips):
                first.append(copy(op, 1 + j, me, (*chip, mc), src=x_refs[op]))
        for cp in first:
            cp.start()
        for j, chip in enumerate(chips):
            for op in range(n_ops):
                copy(op, 1 + j, (*chip, mc), me).wait_recv()
                cp = copy(op, 4 + j, (*chip, mc), sibling)
                cp.start()
                passed.append(cp)
        for op in range(n_ops):
            copy(op, 0, sibling, me).wait_recv()
            for j, chip in enumerate(chips):
                copy(op, 4 + j, (*chip, 1 - mc), me).wait_recv()
        for cp in first + passed:
            cp.wait_send()
        for cp in mine:
            cp.wait()

    hbm = pl.BlockSpec(memory_space=pltpu.HBM)
    return pl.pallas_call(
        body, name=name,
        in_specs=[hbm] * n_ops, out_specs=[hbm] * n_ops,
        out_shape=[jax.ShapeDtypeStruct((_N_DEV,) + tuple(x.shape), x.dtype) for x in xs],
        scratch_shapes=[pltpu.SemaphoreType.DMA((7 * n_ops,)), pltpu.SemaphoreType.DMA((7 * n_ops,)),
                        pltpu.SemaphoreType.DMA((n_ops,))],
    )(*xs)


def _scatter_pair(xs, name):
    n_ops = len(xs)

    def body(*refs):
        x_refs, o_refs = refs[:n_ops], refs[n_ops:2 * n_ops]
        send_sems, recv_sems, local_sems = refs[2 * n_ops:]
        mx, my, mc = lax.axis_index("x"), lax.axis_index("y"), lax.axis_index("c")
        copies = []
        for op in range(n_ops):
            loc = pltpu.make_async_copy(x_refs[op].at[mc], o_refs[op].at[0], local_sems.at[op])
            loc.start()
            rem = pltpu.make_async_remote_copy(
                src_ref=x_refs[op].at[1 - mc], dst_ref=o_refs[op].at[1],
                send_sem=send_sems.at[op], recv_sem=recv_sems.at[op],
                device_id=(mx, my, 1 - mc), device_id_type=pl.DeviceIdType.MESH)
            rem.start()
            copies += [loc, rem]
        for cp in copies:
            cp.wait()

    hbm = pl.BlockSpec(memory_space=pltpu.HBM)
    return pl.pallas_call(
        body, name=name,
        in_specs=[hbm] * n_ops, out_specs=[hbm] * n_ops,
        out_shape=[jax.ShapeDtypeStruct(x.shape, x.dtype) for x in xs],
        scratch_shapes=[pltpu.SemaphoreType.DMA((n_ops,)), pltpu.SemaphoreType.DMA((n_ops,)),
                        pltpu.SemaphoreType.DMA((n_ops,))],
    )(*xs)


def _scatter_chips(xs, name):
    n_ops = len(xs)

    def body(*refs):
        x_refs, o_refs = refs[:n_ops], refs[n_ops:2 * n_ops]
        send_sems, recv_sems, local_sems = refs[2 * n_ops:]
        mx, my, mc = lax.axis_index("x"), lax.axis_index("y"), lax.axis_index("c")
        my_chip = 2 * mx + my
        copies = []
        for op in range(n_ops):
            loc = pltpu.make_async_copy(x_refs[op].at[my_chip], o_refs[op].at[my_chip], local_sems.at[op])
            loc.start()
            copies.append(loc)
            for j, (px, py) in enumerate(_chip_peers(mx, my)):
                rem = pltpu.make_async_remote_copy(
                    src_ref=x_refs[op].at[2 * px + py], dst_ref=o_refs[op].at[my_chip],
                    send_sem=send_sems.at[op * 3 + j], recv_sem=recv_sems.at[op * 3 + j],
                    device_id=(px, py, mc), device_id_type=pl.DeviceIdType.MESH)
                rem.start()
                copies.append(rem)
        for cp in copies:
            cp.wait()

    hbm = pl.BlockSpec(memory_space=pltpu.HBM)
    return pl.pallas_call(
        body, name=name,
        in_specs=[hbm] * n_ops, out_specs=[hbm] * n_ops,
        out_shape=[jax.ShapeDtypeStruct(x.shape, x.dtype) for x in xs],
        scratch_shapes=[pltpu.SemaphoreType.DMA((3 * n_ops,)), pltpu.SemaphoreType.DMA((3 * n_ops,)),
                        pltpu.SemaphoreType.DMA((n_ops,))],
    )(*xs)


def _weights_gather_op(name):
    def impl(shards):
        got = _gather_two_level([s.astype(_MXU_DTYPE) for s in shards], name + "_gather")
        return tuple(g.astype(F32) for g in got)

    @jax.custom_vjp
    def op(shards):
        return impl(shards)

    def fwd(shards):
        return impl(shards), None

    def bwd(_, cts):
        by_core = [jnp.swapaxes(g.astype(_MXU_DTYPE).reshape((4, 2) + g.shape[1:]), 0, 1) for g in cts]
        paired = _scatter_pair(by_core, name + "_scatter_pair")
        chip_sums = []
        for k, p in enumerate(paired):
            flat = p.reshape(2, -1, p.shape[-1])
            chip_sums.append(_sum_parts(flat, "%s_pair_sum%d" % (name, k), _MXU_DTYPE).reshape(p.shape[1:]))
        crossed = _scatter_chips(chip_sums, name + "_scatter_chips")
        out = []
        for k, q in enumerate(crossed):
            flat = q.reshape(4, -1, q.shape[-1])
            out.append(_sum_parts(flat, "%s_chip_sum%d" % (name, k), F32).reshape(q.shape[1:]))
        return (tuple(out),)

    op.defvjp(fwd, bwd)
    return op


def _all_gather_op(name, payload_dtype):
    def impl(x):
        return _exchange(x.astype(payload_dtype), True, name + "_gather").astype(F32)

    @jax.custom_vjp
    def op(x):
        return impl(x)

    def fwd(x):
        return impl(x), None

    def bwd(_, g):
        return (_sum_parts(_exchange(g, False, name + "_scatter"), name + "_sum"),)

    op.defvjp(fwd, bwd)
    return op


def _adamw(w, g, m, v, partial, name):
    r, wd = w.shape
    tile = _pick(r, max(8, (1 << 20) // (4 * wd) // 8 * 8), 8)
    c1 = 1.0 / (1.0 - ADAM_B1 ** ADAM_STEP)
    c2 = 1.0 / (1.0 - ADAM_B2 ** ADAM_STEP)

    def body(w_ref, g_ref, m_ref, v_ref, go_ref, d_ref, mo_ref, vo_ref):
        if partial:
            g = g_ref[0]
            for j in range(1, _N_DEV):
                g = g + g_ref[j]
        else:
            g = g_ref[...]
        m_new = ADAM_B1 * m_ref[...] + (1.0 - ADAM_B1) * g
        v_new = ADAM_B2 * v_ref[...] + (1.0 - ADAM_B2) * (g * g)
        m_hat = m_new * c1
        v_hat = v_new * c2
        go_ref[...] = g
        d_ref[...] = -ADAM_LR * (m_hat / (jnp.sqrt(v_hat) + ADAM_EPS) + ADAM_WD * w_ref[...])
        mo_ref[...] = m_new
        vo_ref[...] = v_new

    spec = pl.BlockSpec((tile, wd), lambda i: (i, 0))
    g_spec = pl.BlockSpec((_N_DEV, tile, wd), lambda i: (0, i, 0)) if partial else spec
    return pl.pallas_call(
        body, name=name, grid=(r // tile,),
        in_specs=[spec, g_spec, spec, spec],
        out_specs=[spec] * 4,
        out_shape=[jax.ShapeDtypeStruct((r, wd), F32)] * 4,
        compiler_params=_cparams(("arbitrary",)),
    )(w, g, m, v)


def _pack(arrays):
    flat, meta, off = [], [], 0
    for a in arrays:
        n = int(np.prod(a.shape))
        pad = (-n) % _LANES
        flat.append(a.reshape(-1))
        if pad:
            flat.append(jnp.zeros((pad,), a.dtype))
        meta.append((off, a.shape))
        off += n + pad
    pad = (-off) % (8 * _LANES)
    if pad:
        flat.append(jnp.zeros((pad,), arrays[0].dtype))
    return jnp.concatenate(flat).reshape(-1, _LANES), meta


def _unpack(packed, meta):
    flat = packed.reshape(-1)
    return [flat[off:off + int(np.prod(shape))].reshape(shape) for off, shape in meta]


def _rope_tables(t, l, dim, width):
    rows = l // GRID_W
    r = np.repeat(np.arange(rows, dtype=np.float32), GRID_W)
    cc = np.tile(np.arange(GRID_W, dtype=np.float32), rows)
    n_freq = dim // 4
    inv = jnp.asarray(ROPE_THETA, F32) ** (-jnp.arange(n_freq, dtype=F32) / n_freq)
    ang_r = jnp.asarray(r)[:, None] * inv
    ang_c = jnp.asarray(cc)[:, None] * inv
    ang = jnp.concatenate([ang_r, ang_r, ang_c, ang_c], axis=-1)
    cos, sin = jnp.cos(ang), jnp.sin(ang)
    if width > dim:
        cos = jnp.concatenate([cos, jnp.ones((l, width - dim), F32)], axis=1)
        sin = jnp.concatenate([sin, jnp.zeros((l, width - dim), F32)], axis=1)
    cos = jnp.concatenate([jnp.ones((t, width), F32), cos], axis=0)
    sin = jnp.concatenate([jnp.zeros((t, width), F32), sin], axis=0)
    return cos, sin


def _shift_rows(g, t, up):
    def seg(a):
        z = jnp.zeros((1, a.shape[1]), a.dtype)
        return jnp.concatenate([a[1:], z], axis=0) if up else jnp.concatenate([z, a[:-1]], axis=0)
    return jnp.concatenate([seg(g[:t]), seg(g[t:])], axis=0)


def _full_weights(gathered):
    full = {}
    for n, g in zip(GATHERED, gathered):
        if GATHER_AXIS[n] == 0:
            full[n] = g.reshape(g.shape[0] * g.shape[1], g.shape[2])
        else:
            full[n] = jnp.transpose(g, (1, 0, 2)).reshape(g.shape[1], g.shape[0] * g.shape[2])
    return full


def _layer(l, stream, mod, wts, small, tables, dims, last):
    s, t, dm, dff = dims
    tile = min(256, t)
    n_ctx_tiles = t // tile
    tile_w = min(128, t)
    n_ctx_tiles_w = t // tile_w
    cos_h, sin_h, cos_m, sin_m = tables
    tag = "l%d_" % l

    def mrow(k):
        return mod[:, k:k + 1, :]

    def one(a):
        return a.reshape(1, 1, -1)

    w_in = jnp.concatenate([wts['w_in'], jnp.zeros((dm, IN_WIDTH_PAD - IN_WIDTH), F32)], axis=1)
    modulate = _rowwise(_fn_modulate, tag + "modulate", 1, [[(0, dm)]], [True], [True, True], [dm],
                        tile, n_ctx_tiles)
    (h,) = modulate((stream,), (mrow(0), mrow(1)))
    proj = _linear(tag + "w_in")(h, w_in)

    in_parts = [(0, 512), (512, 512), (1024, 512), (1536, 512), (2048, 768), (2816, 256), (3072, 256),
                (3328, 512), (3840, 256), (4096, 128)]
    postproj = _rowwise(_fn_postproj, tag + "postproj", 5,
                        [in_parts, [(0, 128)], [(0, 128)], [(0, 128)], [(0, 128)]],
                        [True, False, False, False, False], [True, True],
                        [512, 512, 512, 512, 768, 256, 256, 512, 256, 128], tile_w, n_ctx_tiles_w)
    (rq, rk, rv, rg, sq, sk, sv, cqn, ckvn, kr) = postproj(
        (proj, cos_h, sin_h, cos_m, sin_m), (one(small['mla_q_norm'][l]), one(small['mla_kv_norm'][l])))

    lg = jnp.stack([jax.nn.log_sigmoid(small['ret_decay_fwd'][l]), jax.nn.log_sigmoid(small['ret_decay_bwd'][l])])
    ret = _retention(t // RET_CHUNK, tag + "retention")(rq, rk, rv, lg)

    swa_blk = 256 if (t % 256 == 0 and s % 256 == 0) else 128
    swa_cfg = _AttnCfg(tag + "swa", s, t, SWA_HEADS, SWA_KV_HEADS, HEAD_DIM, HEAD_DIM, swa_blk, True,
                       HEAD_DIM ** -0.5, True)
    y_swa = _attention(swa_cfg)(sq, sk, sv, small['swa_sink'][l])

    w_uq = wts['mla_w_uq'].reshape(MLA_Q_RANK, MLA_HEADS, MLA_NOPE + MLA_ROPE)
    w_uq = jnp.concatenate([w_uq, jnp.zeros((MLA_Q_RANK, MLA_HEADS, 256 - MLA_NOPE - MLA_ROPE), F32)], axis=2)
    w_uq = w_uq.reshape(MLA_Q_RANK, MLA_HEADS * 256)
    w_ukv = wts['mla_w_ukv'].reshape(MLA_KV_RANK, MLA_HEADS, MLA_NOPE + MLA_V)
    w_ukv = jnp.concatenate([w_ukv[:, :, :MLA_NOPE].reshape(MLA_KV_RANK, -1),
                             w_ukv[:, :, MLA_NOPE:].reshape(MLA_KV_RANK, -1)], axis=1)
    q_lin = _linear(tag + "w_uq")(cqn, w_uq)
    kv_lin = _linear(tag + "w_ukv")(ckvn, w_ukv)
    kv_parts = [(hh * 128, 128) for hh in range(MLA_HEADS)] + [(MLA_HEADS * 128, MLA_HEADS * 128)]
    assemble = _rowwise(_fn_mla_assemble, tag + "mla_assemble", 5,
                        [[(0, MLA_HEADS * 256)], kv_parts, [(0, 128)], [(0, 128)], [(0, 128)]],
                        [True, True, True, False, False], [],
                        [MLA_HEADS * 256, MLA_HEADS * 256, MLA_HEADS * 128], tile_w, n_ctx_tiles_w)
    (q_full, k_full, v_mla) = assemble((q_lin, kv_lin, kr, cos_m, sin_m), ())
    mla_cfg = _AttnCfg(tag + "mla", s, t, MLA_HEADS, MLA_HEADS, 256, MLA_V, _pick(s, 768, 128), False,
                       MLA_SCALE, False)
    y_mla = _attention(mla_cfg)(q_full, k_full, v_mla, jnp.zeros((MLA_HEADS,), F32))

    hparts = [(hh * 128, 128) for hh in range(RET_HEADS)]
    mix_op = _rowwise(_fn_mix, tag + "mix", 5, [hparts, hparts, hparts, [(0, 768)], [(0, 768)]],
                      [True] * 5, [], [dm_mix()], tile, n_ctx_tiles)
    (mix_in,) = mix_op((ret[0], ret[1], rg, y_swa, y_mla), ())
    mix = _linear(tag + "w_o")(mix_in, wts['w_o'])
    ln1 = _rowwise(_fn_ln1, tag + "ln1", 2, [[(0, dm)], [(0, dm)]], [True, True], [True] * 5, [dm, dm],
                   tile, n_ctx_tiles)
    x_a, h2 = ln1((stream, mix), (mrow(2), one(small['ln1_g'][l]), one(small['ln1_b'][l]), mrow(3), mrow(4)))
    ug = _linear(tag + "w_up")(h2, wts['ffn_w_up'])
    u_lin, g_lin = ug[:, :dff], ug[:, dff:]
    conv = _rowwise(_fn_conv_gate, tag + "conv_gate", 4, [[(0, dff)]] * 4,
                    [True] * 4, [True] * 4, [dff], tile, n_ctx_tiles, col_tile=_pick(dff, 512, 128))
    cw = wts['ffn_conv_w']
    (y,) = conv((u_lin, g_lin, _shift_rows(g_lin, t, False), _shift_rows(g_lin, t, True)),
                (one(cw[0]), one(cw[1]), one(cw[2]), one(small['ffn_conv_b'][l])))
    f = _linear(tag + "w_down")(y, wts['ffn_w_down'])
    ln2 = _rowwise(_fn_ln2, tag + "ln2", 2, [[(0, dm)], [(0, dm)]], [True, True], [True] * 3, [dm],
                   tile, n_ctx_tiles)
    (out,) = ln2((x_a, f), (mrow(5), one(small['ln2_g'][l]), one(small['ln2_b'][l])))
    return out


def dm_mix():
    return RET_DIM + SWA_HEADS * HEAD_DIM + MLA_HEADS * MLA_V


def kernel(x, c, ctx, c_ctx, ada_w, ada_b, w_in, ret_decay_fwd, ret_decay_bwd, swa_sink, mla_q_norm, mla_w_uq, mla_kv_norm, mla_w_ukv, w_o, ln1_g, ln1_b, ffn_w_up, ffn_conv_w, ffn_conv_b, ffn_w_down, ln2_g, ln2_b, loss_target, m_c_ctx, m_ada_w, m_ada_b, m_w_in, m_ret_decay_fwd, m_ret_decay_bwd, m_swa_sink, m_mla_q_norm, m_mla_w_uq, m_mla_kv_norm, m_mla_w_ukv, m_w_o, m_ln1_g, m_ln1_b, m_ffn_w_up, m_ffn_conv_w, m_ffn_conv_b, m_ffn_w_down, m_ln2_g, m_ln2_b, v_c_ctx, v_ada_w, v_ada_b, v_w_in, v_ret_decay_fwd, v_ret_decay_bwd, v_swa_sink, v_mla_q_norm, v_mla_w_uq, v_mla_kv_norm, v_mla_w_ukv, v_w_o, v_ln1_g, v_ln1_b, v_ffn_w_up, v_ffn_conv_w, v_ffn_conv_b, v_ffn_w_down, v_ln2_g, v_ln2_b):
    weights = dict(c_ctx=c_ctx, ada_w=ada_w, ada_b=ada_b, w_in=w_in, ret_decay_fwd=ret_decay_fwd,
                   ret_decay_bwd=ret_decay_bwd, swa_sink=swa_sink, mla_q_norm=mla_q_norm, mla_w_uq=mla_w_uq,
                   mla_kv_norm=mla_kv_norm, mla_w_ukv=mla_w_ukv, w_o=w_o, ln1_g=ln1_g, ln1_b=ln1_b,
                   ffn_w_up=ffn_w_up, ffn_conv_w=ffn_conv_w, ffn_conv_b=ffn_conv_b, ffn_w_down=ffn_w_down,
                   ln2_g=ln2_g, ln2_b=ln2_b)
    m_in = dict(c_ctx=m_c_ctx, ada_w=m_ada_w, ada_b=m_ada_b, w_in=m_w_in, ret_decay_fwd=m_ret_decay_fwd,
                ret_decay_bwd=m_ret_decay_bwd, swa_sink=m_swa_sink, mla_q_norm=m_mla_q_norm, mla_w_uq=m_mla_w_uq,
                mla_kv_norm=m_mla_kv_norm, mla_w_ukv=m_mla_w_ukv, w_o=m_w_o, ln1_g=m_ln1_g, ln1_b=m_ln1_b,
                ffn_w_up=m_ffn_w_up, ffn_conv_w=m_ffn_conv_w, ffn_conv_b=m_ffn_conv_b, ffn_w_down=m_ffn_w_down,
                ln2_g=m_ln2_g, ln2_b=m_ln2_b)
    v_in = dict(c_ctx=v_c_ctx, ada_w=v_ada_w, ada_b=v_ada_b, w_in=v_w_in, ret_decay_fwd=v_ret_decay_fwd,
                ret_decay_bwd=v_ret_decay_bwd, swa_sink=v_swa_sink, mla_q_norm=v_mla_q_norm, mla_w_uq=v_mla_w_uq,
                mla_kv_norm=v_mla_kv_norm, mla_w_ukv=v_mla_w_ukv, w_o=v_w_o, ln1_g=v_ln1_g, ln1_b=v_ln1_b,
                ffn_w_up=v_ffn_w_up, ffn_conv_w=v_ffn_conv_w, ffn_conv_b=v_ffn_conv_b, ffn_w_down=v_ffn_w_down,
                ln2_g=v_ln2_g, ln2_b=v_ln2_b)

    l_tok, dm = x.shape[1], x.shape[2]
    t = ctx.shape[1]
    s = t + l_tok
    dff = ffn_w_down.shape[1] * _N_DEV
    dims = (s, t, dm, dff)
    me = 4 * lax.axis_index("x") + 2 * lax.axis_index("y") + lax.axis_index("c")
    cos_h, sin_h = _rope_tables(t, l_tok, HEAD_DIM, HEAD_DIM)
    cos_m, sin_m = _rope_tables(t, l_tok, MLA_ROPE, _LANES)
    tables = (cos_h, sin_h, cos_m, sin_m)
    c_all = _exchange(c, True, "gather_cond").reshape(_N_DEV, dm)
    tile = min(256, t)

    def loss_fn(wd, xin):
        mod_rows = jnp.concatenate([jax.nn.silu(c_all), jax.nn.silu(wd['c_ctx'])[None, :],
                                  jnp.zeros((_MOD_ROWS - _N_DEV - 1, dm), F32)], axis=0)
        mods_shard = jnp.stack([_linear("ada_l%d" % l)(mod_rows, wd['ada_w'][l]) for l in range(DEPTH)])
        n_sh = mods_shard.shape[-1]
        mods_all = _all_gather_op("mods", F32)(mods_shard.reshape(DEPTH * _MOD_ROWS, n_sh))
        mods_all = mods_all.reshape(_N_DEV, DEPTH, _MOD_ROWS, n_sh).transpose(1, 2, 0, 3).reshape(DEPTH, _MOD_ROWS, _N_DEV * n_sh)
        mods_all = mods_all + wd['ada_b'][:, None, :]
        mod_x = lax.dynamic_slice_in_dim(mods_all, me, 1, axis=1)[:, 0]
        mod_c = mods_all[:, _N_DEV]
        stream = jnp.concatenate([ctx[0], xin[0]], axis=0)
        for l in range(DEPTH):
            gathered = _weights_gather_op("weights_l%d" % l)(tuple(wd[n][l] for n in GATHERED))
            full = _full_weights(gathered)
            mod = jnp.stack([mod_c[l].reshape(N_MOD, dm), mod_x[l].reshape(N_MOD, dm)])
            stream = _layer(l, stream, mod, full, wd, tables, dims, l == DEPTH - 1)
        return _loss_op(t // tile, tile)(stream, loss_target[0])

    loss_local, (gw, gx) = jax.value_and_grad(loss_fn, argnums=(0, 1))(weights, x)
    loss = lax.psum(loss_local, ("x", "y", "c"))

    grads, deltas, new_m, new_v = {}, {}, {}, {}

    def as2d(a):
        return a.reshape(-1, a.shape[-1])

    for n in ['ada_w'] + GATHERED:
        g2, d2, m2, v2 = _adamw(as2d(weights[n]), as2d(gw[n]), as2d(m_in[n]), as2d(v_in[n]), False, "adamw_" + n)
        shp = weights[n].shape
        grads[n], deltas[n], new_m[n], new_v[n] = g2.reshape(shp), d2.reshape(shp), m2.reshape(shp), v2.reshape(shp)

    w_pack, meta = _pack([weights[n] for n in REPLICATED])
    g_pack, _ = _pack([gw[n] for n in REPLICATED])
    m_pack, _ = _pack([m_in[n] for n in REPLICATED])
    v_pack, _ = _pack([v_in[n] for n in REPLICATED])
    g_parts = _exchange(g_pack, True, "gather_small_grads")
    outs = _adamw(w_pack, g_parts, m_pack, v_pack, True, "adamw_replicated")
    for dst, packed in zip((grads, deltas, new_m, new_v), outs):
        for n, a in zip(REPLICATED, _unpack(packed, meta)):
            dst[n] = a

    return (loss, gx, *[grads[n] for n in WEIGHTS], *[deltas[n] for n in WEIGHTS],
            *[new_m[n] for n in WEIGHTS], *[new_v[n] for n in WEIGHTS])
```

```python
import functools
import math

import numpy as np
import jax
import jax.numpy as jnp
from jax import lax
from jax.experimental import pallas as pl
from jax.experimental.pallas import tpu as pltpu

F32 = jnp.float32
_MXU_DTYPE = jnp.bfloat16
_VMEM_LIMIT_BYTES = 56 * 1024 * 1024
_LANES = 128
_N_DEV = 8
_MOD_ROWS = 128

DEPTH = 4
HEAD_DIM = 128
ROPE_THETA = 10000.0
GRID_W = 64
RET_HEADS = 4
RET_DIM = RET_HEADS * HEAD_DIM
RET_CHUNK = 128
SWA_HEADS = 6
SWA_KV_HEADS = 2
SWA_WINDOW = 128
MLA_HEADS = 6
MLA_Q_RANK = 512
MLA_KV_RANK = 256
MLA_NOPE = 128
MLA_ROPE = 64
MLA_V = 128
MLA_SCALE = (MLA_NOPE + MLA_ROPE) ** -0.5
N_MOD = 6
LN_EPS = 1e-5
RMS_EPS = 1e-6
NEG_INF = -1e30
ALPHA = (2 * DEPTH) ** 0.25
IN_WIDTH = 4160
IN_WIDTH_PAD = 4224

ADAM_LR = 0.001
ADAM_B1 = 0.9
ADAM_B2 = 0.999
ADAM_EPS = 1e-08
ADAM_WD = 0.01
ADAM_STEP = 10

WEIGHTS = ['c_ctx', 'ada_w', 'ada_b', 'w_in', 'ret_decay_fwd', 'ret_decay_bwd', 'swa_sink', 'mla_q_norm',
           'mla_w_uq', 'mla_kv_norm', 'mla_w_ukv', 'w_o', 'ln1_g', 'ln1_b', 'ffn_w_up', 'ffn_conv_w',
           'ffn_conv_b', 'ffn_w_down', 'ln2_g', 'ln2_b']
GATHERED = ['w_in', 'mla_w_uq', 'mla_w_ukv', 'w_o', 'ffn_w_up', 'ffn_conv_w', 'ffn_w_down']
GATHER_AXIS = {'w_in': 1, 'mla_w_uq': 1, 'mla_w_ukv': 1, 'w_o': 0, 'ffn_w_up': 1, 'ffn_conv_w': 1, 'ffn_w_down': 0}
REPLICATED = ['c_ctx', 'ada_b', 'ret_decay_fwd', 'ret_decay_bwd', 'swa_sink', 'mla_q_norm', 'mla_kv_norm',
              'ln1_g', 'ln1_b', 'ffn_conv_b', 'ln2_g', 'ln2_b']


def _cparams(semantics):
    return pltpu.CompilerParams(dimension_semantics=semantics, vmem_limit_bytes=_VMEM_LIMIT_BYTES)


def _pick(n, target, align):
    best = None
    d = align
    while d <= min(n, target):
        if n % d == 0:
            best = d
        d += align
    return n if best is None else best


def _matmul(a, b, mode, name):
    if mode == 'nn':
        (m, k), (k2, n) = a.shape, b.shape
    elif mode == 'nt':
        (m, k), (n, k2) = a.shape, b.shape
    else:
        (k, m), (k2, n) = a.shape, b.shape
    assert k == k2, (a.shape, b.shape, mode)
    tm = _pick(m, 1024, 128)
    tn = _pick(n, 1408, 128)
    tk = _pick(k, 1408 if mode == 'tn' else 2048, 128)
    nk = k // tk

    def body(a_ref, b_ref, o_ref, *scratch):
        kk = pl.program_id(2)
        if mode == 'nn':
            part = jnp.dot(a_ref[...].astype(_MXU_DTYPE), b_ref[...].astype(_MXU_DTYPE),
                           preferred_element_type=F32)
        elif mode == 'nt':
            part = lax.dot_general(a_ref[...].astype(_MXU_DTYPE), b_ref[...].astype(_MXU_DTYPE),
                                   (((1,), (1,)), ((), ())), preferred_element_type=F32)
        else:
            at = a_ref[...].astype(F32).T.astype(_MXU_DTYPE)
            part = jnp.dot(at, b_ref[...].astype(_MXU_DTYPE), preferred_element_type=F32)
        if nk == 1:
            o_ref[...] = part
            return
        acc_ref, = scratch

        @pl.when(kk == 0)
        def _():
            acc_ref[...] = part

        @pl.when((kk > 0) & (kk < nk - 1))
        def _():
            acc_ref[...] += part

        @pl.when(kk == nk - 1)
        def _():
            o_ref[...] = acc_ref[...] + part

    if mode == 'nn':
        a_spec = pl.BlockSpec((tm, tk), lambda i, j, kk: (i, kk))
        b_spec = pl.BlockSpec((tk, tn), lambda i, j, kk: (kk, j))
    elif mode == 'nt':
        a_spec = pl.BlockSpec((tm, tk), lambda i, j, kk: (i, kk))
        b_spec = pl.BlockSpec((tn, tk), lambda i, j, kk: (j, kk))
    else:
        a_spec = pl.BlockSpec((tk, tm), lambda i, j, kk: (kk, i))
        b_spec = pl.BlockSpec((tk, tn), lambda i, j, kk: (kk, j))
    return pl.pallas_call(
        body, name=name,
        grid=(m // tm, n // tn, nk),
        in_specs=[a_spec, b_spec],
        out_specs=pl.BlockSpec((tm, tn), lambda i, j, kk: (i, j)),
        out_shape=jax.ShapeDtypeStruct((m, n), F32),
        scratch_shapes=[pltpu.VMEM((tm, tn), F32)] if nk > 1 else [],
        compiler_params=_cparams(("parallel", "parallel", "arbitrary")),
    )(a, b)


def _linear(name):
    @jax.custom_vjp
    def op(a, w):
        return _matmul(a.astype(_MXU_DTYPE), w.astype(_MXU_DTYPE), 'nn', name + "_fwd")

    def fwd(a, w):
        ab, wb = a.astype(_MXU_DTYPE), w.astype(_MXU_DTYPE)
        return _matmul(ab, wb, 'nn', name + "_fwd"), (ab, wb)

    def bwd(res, g):
        ab, wb = res
        gb = g.astype(_MXU_DTYPE)
        da = _matmul(gb, wb, 'nt', name + "_da")
        dw = _matmul(ab, gb, 'tn', name + "_dw")
        return da, dw

    op.defvjp(fwd, bwd)
    return op


def _pieces(parts):
    out = []
    for p, (start, width) in enumerate(parts):
        pw = math.gcd(start, width) if start else width
        assert pw % _LANES == 0, (start, width)
        for t in range(width // pw):
            out.append((p, pw, start // pw + t))
    return out


def _rowwise(fn, name, rows, parts, diff, pdiff, out_widths, tile, n_ctx_tiles, col_tile=None):
    pieces = [_pieces(p) for p in parts]

    def sel_of(i, n_sel):
        return jnp.where(i >= n_ctx_tiles, n_sel - 1, 0)

    def in_specs_for(row_arrays, params):
        specs, operands = [], []
        for r in range(rows):
            for (_, pw, blk) in pieces[r]:
                if col_tile is None:
                    specs.append(pl.BlockSpec((tile, pw), lambda j, i, blk=blk: (i, blk)))
                else:
                    nb = pw // col_tile
                    specs.append(pl.BlockSpec((tile, col_tile), lambda j, i, blk=blk, nb=nb: (i, blk * nb + j)))
                operands.append(row_arrays[r])
        for p in params:
            n_sel, _, w = p.shape
            cw = w if col_tile is None else col_tile
            if col_tile is None:
                specs.append(pl.BlockSpec((None, 1, cw), lambda j, i, n_sel=n_sel: (sel_of(i, n_sel), 0, 0)))
            else:
                specs.append(pl.BlockSpec((None, 1, cw), lambda j, i, n_sel=n_sel: (sel_of(i, n_sel), 0, j)))
            operands.append(p)
        return specs, operands

    def load_inputs(refs):
        k = 0
        vals = []
        for r in range(rows):
            got = [[] for _ in parts[r]]
            for (p, _, _) in pieces[r]:
                got[p].append(refs[k][...].astype(F32))
                k += 1
            vals.append([g[0] if len(g) == 1 else jnp.concatenate(g, axis=1) for g in got])
        return vals, k

    def forward(row_arrays, params):
        s = row_arrays[0].shape[0]
        ncol = 1 if col_tile is None else out_widths[0] // col_tile
        n_par = len(params)

        def body(*refs):
            vals, k = load_inputs(refs)
            pvals = [refs[k + q][...].astype(F32) for q in range(n_par)]
            outs = fn(vals, pvals)
            for o_ref, o in zip(refs[k + n_par:], outs):
                o_ref[...] = o.astype(o_ref.dtype)

        specs, operands = in_specs_for(row_arrays, params)
        if col_tile is None:
            out_specs = [pl.BlockSpec((tile, w), lambda j, i: (i, 0)) for w in out_widths]
        else:
            out_specs = [pl.BlockSpec((tile, col_tile), lambda j, i: (i, j)) for _ in out_widths]
        return pl.pallas_call(
            body, name=name + "_fwd",
            grid=(ncol, s // tile),
            in_specs=specs, out_specs=out_specs,
            out_shape=[jax.ShapeDtypeStruct((s, w), F32) for w in out_widths],
            compiler_params=_cparams(("arbitrary", "arbitrary")),
        )(*operands)

    def backward(row_arrays, params, cts):
        s = row_arrays[0].shape[0]
        ncol = 1 if col_tile is None else out_widths[0] // col_tile
        n_par = len(params)
        n_out = len(out_widths)
        d_rows = [r for r in range(rows) if diff[r]]
        d_pars = [q for q in range(n_par) if pdiff[q]]

        def body(*refs):
            i = pl.program_id(1)
            vals, k = load_inputs(refs)
            pvals = [refs[k + q][...].astype(F32) for q in range(n_par)]
            k += n_par
            ct_vals = [refs[k + o][...].astype(F32) for o in range(n_out)]
            k += n_out
            drow_refs = refs[k:k + len(d_rows)]
            dpar_refs = refs[k + len(d_rows):]

            def f(dv, dp):
                full_v = list(vals)
                for r, v in zip(d_rows, dv):
                    full_v[r] = v
                full_p = list(pvals)
                for q, v in zip(d_pars, dp):
                    full_p[q] = v
                return fn(full_v, full_p)

            _, vjp = jax.vjp(f, [vals[r] for r in d_rows], [pvals[q] for q in d_pars])
            g_rows, g_pars = vjp(ct_vals)
            for ref, r, g in zip(drow_refs, d_rows, g_rows):
                covered = sum(w for (_, w) in parts[r])
                if col_tile is None:
                    if covered != ref.shape[1]:
                        ref[...] = jnp.zeros_like(ref)
                    for (start, width), gp in zip(parts[r], g):
                        ref[:, start:start + width] = gp
                else:
                    ref[...] = g[0]
            for ref, q, g in zip(dpar_refs, d_pars, g_pars):
                n_sel = params[q].shape[0]
                first = (i == 0) if n_sel == 1 else ((i == 0) | (i == n_ctx_tiles))

                @pl.when(first)
                def _(ref=ref):
                    ref[...] = jnp.zeros_like(ref)

                ref[...] += g

        specs, operands = in_specs_for(row_arrays, params)
        for o, w in enumerate(out_widths):
            if col_tile is None:
                specs.append(pl.BlockSpec((tile, w), lambda j, i: (i, 0)))
            else:
                specs.append(pl.BlockSpec((tile, col_tile), lambda j, i: (i, j)))
            operands.append(cts[o])
        out_specs, out_shape = [], []
        for r in d_rows:
            w = row_arrays[r].shape[1]
            if col_tile is None:
                out_specs.append(pl.BlockSpec((tile, w), lambda j, i: (i, 0)))
            else:
                assert len(parts[r]) == 1 and parts[r][0] == (0, w)
                out_specs.append(pl.BlockSpec((tile, col_tile), lambda j, i: (i, j)))
            out_shape.append(jax.ShapeDtypeStruct((s, w), F32))
        for q in d_pars:
            n_sel, _, w = params[q].shape
            cw = w if col_tile is None else col_tile
            if col_tile is None:
                out_specs.append(pl.BlockSpec((None, 1, cw), lambda j, i, n_sel=n_sel: (sel_of(i, n_sel), 0, 0)))
            else:
                out_specs.append(pl.BlockSpec((None, 1, cw), lambda j, i, n_sel=n_sel: (sel_of(i, n_sel), 0, j)))
            out_shape.append(jax.ShapeDtypeStruct((n_sel, 1, w), F32))
        res = pl.pallas_call(
            body, name=name + "_bwd",
            grid=(ncol, s // tile),
            in_specs=specs, out_specs=out_specs, out_shape=out_shape,
            compiler_params=_cparams(("arbitrary", "arbitrary")),
        )(*operands)
        g_rows = [None] * rows
        for r, g in zip(d_rows, res[:len(d_rows)]):
            g_rows[r] = g
        g_pars = [None] * n_par
        for q, g in zip(d_pars, res[len(d_rows):]):
            g_pars[q] = g
        return g_rows, g_pars

    @jax.custom_vjp
    def op(row_arrays, params):
        return tuple(forward(list(row_arrays), list(params)))

    def op_fwd(row_arrays, params):
        return tuple(forward(list(row_arrays), list(params))), (row_arrays, params)

    def op_bwd(res, cts):
        row_arrays, params = res
        g_rows, g_pars = backward(list(row_arrays), list(params), list(cts))
        g_rows = tuple(jnp.zeros_like(a) if g is None else g for a, g in zip(row_arrays, g_rows))
        g_pars = tuple(jnp.zeros_like(a) if g is None else g for a, g in zip(params, g_pars))
        return g_rows, g_pars

    op.defvjp(op_fwd, op_bwd)
    return op


def _rot_impl(x, quarter):
    lane = lax.broadcasted_iota(jnp.int32, (x.shape[0], _LANES), 1)
    even = ((lane // quarter) % 2) == 0
    outs = []
    for k in range(x.shape[1] // _LANES):
        xs = x[:, k * _LANES:(k + 1) * _LANES]
        left = pltpu.roll(xs, _LANES - quarter, 1)
        right = pltpu.roll(xs, quarter, 1)
        outs.append(jnp.where(even, -left, right))
    return outs[0] if len(outs) == 1 else jnp.concatenate(outs, axis=1)


def _make_rot(quarter):
    @jax.custom_vjp
    def rot(x):
        return _rot_impl(x, quarter)

    rot.defvjp(lambda x: (_rot_impl(x, quarter), None), lambda _, g: (-_rot_impl(g, quarter),))
    return rot


_rot32 = _make_rot(32)
_rot16 = _make_rot(16)


def _tile_lanes(t, n):
    return t if n == 1 else jnp.concatenate([t] * n, axis=1)


def _rope(x, cos, sin, rot):
    n = x.shape[1] // _LANES
    return x * _tile_lanes(cos, n) + rot(x) * _tile_lanes(sin, n)


def _rms(x):
    return x * lax.rsqrt(jnp.mean(x * x, axis=-1, keepdims=True) + RMS_EPS)


def _ln(x):
    mu = jnp.mean(x, axis=-1, keepdims=True)
    xc = x - mu
    var = jnp.mean(xc * xc, axis=-1, keepdims=True)
    return xc * lax.rsqrt(var + LN_EPS)


def _sum_all(x):
    return jnp.sum(jnp.sum(x, axis=1, keepdims=True), axis=0, keepdims=True)


def _silu(x):
    return x * (1.0 / (1.0 + jnp.exp(-x)))


def _fn_modulate(vals, pars):
    (s,), = vals
    shift, scale = pars
    return [s * (1.0 + scale) + shift]


def _fn_postproj(vals, pars):
    (rq, rk, rv, rg, sq, sk, sv, mcq, mckv, mkr), (cos_h,), (sin_h,), (cos_m,), (sin_m,) = vals
    q_norm, kv_norm = pars
    k_scale = HEAD_DIM ** -0.5
    return [_rope(rq, cos_h, sin_h, _rot32), _rope(rk, cos_h, sin_h, _rot32) * k_scale, rv, rg,
            _rope(sq, cos_h, sin_h, _rot32), _rope(sk, cos_h, sin_h, _rot32), sv,
            _rms(mcq) * q_norm, _rms(mckv) * kv_norm, _rope(mkr, cos_m, sin_m, _rot16)]


def _fn_mla_assemble(vals, pars):
    (q_lin,), kn_v, (kr,), (cos_m,), (sin_m,) = vals
    kn, vv = kn_v[:MLA_HEADS], kn_v[MLA_HEADS]
    ones, zeros = jnp.ones_like(cos_m), jnp.zeros_like(sin_m)
    cos_q = jnp.concatenate([ones, cos_m] * MLA_HEADS, axis=1)
    sin_q = jnp.concatenate([zeros, sin_m] * MLA_HEADS, axis=1)
    q_full = q_lin * cos_q + _rot16(q_lin) * sin_q
    k_full = jnp.concatenate([t for h in range(MLA_HEADS) for t in (kn[h], kr)], axis=1)
    return [q_full, k_full, vv]


def _fn_mix(vals, pars):
    ret_f, ret_b, rg, (y_swa,), (y_mla,) = vals
    heads = [_silu(rg[h]) * _rms(ret_f[h] + ret_b[h]) for h in range(RET_HEADS)]
    return [jnp.concatenate(heads + [y_swa, y_mla], axis=1)]


def _fn_ln1(vals, pars):
    (s,), (mix,) = vals
    gate, g, b, shift_f, scale_f = pars
    x_a = _ln(ALPHA * s + (1.0 + gate) * mix) * g + b
    return [x_a, x_a * (1.0 + scale_f) + shift_f]


def _fn_ln2(vals, pars):
    (x_a,), (f,) = vals
    gate, g, b = pars
    return [_ln(ALPHA * x_a + (1.0 + gate) * f) * g + b]


def _conv_tiles(s, t, f):
    r = min(256, t)
    assert t % r == 0 and s % r == 0 and r % 8 == 0
    return r, _pick(f, 1408, _LANES)


def _conv_gate_fwd(u, g, par, t, name):
    s, f = u.shape
    r, cw = _conv_tiles(s, t, f)
    n_ctx, n_tiles, per = t // r, s // r, r // 8

    def body(u_ref, g_ref, gp_ref, gn_ref, p_ref, y_ref):
        i = pl.program_id(1)
        gv = g_ref[...]
        row = lax.broadcasted_iota(jnp.int32, (r, 1), 0)
        seg_start = (i == 0) | (i == n_ctx)
        seg_end = (i == n_ctx - 1) | (i == n_tiles - 1)
        prev_row = jnp.where(seg_start, 0.0, gp_ref[7:8, :])
        next_row = jnp.where(seg_end, 0.0, gn_ref[0:1, :])
        gp = jnp.where(row == 0, prev_row, pltpu.roll(gv, 1, 0))
        gn = jnp.where(row == r - 1, next_row, pltpu.roll(gv, r - 1, 0))
        gc = p_ref[0:1, :] * gp + p_ref[1:2, :] * gv + p_ref[2:3, :] * gn + p_ref[3:4, :]
        y_ref[...] = _silu(gc) * u_ref[...]

    tile = pl.BlockSpec((r, cw), lambda j, i: (i, j))
    return pl.pallas_call(
        body, name=name + "_fwd",
        grid=(f // cw, n_tiles),
        in_specs=[tile, tile,
                  pl.BlockSpec((8, cw), lambda j, i: (jnp.maximum(i * per - 1, 0), j)),
                  pl.BlockSpec((8, cw), lambda j, i: (jnp.minimum((i + 1) * per, s // 8 - 1), j)),
                  pl.BlockSpec((8, cw), lambda j, i: (0, j))],
        out_specs=tile,
        out_shape=jax.ShapeDtypeStruct((s, f), F32),
        compiler_params=_cparams(("arbitrary", "arbitrary")),
    )(u, g, g, g, par)


def _conv_gate_bwd(u, g, par, dy, t, name):
    s, f = u.shape
    r, cw = _conv_tiles(s, t, f)
    n_tiles, per = s // r, r // 8
    re = r + 16

    def body(u_ref, up_ref, un_ref, g_ref, gp_ref, gn_ref, dy_ref, dyp_ref, dyn_ref, p_ref,
             du_ref, dg_ref, dp_ref):
        i = pl.program_id(1)

        def ext(prev, cur, nxt):
            return jnp.concatenate([prev[...], cur[...], nxt[...]], axis=0)

        ge, ue, dye = ext(gp_ref, g_ref, gn_ref), ext(up_ref, u_ref, un_ref), ext(dyp_ref, dy_ref, dyn_ref)
        grow = i * r - 8 + lax.broadcasted_iota(jnp.int32, (re, 1), 0)
        is_start = (grow == 0) | (grow == t)
        is_end = (grow == t - 1) | (grow == s - 1)
        inside = (grow >= 0) & (grow < s)
        w0, w1, w2, bias = p_ref[0:1, :], p_ref[1:2, :], p_ref[2:3, :], p_ref[3:4, :]
        gpe = jnp.where(is_start, 0.0, pltpu.roll(ge, 1, 0))
        gne = jnp.where(is_end, 0.0, pltpu.roll(ge, re - 1, 0))
        gce = w0 * gpe + w1 * ge + w2 * gne + bias
        sig = 1.0 / (1.0 + jnp.exp(-gce))
        dgce = jnp.where(inside, dye * ue * (sig * (1.0 + gce * (1.0 - sig))), 0.0)
        dge = (w1 * dgce + w0 * jnp.where(is_end, 0.0, pltpu.roll(dgce, re - 1, 0))
               + w2 * jnp.where(is_start, 0.0, pltpu.roll(dgce, 1, 0)))
        mid = slice(8, r + 8)
        dg_ref[...] = dge[mid]
        du_ref[...] = (dye * gce * sig)[mid]
        dgc = dgce[mid]

        @pl.when(i == 0)
        def _():
            dp_ref[...] = jnp.zeros_like(dp_ref)

        dp_ref[0:1, :] += jnp.sum(dgc * gpe[mid], axis=0, keepdims=True)
        dp_ref[1:2, :] += jnp.sum(dgc * ge[mid], axis=0, keepdims=True)
        dp_ref[2:3, :] += jnp.sum(dgc * gne[mid], axis=0, keepdims=True)
        dp_ref[3:4, :] += jnp.sum(dgc, axis=0, keepdims=True)

    tile = pl.BlockSpec((r, cw), lambda j, i: (i, j))
    prev = pl.BlockSpec((8, cw), lambda j, i: (jnp.maximum(i * per - 1, 0), j))
    nxt = pl.BlockSpec((8, cw), lambda j, i: (jnp.minimum((i + 1) * per, s // 8 - 1), j))
    par_spec = pl.BlockSpec((8, cw), lambda j, i: (0, j))
    return pl.pallas_call(
        body, name=name + "_bwd",
        grid=(f // cw, n_tiles),
        in_specs=[tile, prev, nxt, tile, prev, nxt, tile, prev, nxt, par_spec],
        out_specs=[tile, tile, par_spec],
        out_shape=[jax.ShapeDtypeStruct((s, f), F32), jax.ShapeDtypeStruct((s, f), F32),
                   jax.ShapeDtypeStruct((8, f), F32)],
        compiler_params=_cparams(("arbitrary", "arbitrary")),
    )(u, u, u, g, g, g, dy, dy, dy, par)


def _conv_gate(t, name):
    @jax.custom_vjp
    def op(u, g, par):
        return _conv_gate_fwd(u, g, par, t, name)

    def fwd(u, g, par):
        return _conv_gate_fwd(u, g, par, t, name), (u, g, par)

    def bwd(res, dy):
        u, g, par = res
        return _conv_gate_bwd(u, g, par, dy, t, name)

    op.defvjp(fwd, bwd)
    return op


class _AttnCfg:
    def __init__(self, name, s, t, heads, kv_heads, dk, dv, blk, band, scale, has_sink):
        self.name, self.s, self.t = name, s, t
        self.heads, self.kv_heads, self.group = heads, kv_heads, heads // kv_heads
        self.dk, self.dv, self.blk, self.band, self.scale, self.has_sink = dk, dv, blk, band, scale, has_sink
        self.nq = s // blk
        self.n_ctx = t // blk if band else 0
        self.ks = self.n_ctx + 3 if band else s // blk
        assert s % blk == 0 and (not band or (t % blk == 0 and blk >= SWA_WINDOW))

    def kblock(self, i, st):
        if not self.band:
            return st
        kb = jnp.clip(i + st - self.n_ctx - 1, self.n_ctx, self.nq - 1)
        return jnp.where(st < self.n_ctx, st, kb)

    def valid(self, i, st):
        if not self.band:
            return st >= 0
        kb = i + st - self.n_ctx - 1
        return (st < self.n_ctx) | ((i >= self.n_ctx) & (kb >= self.n_ctx) & (kb <= self.nq - 1))

    def masked(self, i, st):
        if self.band:
            return st >= self.n_ctx
        return i * self.blk < self.t

    def visible(self, i, kb, keys_first=False):
        b = self.blk
        qpos = i * b + lax.broadcasted_iota(jnp.int32, (b, b), 1 if keys_first else 0)
        kpos = kb * b + lax.broadcasted_iota(jnp.int32, (b, b), 0 if keys_first else 1)
        if self.band:
            return jnp.abs(qpos - kpos) <= SWA_WINDOW
        return (kpos < self.t) | (qpos >= self.t)


def _attn_fwd(cfg, q, k, vt, sink):
    b, dk, dv, g = cfg.blk, cfg.dk, cfg.dv, cfg.group

    def body(q_ref, k_ref, vt_ref, sink_ref, o_ref, lse_ref, m_sc, l_sc, acc_sc):
        i, st = pl.program_id(1), pl.program_id(2)

        @pl.when(st == 0)
        def _():
            if cfg.has_sink:
                for gg in range(g):
                    m_sc[gg] = jnp.broadcast_to(sink_ref[gg, 0:1, 0:1], (8, b))
                l_sc[...] = jnp.ones_like(l_sc)
            else:
                m_sc[...] = jnp.full_like(m_sc, NEG_INF)
                l_sc[...] = jnp.zeros_like(l_sc)
            acc_sc[...] = jnp.zeros_like(acc_sc)

        def step(use_mask):
            vis = cfg.visible(i, cfg.kblock(i, st), keys_first=True) if use_mask else None
            kv, vtv = k_ref[...], vt_ref[...]
            for gg in range(g):
                sc = lax.dot_general(kv, q_ref[:, gg * dk:(gg + 1) * dk], _NT,
                                     preferred_element_type=F32)
                if use_mask:
                    sc = jnp.where(vis, sc, NEG_INF)
                m_prev = m_sc[gg, 0:1, :]
                m_new = jnp.maximum(m_prev, jnp.max(sc, axis=0, keepdims=True))
                alpha = jnp.exp2(m_prev - m_new)
                p = jnp.exp2(sc - m_new)
                l_new = alpha * l_sc[gg, 0:1, :] + jnp.sum(p, axis=0, keepdims=True)
                acc_sc[gg] = acc_sc[gg] * alpha + jnp.dot(vtv, p.astype(_MXU_DTYPE), preferred_element_type=F32)
                m_sc[gg] = jnp.broadcast_to(m_new, (8, b))
                l_sc[gg] = jnp.broadcast_to(l_new, (8, b))

        ok = cfg.valid(i, st)
        msk = cfg.masked(i, st)
        pl.when(ok & msk)(lambda: step(True))
        pl.when(ok & jnp.logical_not(msk))(lambda: step(False))

        @pl.when(st == cfg.ks - 1)
        def _():
            for gg in range(g):
                o_ref[gg * dv:(gg + 1) * dv, :] = acc_sc[gg] / l_sc[gg, 0:1, :]
            lse_ref[...] = m_sc[...] + jnp.log2(l_sc[...])

    return pl.pallas_call(
        body, name=cfg.name + "_fwd",
        grid=(cfg.kv_heads, cfg.nq, cfg.ks),
        in_specs=[pl.BlockSpec((b, g * dk), lambda h, i, st: (i, h)),
                  pl.BlockSpec((b, dk), lambda h, i, st: (cfg.kblock(i, st), h)),
                  pl.BlockSpec((dv, b), lambda h, i, st: (h, cfg.kblock(i, st))),
                  pl.BlockSpec((g, 8, _LANES), lambda h, i, st: (h, 0, 0))],
        out_specs=[pl.BlockSpec((g * dv, b), lambda h, i, st: (h, i)),
                   pl.BlockSpec((g, 8, b), lambda h, i, st: (h, 0, i))],
        out_shape=[jax.ShapeDtypeStruct((cfg.heads * dv, cfg.s), F32),
                   jax.ShapeDtypeStruct((cfg.heads, 8, cfg.s), F32)],
        scratch_shapes=[pltpu.VMEM((g, 8, b), F32), pltpu.VMEM((g, 8, b), F32), pltpu.VMEM((g, dv, b), F32)],
        compiler_params=_cparams(("parallel", "parallel", "arbitrary")),
    )(q, k, vt, sink)


def _attn_bwd(cfg, q, k, v, sink, o, lse, do):
    b, dk, dv, g = cfg.blk, cfg.dk, cfg.dv, cfg.group

    def body(q_ref, k_ref, v_ref, sink_ref, o_ref, lse_ref, do_ref, dq_ref, dk_ref, dv_ref, dsink_ref,
             dq_sc, delta_sc):
        i, st = pl.program_id(1), pl.program_id(2)

        @pl.when((i == 0) & (st == 0))
        def _():
            dk_ref[...] = jnp.zeros_like(dk_ref)
            dv_ref[...] = jnp.zeros_like(dv_ref)

        @pl.when(st == 0)
        def _():
            dq_sc[...] = jnp.zeros_like(dq_sc)
            for gg in range(g):
                vs = slice(gg * dv, (gg + 1) * dv)
                delta = jnp.sum(do_ref[:, vs] * o_ref[:, vs], axis=1, keepdims=True)
                delta_sc[gg] = jnp.broadcast_to(delta, (b, _LANES))
                if cfg.has_sink:
                    ps = jnp.exp2(sink_ref[gg, 0:1, :] - lse_ref[gg]) * delta_sc[gg]
                    dsink_ref[gg] = jnp.broadcast_to(-jnp.sum(ps, axis=0, keepdims=True), (8, _LANES))
                else:
                    dsink_ref[gg] = jnp.zeros((8, _LANES), F32)

        def step(use_mask):
            kb = cfg.kblock(i, st)
            vis = cfg.visible(i, kb) if use_mask else None
            kbv = k_ref[...]
            vbv = v_ref[...]
            rows = pl.ds(pl.multiple_of(kb * b, b), b)
            dk_acc = None
            dv_acc = None
            for gg in range(g):
                ks, vs = slice(gg * dk, (gg + 1) * dk), slice(gg * dv, (gg + 1) * dv)
                qb = q_ref[:, ks]
                dob = do_ref[:, vs].astype(_MXU_DTYPE)
                sc = lax.dot_general(qb, kbv, _NT, preferred_element_type=F32)
                if use_mask:
                    sc = jnp.where(vis, sc, NEG_INF)
                p = jnp.exp2(sc - lse_ref[gg, :, 0:1])
                dp = lax.dot_general(dob, vbv, _NT, preferred_element_type=F32)
                ds = p * (dp - delta_sc[gg, :, 0:1])
                dq_sc[:, ks] += jnp.dot(ds.astype(_MXU_DTYPE), kbv, preferred_element_type=F32)
                dk_h = jnp.dot(ds.T.astype(_MXU_DTYPE), qb, preferred_element_type=F32)
                dv_h = jnp.dot(p.T.astype(_MXU_DTYPE), dob, preferred_element_type=F32)
                dk_acc = dk_h if dk_acc is None else dk_acc + dk_h
                dv_acc = dv_h if dv_acc is None else dv_acc + dv_h
            dk_ref[rows, :] += dk_acc
            dv_ref[rows, :] += dv_acc

        ok = cfg.valid(i, st)
        msk = cfg.masked(i, st)
        pl.when(ok & msk)(lambda: step(True))
        pl.when(ok & jnp.logical_not(msk))(lambda: step(False))

        @pl.when(st == cfg.ks - 1)
        def _():
            dq_ref[...] = dq_sc[...] * cfg.scale

    return pl.pallas_call(
        body, name=cfg.name + "_bwd",
        grid=(cfg.kv_heads, cfg.nq, cfg.ks),
        in_specs=[pl.BlockSpec((b, g * dk), lambda h, i, st: (i, h)),
                  pl.BlockSpec((b, dk), lambda h, i, st: (cfg.kblock(i, st), h)),
                  pl.BlockSpec((b, dv), lambda h, i, st: (cfg.kblock(i, st), h)),
                  pl.BlockSpec((g, 8, _LANES), lambda h, i, st: (h, 0, 0)),
                  pl.BlockSpec((b, g * dv), lambda h, i, st: (i, h)),
                  pl.BlockSpec((g, b, _LANES), lambda h, i, st: (h, i, 0)),
                  pl.BlockSpec((b, g * dv), lambda h, i, st: (i, h))],
        out_specs=[pl.BlockSpec((b, g * dk), lambda h, i, st: (i, h)),
                   pl.BlockSpec((cfg.s, dk), lambda h, i, st: (0, h)),
                   pl.BlockSpec((cfg.s, dv), lambda h, i, st: (0, h)),
                   pl.BlockSpec((g, None, 8, _LANES), lambda h, i, st: (h, i, 0, 0))],
        out_shape=[jax.ShapeDtypeStruct((cfg.s, cfg.heads * dk), F32),
                   jax.ShapeDtypeStruct((cfg.s, cfg.kv_heads * dk), F32),
                   jax.ShapeDtypeStruct((cfg.s, cfg.kv_heads * dv), F32),
                   jax.ShapeDtypeStruct((cfg.heads, cfg.nq, 8, _LANES), F32)],
        scratch_shapes=[pltpu.VMEM((b, g * dk), F32), pltpu.VMEM((g, b, _LANES), F32)],
        compiler_params=_cparams(("parallel", "arbitrary", "arbitrary")),
    )(q, k, v, sink, o, lse, do)


def _attention(cfg):
    log2e = math.log2(math.e)

    def run(q, k, v, sink):
        qb = (q * (cfg.scale * log2e)).astype(_MXU_DTYPE)
        kb, vb = k.astype(_MXU_DTYPE), v.astype(_MXU_DTYPE)
        sr = jnp.broadcast_to((sink.astype(F32) * log2e)[:, None, None], (cfg.heads, 8, _LANES))
        ot, lse = _attn_fwd(cfg, qb, kb, vb.T, sr)
        return ot.T, (qb, kb, vb, sr, lse)

    @jax.custom_vjp
    def op(q, k, v, sink):
        return run(q, k, v, sink)[0]

    def fwd(q, k, v, sink):
        o, (qb, kb, vb, sr, lse) = run(q, k, v, sink)
        return o, (qb, kb, vb, sr, o, lse)

    def bwd(res, do):
        qb, kb, vb, sr, o, lse = res
        lse_rows = jnp.broadcast_to(lse[:, 0, :, None], (cfg.heads, cfg.s, _LANES))
        dq, dk, dv, dsink = _attn_bwd(cfg, qb, kb, vb, sr, o, lse_rows, do)
        return dq, dk * (1.0 / log2e), dv, dsink[:, :, 0, 0].sum(1)

    op.defvjp(fwd, bwd)
    return op


def _ret_chunk_order(n, d, n_ctx, n_all):
    fwd = n
    bwd = jnp.where(n < n_ctx, n_ctx - 1 - n, n_all - 1 - (n - n_ctx))
    return jnp.where(d == 0, fwd, bwd)


def _ret_decays(lg, d):
    c = RET_CHUNK
    i = lax.broadcasted_iota(jnp.int32, (c, c), 0)
    j = lax.broadcasted_iota(jnp.int32, (c, c), 1)
    sign = (1 - 2 * d)
    diff = ((i - j) * sign).astype(F32)
    intra = jnp.where(diff >= 0, jnp.exp(lg * jnp.maximum(diff, 0.0)), 0.0)
    pos = lax.broadcasted_iota(jnp.int32, (c, 1), 0)
    r = (pos + d * (c - 1 - 2 * pos)).astype(F32)
    qd = jnp.exp(lg * (r + 1.0))
    kd = jnp.exp(lg * (c - 1.0 - r))
    cd = jnp.exp(lg * c)
    return intra, qd, kd, cd, diff, r


def _mxu_dot(a, b, dims=None):
    a, b = a.astype(_MXU_DTYPE), b.astype(_MXU_DTYPE)
    if dims is None:
        return jnp.dot(a, b, preferred_element_type=F32)
    return lax.dot_general(a, b, dims, preferred_element_type=F32)


_NT = (((1,), (1,)), ((), ()))


def _ret_fwd(q, k, v, lg, n_ctx, name):
    s = q.shape[0]
    c, hd = RET_CHUNK, HEAD_DIM
    n_all = s // c

    def body(lg_ref, q_ref, k_ref, v_ref, o_ref, st_ref, state_sc):
        d, n = pl.program_id(0), pl.program_id(1)

        @pl.when(n == 0)
        def _():
            state_sc[...] = jnp.zeros_like(state_sc)

        for h in range(RET_HEADS):
            cols = slice(h * hd, (h + 1) * hd)
            intra, qd, kd, cd, _, _ = _ret_decays(lg_ref[d, h], d)
            qv, kv, vv = q_ref[:, cols], k_ref[:, cols], v_ref[:, cols]
            s_in = state_sc[h]
            st_ref[h] = s_in
            p = _mxu_dot(qv, kv, _NT) * intra
            o_ref[:, cols] = _mxu_dot(p, vv) + _mxu_dot(qv * qd, s_in)
            state_sc[h] = cd * s_in + _mxu_dot((kv * kd).T, vv)

    def chunk_spec():
        return pl.BlockSpec((c, RET_DIM), lambda d, n: (_ret_chunk_order(n, d, n_ctx, n_all), 0))

    return pl.pallas_call(
        body, name=name + "_fwd",
        grid=(2, n_all),
        in_specs=[pl.BlockSpec(memory_space=pltpu.SMEM), chunk_spec(), chunk_spec(), chunk_spec()],
        out_specs=[pl.BlockSpec((None, c, RET_DIM), lambda d, n: (d, _ret_chunk_order(n, d, n_ctx, n_all), 0)),
                   pl.BlockSpec((None, RET_HEADS, None, hd, hd), lambda d, n: (d, 0, n, 0, 0))],
        out_shape=[jax.ShapeDtypeStruct((2, s, RET_DIM), F32),
                   jax.ShapeDtypeStruct((2, RET_HEADS, n_all, hd, hd), F32)],
        scratch_shapes=[pltpu.VMEM((RET_HEADS, hd, hd), F32)],
        compiler_params=_cparams(("arbitrary", "arbitrary")),
    )(lg, q, k, v)


def _ret_bwd(q, k, v, lg, states, dout, n_ctx, name):
    s = q.shape[0]
    c, hd = RET_CHUNK, HEAD_DIM
    n_all = s // c

    def body(lg_ref, q_ref, k_ref, v_ref, st_ref, do_ref, dq_ref, dk_ref, dv_ref, dlg_ref, ds_sc):
        d, n = pl.program_id(0), pl.program_id(1)

        @pl.when(n == 0)
        def _():
            ds_sc[...] = jnp.zeros_like(ds_sc)
            dlg_ref[...] = jnp.zeros_like(dlg_ref)

        for h in range(RET_HEADS):
            cols = slice(h * hd, (h + 1) * hd)
            intra, qd, kd, cd, diff, r = _ret_decays(lg_ref[d, h], d)
            qv, kv, vv, do = q_ref[:, cols], k_ref[:, cols], v_ref[:, cols], do_ref[:, cols]
            s_in = st_ref[h]
            ds_out = ds_sc[h]
            sc = _mxu_dot(qv, kv, _NT)
            p = sc * intra
            dp = _mxu_dot(do, vv, _NT)
            dsc = dp * intra
            dqs = _mxu_dot(do, s_in, _NT)
            dkk = _mxu_dot(vv, ds_out, _NT)
            dq_ref[:, cols] = _mxu_dot(dsc, kv) + dqs * qd
            dk_ref[:, cols] = _mxu_dot(dsc.T, qv) + dkk * kd
            dv_ref[:, cols] = _mxu_dot(p.T, do) + _mxu_dot(kv * kd, ds_out)
            ds_sc[h] = cd * ds_out + _mxu_dot((qv * qd).T, do)
            dlg = (_sum_all(dp * p * diff)
                   + _sum_all(jnp.sum(dqs * qv, axis=1, keepdims=True) * qd * (r + 1.0))
                   + _sum_all(jnp.sum(dkk * kv, axis=1, keepdims=True) * kd * (c - 1.0 - r))
                   + _sum_all(ds_out * s_in) * (cd * c))
            dlg_ref[h] += jnp.broadcast_to(dlg, (8, _LANES))

    def order(n, d):
        return _ret_chunk_order(n_all - 1 - n, d, n_ctx, n_all)

    def chunk_spec():
        return pl.BlockSpec((c, RET_DIM), lambda d, n: (order(n, d), 0))

    def dir_spec():
        return pl.BlockSpec((None, c, RET_DIM), lambda d, n: (d, order(n, d), 0))

    return pl.pallas_call(
        body, name=name + "_bwd",
        grid=(2, n_all),
        in_specs=[pl.BlockSpec(memory_space=pltpu.SMEM), chunk_spec(), chunk_spec(), chunk_spec(),
                  pl.BlockSpec((None, RET_HEADS, None, hd, hd), lambda d, n: (d, 0, n_all - 1 - n, 0, 0)),
                  dir_spec()],
        out_specs=[dir_spec(), dir_spec(), dir_spec(),
                   pl.BlockSpec((None, RET_HEADS, 8, _LANES), lambda d, n: (d, 0, 0, 0))],
        out_shape=[jax.ShapeDtypeStruct((2, s, RET_DIM), F32)] * 3
        + [jax.ShapeDtypeStruct((2, RET_HEADS, 8, _LANES), F32)],
        scratch_shapes=[pltpu.VMEM((RET_HEADS, hd, hd), F32)],
        compiler_params=_cparams(("arbitrary", "arbitrary")),
    )(lg, q, k, v, states, dout)


def _retention(n_ctx, name):
    @jax.custom_vjp
    def op(q, k, v, lg):
        return _ret_fwd(q, k, v, lg, n_ctx, name)[0]

    def fwd(q, k, v, lg):
        out, states = _ret_fwd(q, k, v, lg, n_ctx, name)
        return out, (q, k, v, lg, states)

    def bwd(res, dout):
        q, k, v, lg, states = res
        dq, dk, dv, dlg = _ret_bwd(q, k, v, lg, states, dout, n_ctx, name)
        return dq[0] + dq[1], dk[0] + dk[1], dv[0] + dv[1], dlg[:, :, 0, 0]

    op.defvjp(fwd, bwd)
    return op


def _loss_call(y, target, n_ctx_tiles, tile, name):
    s, dm = y.shape

    def body(y_ref, t_ref, loss_ref, dy_ref):
        i = pl.program_id(0)

        @pl.when(i == 0)
        def _():
            loss_ref[...] = jnp.zeros_like(loss_ref)

        @pl.when(i < n_ctx_tiles)
        def _():
            dy_ref[...] = jnp.zeros_like(dy_ref)

        @pl.when(i >= n_ctx_tiles)
        def _():
            err = y_ref[...] - t_ref[...]
            dy_ref[...] = err * (1.0 / dm)
            loss_ref[...] += jnp.broadcast_to(_sum_all(err * err) * (0.5 / dm), loss_ref.shape)

    return pl.pallas_call(
        body, name=name,
        grid=(s // tile,),
        in_specs=[pl.BlockSpec((tile, dm), lambda i: (i, 0)),
                  pl.BlockSpec((tile, dm), lambda i: (jnp.maximum(i - n_ctx_tiles, 0), 0))],
        out_specs=[pl.BlockSpec((8, _LANES), lambda i: (0, 0)),
                   pl.BlockSpec((tile, dm), lambda i: (i, 0))],
        out_shape=[jax.ShapeDtypeStruct((8, _LANES), F32), jax.ShapeDtypeStruct((s, dm), F32)],
        compiler_params=_cparams(("arbitrary",)),
    )(y, target)


def _loss_op(n_ctx_tiles, tile):
    @jax.custom_vjp
    def op(y, target):
        return _loss_call(y, target, n_ctx_tiles, tile, "loss_head")[0][0, 0]

    def fwd(y, target):
        loss, dy = _loss_call(y, target, n_ctx_tiles, tile, "loss_head")
        return loss[0, 0], (dy, target)

    def bwd(res, g):
        dy, target = res
        return dy * g, jnp.zeros_like(target)

    op.defvjp(fwd, bwd)
    return op


def _exchange(x, gather, name):
    blk_shape = x.shape if gather else x.shape[1:]

    def body(x_ref, o_ref, send_sems, recv_sems, local_sem):
        mx, my, mc = lax.axis_index("x"), lax.axis_index("y"), lax.axis_index("c")
        me = 4 * mx + 2 * my + mc
        copies = []
        for rel in range(1, _N_DEV):
            px = mx ^ ((rel >> 2) & 1)
            py = my ^ ((rel >> 1) & 1)
            pc = mc ^ (rel & 1)
            src = x_ref if gather else x_ref.at[4 * px + 2 * py + pc]
            cp = pltpu.make_async_remote_copy(
                src_ref=src, dst_ref=o_ref.at[me],
                send_sem=send_sems.at[rel - 1], recv_sem=recv_sems.at[rel - 1],
                device_id=(px, py, pc), device_id_type=pl.DeviceIdType.MESH)
            cp.start()
            copies.append(cp)
        mine = pltpu.make_async_copy(x_ref if gather else x_ref.at[me], o_ref.at[me], local_sem)
        mine.start()
        for cp in copies:
            cp.wait()
        mine.wait()

    return pl.pallas_call(
        body, name=name,
        in_specs=[pl.BlockSpec(memory_space=pltpu.HBM)],
        out_specs=pl.BlockSpec(memory_space=pltpu.HBM),
        out_shape=jax.ShapeDtypeStruct((_N_DEV,) + tuple(blk_shape), x.dtype),
        scratch_shapes=[pltpu.SemaphoreType.DMA((_N_DEV - 1,)), pltpu.SemaphoreType.DMA((_N_DEV - 1,)),
                        pltpu.SemaphoreType.DMA(())],
    )(x)


def _sum_parts(parts, name, out_dtype=F32):
    n, r, w = parts.shape
    tile = _pick(r, max(8, (1 << 19) // (w * n) // 16 * 16), 16)

    def body(p_ref, o_ref):
        acc = p_ref[0].astype(F32)
        for j in range(1, n):
            acc = acc + p_ref[j].astype(F32)
        o_ref[...] = acc.astype(o_ref.dtype)

    return pl.pallas_call(
        body, name=name, grid=(r // tile,),
        in_specs=[pl.BlockSpec((n, tile, w), lambda i: (0, i, 0))],
        out_specs=pl.BlockSpec((tile, w), lambda i: (i, 0)),
        out_shape=jax.ShapeDtypeStruct((r, w), out_dtype),
        compiler_params=_cparams(("arbitrary",)),
    )(parts)


def _chip_peers(mx, my):
    return [(1 - mx, my), (mx, 1 - my), (1 - mx, 1 - my)]


def _gather_two_level(xs, name):
    n_ops = len(xs)

    def body(*refs):
        x_refs, o_refs = refs[:n_ops], refs[n_ops:2 * n_ops]
        send_sems, recv_sems, local_sems = refs[2 * n_ops:]
        mx, my, mc = lax.axis_index("x"), lax.axis_index("y"), lax.axis_index("c")
        sibling = (mx, my, 1 - mc)
        chips = _chip_peers(mx, my)

        def copy(op, k, block, to, src=None):
            idx = 4 * block[0] + 2 * block[1] + block[2]
            dst = o_refs[op].at[idx]
            return pltpu.make_async_remote_copy(
                src_ref=dst if src is None else src, dst_ref=dst,
                send_sem=send_sems.at[op * 7 + k], recv_sem=recv_sems.at[op * 7 + k],
                device_id=to, device_id_type=pl.DeviceIdType.MESH)

        me = (mx, my, mc)
        mine, first, passed = [], [], []
        for op in range(n_ops):
            cp = pltpu.make_async_copy(x_refs[op], o_refs[op].at[4 * mx + 2 * my + mc], local_sems.at[op])
            cp.start()
            mine.append(cp)
            first.append(copy(op, 0, me, sibling, src=x_refs[op]))
            for j, chip in enumerate(chips):
                first.append(copy(op, 1 + j, me, (*chip, mc), src=x_refs[op]))
        for cp in first:
            cp.start()
        for j, chip in enumerate(chips):
            for op in range(n_ops):
                copy(op, 1 + j, (*chip, mc), me).wait_recv()
                cp = copy(op, 4 + j, (*chip, mc), sibling)
                cp.start()
                passed.append(cp)
        for op in range(n_ops):
            copy(op, 0, sibling, me).wait_recv()
            for j, chip in enumerate(chips):
                copy(op, 4 + j, (*chip, 1 - mc), me).wait_recv()
        for cp in first + passed:
            cp.wait_send()
        for cp in mine:
            cp.wait()

    hbm = pl.BlockSpec(memory_space=pltpu.HBM)
    return pl.pallas_call(
        body, name=name,
        in_specs=[hbm] * n_ops, out_specs=[hbm] * n_ops,
        out_shape=[jax.ShapeDtypeStruct((_N_DEV,) + tuple(x.shape), x.dtype) for x in xs],
        scratch_shapes=[pltpu.SemaphoreType.DMA((7 * n_ops,)), pltpu.SemaphoreType.DMA((7 * n_ops,)),
                        pltpu.SemaphoreType.DMA((n_ops,))],
    )(*xs)


def _swap_with_sibling(xs, name):
    n_ops = len(xs)

    def body(*refs):
        x_refs, o_refs = refs[:n_ops], refs[n_ops:2 * n_ops]
        send_sems, recv_sems = refs[2 * n_ops:]
        mx, my, mc = lax.axis_index("x"), lax.axis_index("y"), lax.axis_index("c")
        copies = []
        for op in range(n_ops):
            for q in range(4):
                rem = pltpu.make_async_remote_copy(
                    src_ref=x_refs[op].at[q], dst_ref=o_refs[op].at[q],
                    send_sem=send_sems.at[op * 4 + q], recv_sem=recv_sems.at[op * 4 + q],
                    device_id=(mx, my, 1 - mc), device_id_type=pl.DeviceIdType.MESH)
                rem.start()
                copies.append(rem)
        for cp in copies:
            cp.wait()

    hbm = pl.BlockSpec(memory_space=pltpu.HBM)
    return pl.pallas_call(
        body, name=name,
        in_specs=[hbm] * n_ops, out_specs=[hbm] * n_ops,
        out_shape=[jax.ShapeDtypeStruct(x.shape, x.dtype) for x in xs],
        scratch_shapes=[pltpu.SemaphoreType.DMA((4 * n_ops,)), pltpu.SemaphoreType.DMA((4 * n_ops,))],
    )(*xs)


def _add2(a, b, name, out_dtype):
    shape = a.shape
    a2, b2 = a.reshape(-1, shape[-1]), b.reshape(-1, shape[-1])
    r, w = a2.shape
    tile = _pick(r, max(16, (1 << 19) // w // 16 * 16), 16)

    def body(a_ref, b_ref, o_ref):
        o_ref[...] = (a_ref[...].astype(F32) + b_ref[...].astype(F32)).astype(o_ref.dtype)

    spec = pl.BlockSpec((tile, w), lambda i: (i, 0))
    return pl.pallas_call(
        body, name=name, grid=(r // tile,), in_specs=[spec, spec], out_specs=spec,
        out_shape=jax.ShapeDtypeStruct((r, w), out_dtype),
        compiler_params=_cparams(("arbitrary",)),
    )(a2, b2).reshape(shape)


def _scatter_chips(xs, name):
    n_ops = len(xs)

    def body(*refs):
        x_refs, o_refs = refs[:n_ops], refs[n_ops:2 * n_ops]
        send_sems, recv_sems, local_sems = refs[2 * n_ops:]
        mx, my, mc = lax.axis_index("x"), lax.axis_index("y"), lax.axis_index("c")
        my_chip = 2 * mx + my
        copies = []
        for op in range(n_ops):
            loc = pltpu.make_async_copy(x_refs[op].at[my_chip], o_refs[op].at[my_chip], local_sems.at[op])
            loc.start()
            copies.append(loc)
            for j, (px, py) in enumerate(_chip_peers(mx, my)):
                rem = pltpu.make_async_remote_copy(
                    src_ref=x_refs[op].at[2 * px + py], dst_ref=o_refs[op].at[my_chip],
                    send_sem=send_sems.at[op * 3 + j], recv_sem=recv_sems.at[op * 3 + j],
                    device_id=(px, py, mc), device_id_type=pl.DeviceIdType.MESH)
                rem.start()
                copies.append(rem)
        for cp in copies:
            cp.wait()

    hbm = pl.BlockSpec(memory_space=pltpu.HBM)
    return pl.pallas_call(
        body, name=name,
        in_specs=[hbm] * n_ops, out_specs=[hbm] * n_ops,
        out_shape=[jax.ShapeDtypeStruct(x.shape, x.dtype) for x in xs],
        scratch_shapes=[pltpu.SemaphoreType.DMA((3 * n_ops,)), pltpu.SemaphoreType.DMA((3 * n_ops,)),
                        pltpu.SemaphoreType.DMA((n_ops,))],
    )(*xs)


def _weights_gather_op(name):
    def impl(shards):
        got = _gather_two_level([s.astype(_MXU_DTYPE) for s in shards], name + "_gather")
        return tuple(g.astype(F32) for g in got)

    @jax.custom_vjp
    def op(shards):
        return impl(shards)

    def fwd(shards):
        return impl(shards), None

    def bwd(_, cts):
        mc = lax.axis_index("c")
        by_core = [jnp.swapaxes(g.astype(_MXU_DTYPE).reshape((4, 2) + g.shape[1:]), 0, 1) for g in cts]
        mine = [lax.dynamic_index_in_dim(p, mc, 0, keepdims=False) for p in by_core]
        theirs = [lax.dynamic_index_in_dim(p, 1 - mc, 0, keepdims=False) for p in by_core]
        got = _swap_with_sibling(theirs, name + "_scatter_pair")
        chip_sums = [_add2(a, b, "%s_pair_sum%d" % (name, k), _MXU_DTYPE) for k, (a, b) in enumerate(zip(mine, got))]
        crossed = _scatter_chips(chip_sums, name + "_scatter_chips")
        out = []
        for k, q in enumerate(crossed):
            flat = q.reshape(4, -1, q.shape[-1])
            out.append(_sum_parts(flat, "%s_chip_sum%d" % (name, k), F32).reshape(q.shape[1:]))
        return (tuple(out),)

    op.defvjp(fwd, bwd)
    return op


def _all_gather_op(name, payload_dtype):
    def impl(x):
        return _exchange(x.astype(payload_dtype), True, name + "_gather").astype(F32)

    @jax.custom_vjp
    def op(x):
        return impl(x)

    def fwd(x):
        return impl(x), None

    def bwd(_, g):
        return (_sum_parts(_exchange(g, False, name + "_scatter"), name + "_sum"),)

    op.defvjp(fwd, bwd)
    return op


def _adamw(w, g, m, v, partial, name):
    r, wd = w.shape
    tile = _pick(r, max(8, (1 << 20) // (4 * wd) // 8 * 8), 8)
    c1 = 1.0 / (1.0 - ADAM_B1 ** ADAM_STEP)
    c2 = 1.0 / (1.0 - ADAM_B2 ** ADAM_STEP)

    def body(w_ref, g_ref, m_ref, v_ref, go_ref, d_ref, mo_ref, vo_ref):
        if partial:
            g = g_ref[0]
            for j in range(1, _N_DEV):
                g = g + g_ref[j]
        else:
            g = g_ref[...]
        m_new = ADAM_B1 * m_ref[...] + (1.0 - ADAM_B1) * g
        v_new = ADAM_B2 * v_ref[...] + (1.0 - ADAM_B2) * (g * g)
        m_hat = m_new * c1
        v_hat = v_new * c2
        go_ref[...] = g
        d_ref[...] = -ADAM_LR * (m_hat / (jnp.sqrt(v_hat) + ADAM_EPS) + ADAM_WD * w_ref[...])
        mo_ref[...] = m_new
        vo_ref[...] = v_new

    spec = pl.BlockSpec((tile, wd), lambda i: (i, 0))
    g_spec = pl.BlockSpec((_N_DEV, tile, wd), lambda i: (0, i, 0)) if partial else spec
    return pl.pallas_call(
        body, name=name, grid=(r // tile,),
        in_specs=[spec, g_spec, spec, spec],
        out_specs=[spec] * 4,
        out_shape=[jax.ShapeDtypeStruct((r, wd), F32)] * 4,
        compiler_params=_cparams(("arbitrary",)),
    )(w, g, m, v)


def _pack(arrays):
    flat, meta, off = [], [], 0
    for a in arrays:
        n = int(np.prod(a.shape))
        pad = (-n) % _LANES
        flat.append(a.reshape(-1))
        if pad:
            flat.append(jnp.zeros((pad,), a.dtype))
        meta.append((off, a.shape))
        off += n + pad
    pad = (-off) % (8 * _LANES)
    if pad:
        flat.append(jnp.zeros((pad,), arrays[0].dtype))
    return jnp.concatenate(flat).reshape(-1, _LANES), meta


def _unpack(packed, meta):
    flat = packed.reshape(-1)
    return [flat[off:off + int(np.prod(shape))].reshape(shape) for off, shape in meta]


def _rope_tables(t, l, dim, width):
    rows = l // GRID_W
    r = np.repeat(np.arange(rows, dtype=np.float32), GRID_W)
    cc = np.tile(np.arange(GRID_W, dtype=np.float32), rows)
    n_freq = dim // 4
    inv = jnp.asarray(ROPE_THETA, F32) ** (-jnp.arange(n_freq, dtype=F32) / n_freq)
    ang_r = jnp.asarray(r)[:, None] * inv
    ang_c = jnp.asarray(cc)[:, None] * inv
    ang = jnp.concatenate([ang_r, ang_r, ang_c, ang_c], axis=-1)
    cos, sin = jnp.cos(ang), jnp.sin(ang)
    if width > dim:
        cos = jnp.concatenate([cos, jnp.ones((l, width - dim), F32)], axis=1)
        sin = jnp.concatenate([sin, jnp.zeros((l, width - dim), F32)], axis=1)
    cos = jnp.concatenate([jnp.ones((t, width), F32), cos], axis=0)
    sin = jnp.concatenate([jnp.zeros((t, width), F32), sin], axis=0)
    return cos, sin


def _full_weights(gathered):
    full = {}
    for n, g in zip(GATHERED, gathered):
        if GATHER_AXIS[n] == 0:
            full[n] = g.reshape(g.shape[0] * g.shape[1], g.shape[2])
        else:
            full[n] = jnp.transpose(g, (1, 0, 2)).reshape(g.shape[1], g.shape[0] * g.shape[2])
    return full


def _layer(l, stream, mod, wts, small, tables, dims, last):
    s, t, dm, dff = dims
    tile = min(256, t)
    n_ctx_tiles = t // tile
    tile_w = min(128, t)
    n_ctx_tiles_w = t // tile_w
    cos_h, sin_h, cos_m, sin_m = tables
    tag = "l%d_" % l

    def mrow(k):
        return mod[:, k:k + 1, :]

    def one(a):
        return a.reshape(1, 1, -1)

    w_in = jnp.concatenate([wts['w_in'], jnp.zeros((dm, IN_WIDTH_PAD - IN_WIDTH), F32)], axis=1)
    modulate = _rowwise(_fn_modulate, tag + "modulate", 1, [[(0, dm)]], [True], [True, True], [dm],
                        tile, n_ctx_tiles)
    (h,) = modulate((stream,), (mrow(0), mrow(1)))
    proj = _linear(tag + "w_in")(h, w_in)

    in_parts = [(0, 512), (512, 512), (1024, 512), (1536, 512), (2048, 768), (2816, 256), (3072, 256),
                (3328, 512), (3840, 256), (4096, 128)]
    postproj = _rowwise(_fn_postproj, tag + "postproj", 5,
                        [in_parts, [(0, 128)], [(0, 128)], [(0, 128)], [(0, 128)]],
                        [True, False, False, False, False], [True, True],
                        [512, 512, 512, 512, 768, 256, 256, 512, 256, 128], tile_w, n_ctx_tiles_w)
    (rq, rk, rv, rg, sq, sk, sv, cqn, ckvn, kr) = postproj(
        (proj, cos_h, sin_h, cos_m, sin_m), (one(small['mla_q_norm'][l]), one(small['mla_kv_norm'][l])))

    lg = jnp.stack([jax.nn.log_sigmoid(small['ret_decay_fwd'][l]), jax.nn.log_sigmoid(small['ret_decay_bwd'][l])])
    ret = _retention(t // RET_CHUNK, tag + "retention")(rq, rk, rv, lg)

    swa_blk = 256 if (t % 256 == 0 and s % 256 == 0) else 128
    swa_cfg = _AttnCfg(tag + "swa", s, t, SWA_HEADS, SWA_KV_HEADS, HEAD_DIM, HEAD_DIM, swa_blk, True,
                       HEAD_DIM ** -0.5, True)
    y_swa = _attention(swa_cfg)(sq, sk, sv, small['swa_sink'][l])

    w_uq = wts['mla_w_uq'].reshape(MLA_Q_RANK, MLA_HEADS, MLA_NOPE + MLA_ROPE)
    w_uq = jnp.concatenate([w_uq, jnp.zeros((MLA_Q_RANK, MLA_HEADS, 256 - MLA_NOPE - MLA_ROPE), F32)], axis=2)
    w_uq = w_uq.reshape(MLA_Q_RANK, MLA_HEADS * 256)
    w_ukv = wts['mla_w_ukv'].reshape(MLA_KV_RANK, MLA_HEADS, MLA_NOPE + MLA_V)
    w_ukv = jnp.concatenate([w_ukv[:, :, :MLA_NOPE].reshape(MLA_KV_RANK, -1),
                             w_ukv[:, :, MLA_NOPE:].reshape(MLA_KV_RANK, -1)], axis=1)
    q_lin = _linear(tag + "w_uq")(cqn, w_uq)
    kv_lin = _linear(tag + "w_ukv")(ckvn, w_ukv)
    kv_parts = [(hh * 128, 128) for hh in range(MLA_HEADS)] + [(MLA_HEADS * 128, MLA_HEADS * 128)]
    assemble = _rowwise(_fn_mla_assemble, tag + "mla_assemble", 5,
                        [[(0, MLA_HEADS * 256)], kv_parts, [(0, 128)], [(0, 128)], [(0, 128)]],
                        [True, True, True, False, False], [],
                        [MLA_HEADS * 256, MLA_HEADS * 256, MLA_HEADS * 128], tile_w, n_ctx_tiles_w)
    (q_full, k_full, v_mla) = assemble((q_lin, kv_lin, kr, cos_m, sin_m), ())
    mla_cfg = _AttnCfg(tag + "mla", s, t, MLA_HEADS, MLA_HEADS, 256, MLA_V, _pick(s, 768, 128), False,
                       MLA_SCALE, False)
    y_mla = _attention(mla_cfg)(q_full, k_full, v_mla, jnp.zeros((MLA_HEADS,), F32))

    hparts = [(hh * 128, 128) for hh in range(RET_HEADS)]
    mix_op = _rowwise(_fn_mix, tag + "mix", 5, [hparts, hparts, hparts, [(0, 768)], [(0, 768)]],
                      [True] * 5, [], [dm_mix()], tile, n_ctx_tiles)
    (mix_in,) = mix_op((ret[0], ret[1], rg, y_swa, y_mla), ())
    mix = _linear(tag + "w_o")(mix_in, wts['w_o'])
    ln1 = _rowwise(_fn_ln1, tag + "ln1", 2, [[(0, dm)], [(0, dm)]], [True, True], [True] * 5, [dm, dm],
                   tile, n_ctx_tiles)
    x_a, h2 = ln1((stream, mix), (mrow(2), one(small['ln1_g'][l]), one(small['ln1_b'][l]), mrow(3), mrow(4)))
    u_lin = _linear(tag + "w_up_u")(h2, wts['ffn_w_up'][:, :dff])
    g_lin = _linear(tag + "w_up_g")(h2, wts['ffn_w_up'][:, dff:])
    conv_par = jnp.concatenate([wts['ffn_conv_w'], small['ffn_conv_b'][l][None, :], jnp.zeros((4, dff), F32)], axis=0)
    y = _conv_gate(t, tag + "conv_gate")(u_lin, g_lin, conv_par)
    f = _linear(tag + "w_down")(y, wts['ffn_w_down'])
    ln2 = _rowwise(_fn_ln2, tag + "ln2", 2, [[(0, dm)], [(0, dm)]], [True, True], [True] * 3, [dm],
                   tile, n_ctx_tiles)
    (out,) = ln2((x_a, f), (mrow(5), one(small['ln2_g'][l]), one(small['ln2_b'][l])))
    return out


def dm_mix():
    return RET_DIM + SWA_HEADS * HEAD_DIM + MLA_HEADS * MLA_V


def kernel(x, c, ctx, c_ctx, ada_w, ada_b, w_in, ret_decay_fwd, ret_decay_bwd, swa_sink, mla_q_norm, mla_w_uq, mla_kv_norm, mla_w_ukv, w_o, ln1_g, ln1_b, ffn_w_up, ffn_conv_w, ffn_conv_b, ffn_w_down, ln2_g, ln2_b, loss_target, m_c_ctx, m_ada_w, m_ada_b, m_w_in, m_ret_decay_fwd, m_ret_decay_bwd, m_swa_sink, m_mla_q_norm, m_mla_w_uq, m_mla_kv_norm, m_mla_w_ukv, m_w_o, m_ln1_g, m_ln1_b, m_ffn_w_up, m_ffn_conv_w, m_ffn_conv_b, m_ffn_w_down, m_ln2_g, m_ln2_b, v_c_ctx, v_ada_w, v_ada_b, v_w_in, v_ret_decay_fwd, v_ret_decay_bwd, v_swa_sink, v_mla_q_norm, v_mla_w_uq, v_mla_kv_norm, v_mla_w_ukv, v_w_o, v_ln1_g, v_ln1_b, v_ffn_w_up, v_ffn_conv_w, v_ffn_conv_b, v_ffn_w_down, v_ln2_g, v_ln2_b):
    weights = dict(c_ctx=c_ctx, ada_w=ada_w, ada_b=ada_b, w_in=w_in, ret_decay_fwd=ret_decay_fwd,
                   ret_decay_bwd=ret_decay_bwd, swa_sink=swa_sink, mla_q_norm=mla_q_norm, mla_w_uq=mla_w_uq,
                   mla_kv_norm=mla_kv_norm, mla_w_ukv=mla_w_ukv, w_o=w_o, ln1_g=ln1_g, ln1_b=ln1_b,
                   ffn_w_up=ffn_w_up, ffn_conv_w=ffn_conv_w, ffn_conv_b=ffn_conv_b, ffn_w_down=ffn_w_down,
                   ln2_g=ln2_g, ln2_b=ln2_b)
    m_in = dict(c_ctx=m_c_ctx, ada_w=m_ada_w, ada_b=m_ada_b, w_in=m_w_in, ret_decay_fwd=m_ret_decay_fwd,
                ret_decay_bwd=m_ret_decay_bwd, swa_sink=m_swa_sink, mla_q_norm=m_mla_q_norm, mla_w_uq=m_mla_w_uq,
                mla_kv_norm=m_mla_kv_norm, mla_w_ukv=m_mla_w_ukv, w_o=m_w_o, ln1_g=m_ln1_g, ln1_b=m_ln1_b,
                ffn_w_up=m_ffn_w_up, ffn_conv_w=m_ffn_conv_w, ffn_conv_b=m_ffn_conv_b, ffn_w_down=m_ffn_w_down,
                ln2_g=m_ln2_g, ln2_b=m_ln2_b)
    v_in = dict(c_ctx=v_c_ctx, ada_w=v_ada_w, ada_b=v_ada_b, w_in=v_w_in, ret_decay_fwd=v_ret_decay_fwd,
                ret_decay_bwd=v_ret_decay_bwd, swa_sink=v_swa_sink, mla_q_norm=v_mla_q_norm, mla_w_uq=v_mla_w_uq,
                mla_kv_norm=v_mla_kv_norm, mla_w_ukv=v_mla_w_ukv, w_o=v_w_o, ln1_g=v_ln1_g, ln1_b=v_ln1_b,
                ffn_w_up=v_ffn_w_up, ffn_conv_w=v_ffn_conv_w, ffn_conv_b=v_ffn_conv_b, ffn_w_down=v_ffn_w_down,
                ln2_g=v_ln2_g, ln2_b=v_ln2_b)

    l_tok, dm = x.shape[1], x.shape[2]
    t = ctx.shape[1]
    s = t + l_tok
    dff = ffn_w_down.shape[1] * _N_DEV
    dims = (s, t, dm, dff)
    me = 4 * lax.axis_index("x") + 2 * lax.axis_index("y") + lax.axis_index("c")
    cos_h, sin_h = _rope_tables(t, l_tok, HEAD_DIM, HEAD_DIM)
    cos_m, sin_m = _rope_tables(t, l_tok, MLA_ROPE, _LANES)
    tables = (cos_h, sin_h, cos_m, sin_m)
    c_all = _exchange(c, True, "gather_cond").reshape(_N_DEV, dm)
    tile = min(256, t)

    def loss_fn(wd, xin):
        mod_rows = jnp.concatenate([jax.nn.silu(c_all), jax.nn.silu(wd['c_ctx'])[None, :],
                                  jnp.zeros((_MOD_ROWS - _N_DEV - 1, dm), F32)], axis=0)
        mods_shard = jnp.stack([_linear("ada_l%d" % l)(mod_rows, wd['ada_w'][l]) for l in range(DEPTH)])
        n_sh = mods_shard.shape[-1]
        mods_all = _all_gather_op("mods", F32)(mods_shard.reshape(DEPTH * _MOD_ROWS, n_sh))
        mods_all = mods_all.reshape(_N_DEV, DEPTH, _MOD_ROWS, n_sh).transpose(1, 2, 0, 3).reshape(DEPTH, _MOD_ROWS, _N_DEV * n_sh)
        mods_all = mods_all + wd['ada_b'][:, None, :]
        mod_x = lax.dynamic_slice_in_dim(mods_all, me, 1, axis=1)[:, 0]
        mod_c = mods_all[:, _N_DEV]
        stream = jnp.concatenate([ctx[0], xin[0]], axis=0)
        for l in range(DEPTH):
            gathered = _weights_gather_op("weights_l%d" % l)(tuple(wd[n][l] for n in GATHERED))
            full = _full_weights(gathered)
            mod = jnp.stack([mod_c[l].reshape(N_MOD, dm), mod_x[l].reshape(N_MOD, dm)])
            stream = _layer(l, stream, mod, full, wd, tables, dims, l == DEPTH - 1)
        return _loss_op(t // tile, tile)(stream, loss_target[0])

    loss_local, (gw, gx) = jax.value_and_grad(loss_fn, argnums=(0, 1))(weights, x)
    loss = lax.psum(loss_local, ("x", "y", "c"))

    grads, deltas, new_m, new_v = {}, {}, {}, {}

    def as2d(a):
        return a.reshape(-1, a.shape[-1])

    for n in ['ada_w'] + GATHERED:
        g2, d2, m2, v2 = _adamw(as2d(weights[n]), as2d(gw[n]), as2d(m_in[n]), as2d(v_in[n]), False, "adamw_" + n)
        shp = weights[n].shape
        grads[n], deltas[n], new_m[n], new_v[n] = g2.reshape(shp), d2.reshape(shp), m2.reshape(shp), v2.reshape(shp)

    w_pack, meta = _pack([weights[n] for n in REPLICATED])
    g_pack, _ = _pack([gw[n] for n in REPLICATED])
    m_pack, _ = _pack([m_in[n] for n in REPLICATED])
    v_pack, _ = _pack([v_in[n] for n in REPLICATED])
    g_parts = _exchange(g_pack, True, "gather_small_grads")
    outs = _adamw(w_pack, g_parts, m_pack, v_pack, True, "adamw_replicated")
    for dst, packed in zip((grads, deltas, new_m, new_v), outs):
        for n, a in zip(REPLICATED, _unpack(packed, meta)):
            dst[n] = a

    return (loss, gx, *[grads[n] for n in WEIGHTS], *[deltas[n] for n in WEIGHTS],
            *[new_m[n] for n in WEIGHTS], *[new_v[n] for n in WEIGHTS])
```

```python
import functools
import math

import numpy as np
import jax
import jax.numpy as jnp
from jax import lax
from jax.experimental import pallas as pl
from jax.experimental.pallas import tpu as pltpu

F32 = jnp.float32
_MXU_DTYPE = jnp.bfloat16
_VMEM_LIMIT_BYTES = 56 * 1024 * 1024
_LANES = 128
_N_DEV = 8
_MOD_ROWS = 128

DEPTH = 4
HEAD_DIM = 128
ROPE_THETA = 10000.0
GRID_W = 64
RET_HEADS = 4
RET_DIM = RET_HEADS * HEAD_DIM
RET_CHUNK = 128
SWA_HEADS = 6
SWA_KV_HEADS = 2
SWA_WINDOW = 128
MLA_HEADS = 6
MLA_Q_RANK = 512
MLA_KV_RANK = 256
MLA_NOPE = 128
MLA_ROPE = 64
MLA_V = 128
MLA_SCALE = (MLA_NOPE + MLA_ROPE) ** -0.5
_LOG2E = math.log2(math.e)
N_MOD = 6
LN_EPS = 1e-5
RMS_EPS = 1e-6
NEG_INF = -1e30
ALPHA = (2 * DEPTH) ** 0.25
IN_WIDTH = 4160
IN_WIDTH_PAD = 4224

ADAM_LR = 0.001
ADAM_B1 = 0.9
ADAM_B2 = 0.999
ADAM_EPS = 1e-08
ADAM_WD = 0.01
ADAM_STEP = 10

WEIGHTS = ['c_ctx', 'ada_w', 'ada_b', 'w_in', 'ret_decay_fwd', 'ret_decay_bwd', 'swa_sink', 'mla_q_norm',
           'mla_w_uq', 'mla_kv_norm', 'mla_w_ukv', 'w_o', 'ln1_g', 'ln1_b', 'ffn_w_up', 'ffn_conv_w',
           'ffn_conv_b', 'ffn_w_down', 'ln2_g', 'ln2_b']
GATHERED = ['w_in', 'mla_w_uq', 'mla_w_ukv', 'w_o', 'ffn_w_up', 'ffn_conv_w', 'ffn_w_down']
GATHER_AXIS = {'w_in': 1, 'mla_w_uq': 1, 'mla_w_ukv': 1, 'w_o': 0, 'ffn_w_up': 1, 'ffn_conv_w': 1, 'ffn_w_down': 0}
REPLICATED = ['c_ctx', 'ada_b', 'ret_decay_fwd', 'ret_decay_bwd', 'swa_sink', 'mla_q_norm', 'mla_kv_norm',
              'ln1_g', 'ln1_b', 'ffn_conv_b', 'ln2_g', 'ln2_b']


def _cparams(semantics):
    return pltpu.CompilerParams(dimension_semantics=semantics, vmem_limit_bytes=_VMEM_LIMIT_BYTES)


def _pick(n, target, align):
    best = None
    d = align
    while d <= min(n, target):
        if n % d == 0:
            best = d
        d += align
    return n if best is None else best


def _matmul(a, b, mode, name, out_dtype=F32):
    if mode == 'nn':
        (m, k), (k2, n) = a.shape, b.shape
    elif mode == 'nt':
        (m, k), (n, k2) = a.shape, b.shape
    else:
        (k, m), (k2, n) = a.shape, b.shape
    assert k == k2, (a.shape, b.shape, mode)
    tm = _pick(m, 1024, 128)
    tn = _pick(n, 1408, 128)
    tk = _pick(k, 1408 if mode == 'tn' else 2048, 128)
    nk = k // tk

    def body(a_ref, b_ref, o_ref, *scratch):
        kk = pl.program_id(2)
        if mode == 'nn':
            part = jnp.dot(a_ref[...].astype(_MXU_DTYPE), b_ref[...].astype(_MXU_DTYPE),
                           preferred_element_type=F32)
        elif mode == 'nt':
            part = lax.dot_general(a_ref[...].astype(_MXU_DTYPE), b_ref[...].astype(_MXU_DTYPE),
                                   (((1,), (1,)), ((), ())), preferred_element_type=F32)
        else:
            at = a_ref[...].astype(F32).T.astype(_MXU_DTYPE)
            part = jnp.dot(at, b_ref[...].astype(_MXU_DTYPE), preferred_element_type=F32)
        if nk == 1:
            o_ref[...] = part.astype(o_ref.dtype)
            return
        acc_ref, = scratch

        @pl.when(kk == 0)
        def _():
            acc_ref[...] = part

        @pl.when((kk > 0) & (kk < nk - 1))
        def _():
            acc_ref[...] += part

        @pl.when(kk == nk - 1)
        def _():
            o_ref[...] = (acc_ref[...] + part).astype(o_ref.dtype)

    if mode == 'nn':
        a_spec = pl.BlockSpec((tm, tk), lambda i, j, kk: (i, kk))
        b_spec = pl.BlockSpec((tk, tn), lambda i, j, kk: (kk, j))
    elif mode == 'nt':
        a_spec = pl.BlockSpec((tm, tk), lambda i, j, kk: (i, kk))
        b_spec = pl.BlockSpec((tn, tk), lambda i, j, kk: (j, kk))
    else:
        a_spec = pl.BlockSpec((tk, tm), lambda i, j, kk: (kk, i))
        b_spec = pl.BlockSpec((tk, tn), lambda i, j, kk: (kk, j))
    return pl.pallas_call(
        body, name=name,
        grid=(m // tm, n // tn, nk),
        in_specs=[a_spec, b_spec],
        out_specs=pl.BlockSpec((tm, tn), lambda i, j, kk: (i, j)),
        out_shape=jax.ShapeDtypeStruct((m, n), out_dtype),
        scratch_shapes=[pltpu.VMEM((tm, tn), F32)] if nk > 1 else [],
        compiler_params=_cparams(("parallel", "parallel", "arbitrary")),
    )(a, b)


def _linear(name):
    @jax.custom_vjp
    def op(a, w):
        return _matmul(a.astype(_MXU_DTYPE), w.astype(_MXU_DTYPE), 'nn', name + "_fwd")

    def fwd(a, w):
        ab, wb = a.astype(_MXU_DTYPE), w.astype(_MXU_DTYPE)
        return _matmul(ab, wb, 'nn', name + "_fwd"), (ab, wb.T, jnp.zeros((0,), a.dtype))

    def bwd(res, g):
        ab, wbt, a_like = res
        gb = g.astype(_MXU_DTYPE)
        da = _matmul(gb, wbt, 'nn', name + "_da", a_like.dtype)
        dw = _matmul(ab, gb, 'tn', name + "_dw")
        return da, dw

    op.defvjp(fwd, bwd)
    return op


def _pieces(parts):
    out = []
    for p, (start, width) in enumerate(parts):
        pw = math.gcd(start, width) if start else width
        assert pw % _LANES == 0, (start, width)
        for t in range(width // pw):
            out.append((p, pw, start // pw + t))
    return out


def _rowwise(fn, name, rows, parts, diff, pdiff, out_widths, tile, n_ctx_tiles, col_tile=None, mxu_outs=()):
    pieces = [_pieces(p) for p in parts]

    def sel_of(i, n_sel):
        return jnp.where(i >= n_ctx_tiles, n_sel - 1, 0)

    def in_specs_for(row_arrays, params):
        specs, operands = [], []
        for r in range(rows):
            for (_, pw, blk) in pieces[r]:
                if col_tile is None:
                    specs.append(pl.BlockSpec((tile, pw), lambda j, i, blk=blk: (i, blk)))
                else:
                    nb = pw // col_tile
                    specs.append(pl.BlockSpec((tile, col_tile), lambda j, i, blk=blk, nb=nb: (i, blk * nb + j)))
                operands.append(row_arrays[r])
        for p in params:
            n_sel, _, w = p.shape
            cw = w if col_tile is None else col_tile
            if col_tile is None:
                specs.append(pl.BlockSpec((None, 1, cw), lambda j, i, n_sel=n_sel: (sel_of(i, n_sel), 0, 0)))
            else:
                specs.append(pl.BlockSpec((None, 1, cw), lambda j, i, n_sel=n_sel: (sel_of(i, n_sel), 0, j)))
            operands.append(p)
        return specs, operands

    def load_inputs(refs):
        k = 0
        vals = []
        for r in range(rows):
            got = [[] for _ in parts[r]]
            for (p, _, _) in pieces[r]:
                got[p].append(refs[k][...].astype(F32))
                k += 1
            vals.append([g[0] if len(g) == 1 else jnp.concatenate(g, axis=1) for g in got])
        return vals, k

    def forward(row_arrays, params):
        s = row_arrays[0].shape[0]
        ncol = 1 if col_tile is None else out_widths[0] // col_tile
        n_par = len(params)

        def body(*refs):
            vals, k = load_inputs(refs)
            pvals = [refs[k + q][...].astype(F32) for q in range(n_par)]
            outs = fn(vals, pvals)
            for o_ref, o in zip(refs[k + n_par:], outs):
                o_ref[...] = o.astype(o_ref.dtype)

        specs, operands = in_specs_for(row_arrays, params)
        if col_tile is None:
            out_specs = [pl.BlockSpec((tile, w), lambda j, i: (i, 0)) for w in out_widths]
        else:
            out_specs = [pl.BlockSpec((tile, col_tile), lambda j, i: (i, j)) for _ in out_widths]
        return pl.pallas_call(
            body, name=name + "_fwd",
            grid=(ncol, s // tile),
            in_specs=specs, out_specs=out_specs,
            out_shape=[jax.ShapeDtypeStruct((s, w), _MXU_DTYPE if o in mxu_outs else F32)
                       for o, w in enumerate(out_widths)],
            compiler_params=_cparams(("arbitrary", "arbitrary")),
        )(*operands)

    def backward(row_arrays, params, cts):
        s = row_arrays[0].shape[0]
        ncol = 1 if col_tile is None else out_widths[0] // col_tile
        n_par = len(params)
        n_out = len(out_widths)
        d_rows = [r for r in range(rows) if diff[r]]
        d_pars = [q for q in range(n_par) if pdiff[q]]

        def body(*refs):
            i = pl.program_id(1)
            vals, k = load_inputs(refs)
            pvals = [refs[k + q][...].astype(F32) for q in range(n_par)]
            k += n_par
            ct_vals = [refs[k + o][...].astype(F32) for o in range(n_out)]
            k += n_out
            drow_refs = refs[k:k + len(d_rows)]
            dpar_refs = refs[k + len(d_rows):]

            def f(dv, dp):
                full_v = list(vals)
                for r, v in zip(d_rows, dv):
                    full_v[r] = v
                full_p = list(pvals)
                for q, v in zip(d_pars, dp):
                    full_p[q] = v
                return fn(full_v, full_p)

            _, vjp = jax.vjp(f, [vals[r] for r in d_rows], [pvals[q] for q in d_pars])
            g_rows, g_pars = vjp(ct_vals)
            for ref, r, g in zip(drow_refs, d_rows, g_rows):
                covered = sum(w for (_, w) in parts[r])
                if col_tile is None:
                    if covered != ref.shape[1]:
                        ref[...] = jnp.zeros_like(ref)
                    for (start, width), gp in zip(parts[r], g):
                        ref[:, start:start + width] = gp
                else:
                    ref[...] = g[0]
            for ref, q, g in zip(dpar_refs, d_pars, g_pars):
                n_sel = params[q].shape[0]
                first = (i == 0) if n_sel == 1 else ((i == 0) | (i == n_ctx_tiles))

                @pl.when(first)
                def _(ref=ref):
                    ref[...] = jnp.zeros_like(ref)

                ref[...] += g

        specs, operands = in_specs_for(row_arrays, params)
        for o, w in enumerate(out_widths):
            if col_tile is None:
                specs.append(pl.BlockSpec((tile, w), lambda j, i: (i, 0)))
            else:
                specs.append(pl.BlockSpec((tile, col_tile), lambda j, i: (i, j)))
            operands.append(cts[o])
        out_specs, out_shape = [], []
        for r in d_rows:
            w = row_arrays[r].shape[1]
            if col_tile is None:
                out_specs.append(pl.BlockSpec((tile, w), lambda j, i: (i, 0)))
            else:
                assert len(parts[r]) == 1 and parts[r][0] == (0, w)
                out_specs.append(pl.BlockSpec((tile, col_tile), lambda j, i: (i, j)))
            out_shape.append(jax.ShapeDtypeStruct((s, w), F32))
        for q in d_pars:
            n_sel, _, w = params[q].shape
            cw = w if col_tile is None else col_tile
            if col_tile is None:
                out_specs.append(pl.BlockSpec((None, 1, cw), lambda j, i, n_sel=n_sel: (sel_of(i, n_sel), 0, 0)))
            else:
                out_specs.append(pl.BlockSpec((None, 1, cw), lambda j, i, n_sel=n_sel: (sel_of(i, n_sel), 0, j)))
            out_shape.append(jax.ShapeDtypeStruct((n_sel, 1, w), F32))
        res = pl.pallas_call(
            body, name=name + "_bwd",
            grid=(ncol, s // tile),
            in_specs=specs, out_specs=out_specs, out_shape=out_shape,
            compiler_params=_cparams(("arbitrary", "arbitrary")),
        )(*operands)
        g_rows = [None] * rows
        for r, g in zip(d_rows, res[:len(d_rows)]):
            g_rows[r] = g
        g_pars = [None] * n_par
        for q, g in zip(d_pars, res[len(d_rows):]):
            g_pars[q] = g
        return g_rows, g_pars

    @jax.custom_vjp
    def op(row_arrays, params):
        return tuple(forward(list(row_arrays), list(params)))

    def op_fwd(row_arrays, params):
        return tuple(forward(list(row_arrays), list(params))), (row_arrays, params)

    def op_bwd(res, cts):
        row_arrays, params = res
        g_rows, g_pars = backward(list(row_arrays), list(params), list(cts))
        g_rows = tuple(jnp.zeros_like(a) if g is None else g for a, g in zip(row_arrays, g_rows))
        g_pars = tuple(jnp.zeros_like(a) if g is None else g for a, g in zip(params, g_pars))
        return g_rows, g_pars

    op.defvjp(op_fwd, op_bwd)
    return op


def _rot_impl(x, quarter):
    lane = lax.broadcasted_iota(jnp.int32, (x.shape[0], _LANES), 1)
    even = ((lane // quarter) % 2) == 0
    outs = []
    for k in range(x.shape[1] // _LANES):
        xs = x[:, k * _LANES:(k + 1) * _LANES]
        left = pltpu.roll(xs, _LANES - quarter, 1)
        right = pltpu.roll(xs, quarter, 1)
        outs.append(jnp.where(even, -left, right))
    return outs[0] if len(outs) == 1 else jnp.concatenate(outs, axis=1)


def _make_rot(quarter):
    @jax.custom_vjp
    def rot(x):
        return _rot_impl(x, quarter)

    rot.defvjp(lambda x: (_rot_impl(x, quarter), None), lambda _, g: (-_rot_impl(g, quarter),))
    return rot


_rot32 = _make_rot(32)
_rot16 = _make_rot(16)


def _tile_lanes(t, n):
    return t if n == 1 else jnp.concatenate([t] * n, axis=1)


def _rope(x, cos, sin, rot):
    n = x.shape[1] // _LANES
    return x * _tile_lanes(cos, n) + rot(x) * _tile_lanes(sin, n)


def _rms(x):
    return x * lax.rsqrt(jnp.mean(x * x, axis=-1, keepdims=True) + RMS_EPS)


def _ln(x):
    mu = jnp.mean(x, axis=-1, keepdims=True)
    xc = x - mu
    var = jnp.mean(xc * xc, axis=-1, keepdims=True)
    return xc * lax.rsqrt(var + LN_EPS)


def _sum_all(x):
    return jnp.sum(jnp.sum(x, axis=1, keepdims=True), axis=0, keepdims=True)


def _silu(x):
    return x * (1.0 / (1.0 + jnp.exp(-x)))


def _fn_modulate(vals, pars):
    (s,), = vals
    shift, scale = pars
    return [s * (1.0 + scale) + shift]


def _fn_postproj(vals, pars):
    (rq, rk, rv, rg, sq, sk, sv, mcq, mckv, mkr), (cos_h,), (sin_h,), (cos_m,), (sin_m,) = vals
    q_norm, kv_norm = pars
    k_scale = HEAD_DIM ** -0.5
    return [_rope(rq, cos_h, sin_h, _rot32), _rope(rk, cos_h, sin_h, _rot32) * k_scale, rv, rg,
            _rope(sq, cos_h, sin_h, _rot32) * (k_scale * _LOG2E), _rope(sk, cos_h, sin_h, _rot32), sv,
            _rms(mcq) * q_norm, _rms(mckv) * kv_norm, _rope(mkr, cos_m, sin_m, _rot16)]


def _fn_mla_assemble(vals, pars):
    (q_lin,), kn_v, (kr,), (cos_m,), (sin_m,) = vals
    kn, vv = kn_v[:MLA_HEADS], kn_v[MLA_HEADS]
    ones, zeros = jnp.ones_like(cos_m), jnp.zeros_like(sin_m)
    cos_q = jnp.concatenate([ones, cos_m] * MLA_HEADS, axis=1)
    sin_q = jnp.concatenate([zeros, sin_m] * MLA_HEADS, axis=1)
    q_full = (q_lin * cos_q + _rot16(q_lin) * sin_q) * (MLA_SCALE * _LOG2E)
    k_full = jnp.concatenate([t for h in range(MLA_HEADS) for t in (kn[h], kr)], axis=1)
    return [q_full, k_full, vv]


def _fn_mix(vals, pars):
    ret_f, ret_b, rg, (y_swa,), (y_mla,) = vals
    heads = [_silu(rg[h]) * _rms(ret_f[h] + ret_b[h]) for h in range(RET_HEADS)]
    return [jnp.concatenate(heads + [y_swa, y_mla], axis=1)]


def _fn_ln1(vals, pars):
    (s,), (mix,) = vals
    gate, g, b, shift_f, scale_f = pars
    x_a = _ln(ALPHA * s + (1.0 + gate) * mix) * g + b
    return [x_a, x_a * (1.0 + scale_f) + shift_f]


def _fn_ln2(vals, pars):
    (x_a,), (f,) = vals
    gate, g, b = pars
    return [_ln(ALPHA * x_a + (1.0 + gate) * f) * g + b]


def _conv_tiles(s, t, f):
    r = min(256, t)
    assert t % r == 0 and s % r == 0 and r % 8 == 0
    return r, _pick(f, 1408, _LANES)


def _conv_gate_fwd(u, g, par, t, name):
    s, f = u.shape
    r, cw = _conv_tiles(s, t, f)
    n_ctx, n_tiles, per = t // r, s // r, r // 8

    def body(u_ref, g_ref, gp_ref, gn_ref, p_ref, y_ref):
        i = pl.program_id(1)
        gv = g_ref[...]
        row = lax.broadcasted_iota(jnp.int32, (r, 1), 0)
        seg_start = (i == 0) | (i == n_ctx)
        seg_end = (i == n_ctx - 1) | (i == n_tiles - 1)
        prev_row = jnp.where(seg_start, 0.0, gp_ref[7:8, :])
        next_row = jnp.where(seg_end, 0.0, gn_ref[0:1, :])
        gp = jnp.where(row == 0, prev_row, pltpu.roll(gv, 1, 0))
        gn = jnp.where(row == r - 1, next_row, pltpu.roll(gv, r - 1, 0))
        gc = p_ref[0:1, :] * gp + p_ref[1:2, :] * gv + p_ref[2:3, :] * gn + p_ref[3:4, :]
        y_ref[...] = (_silu(gc) * u_ref[...]).astype(y_ref.dtype)

    tile = pl.BlockSpec((r, cw), lambda j, i: (i, j))
    return pl.pallas_call(
        body, name=name + "_fwd",
        grid=(f // cw, n_tiles),
        in_specs=[tile, tile,
                  pl.BlockSpec((8, cw), lambda j, i: (jnp.maximum(i * per - 1, 0), j)),
                  pl.BlockSpec((8, cw), lambda j, i: (jnp.minimum((i + 1) * per, s // 8 - 1), j)),
                  pl.BlockSpec((8, cw), lambda j, i: (0, j))],
        out_specs=tile,
        out_shape=jax.ShapeDtypeStruct((s, f), _MXU_DTYPE),
        compiler_params=_cparams(("arbitrary", "arbitrary")),
    )(u, g, g, g, par)


def _conv_gate_bwd(u, g, par, dy, t, name):
    s, f = u.shape
    r, cw = _conv_tiles(s, t, f)
    halo = 16
    assert r % halo == 0
    n_tiles, per = s // r, r // halo
    re = r + 2 * halo

    def body(u_ref, up_ref, un_ref, g_ref, gp_ref, gn_ref, dy_ref, dyp_ref, dyn_ref, p_ref,
             du_ref, dg_ref, dp_ref):
        i = pl.program_id(1)

        def ext(prev, cur, nxt):
            return jnp.concatenate([prev[...].astype(F32), cur[...].astype(F32), nxt[...].astype(F32)], axis=0)

        ge, ue, dye = ext(gp_ref, g_ref, gn_ref), ext(up_ref, u_ref, un_ref), ext(dyp_ref, dy_ref, dyn_ref)
        grow = i * r - halo + lax.broadcasted_iota(jnp.int32, (re, 1), 0)
        is_start = (grow == 0) | (grow == t)
        is_end = (grow == t - 1) | (grow == s - 1)
        inside = (grow >= 0) & (grow < s)
        w0, w1, w2, bias = p_ref[0:1, :], p_ref[1:2, :], p_ref[2:3, :], p_ref[3:4, :]
        gpe = jnp.where(is_start, 0.0, pltpu.roll(ge, 1, 0))
        gne = jnp.where(is_end, 0.0, pltpu.roll(ge, re - 1, 0))
        gce = w0 * gpe + w1 * ge + w2 * gne + bias
        sig = 1.0 / (1.0 + jnp.exp(-gce))
        dgce = jnp.where(inside, dye * ue * (sig * (1.0 + gce * (1.0 - sig))), 0.0)
        dge = (w1 * dgce + w0 * jnp.where(is_end, 0.0, pltpu.roll(dgce, re - 1, 0))
               + w2 * jnp.where(is_start, 0.0, pltpu.roll(dgce, 1, 0)))
        mid = slice(halo, r + halo)
        dg_ref[...] = dge[mid]
        du_ref[...] = (dye * gce * sig)[mid]
        dgc = dgce[mid]

        @pl.when(i == 0)
        def _():
            dp_ref[...] = jnp.zeros_like(dp_ref)

        dp_ref[0:1, :] += jnp.sum(dgc * gpe[mid], axis=0, keepdims=True)
        dp_ref[1:2, :] += jnp.sum(dgc * ge[mid], axis=0, keepdims=True)
        dp_ref[2:3, :] += jnp.sum(dgc * gne[mid], axis=0, keepdims=True)
        dp_ref[3:4, :] += jnp.sum(dgc, axis=0, keepdims=True)

    tile = pl.BlockSpec((r, cw), lambda j, i: (i, j))
    prev = pl.BlockSpec((halo, cw), lambda j, i: (jnp.maximum(i * per - 1, 0), j))
    nxt = pl.BlockSpec((halo, cw), lambda j, i: (jnp.minimum((i + 1) * per, s // halo - 1), j))
    par_spec = pl.BlockSpec((8, cw), lambda j, i: (0, j))
    return pl.pallas_call(
        body, name=name + "_bwd",
        grid=(f // cw, n_tiles),
        in_specs=[tile, prev, nxt, tile, prev, nxt, tile, prev, nxt, par_spec],
        out_specs=[tile, tile, par_spec],
        out_shape=[jax.ShapeDtypeStruct((s, f), F32), jax.ShapeDtypeStruct((s, f), F32),
                   jax.ShapeDtypeStruct((8, f), F32)],
        compiler_params=_cparams(("arbitrary", "arbitrary")),
    )(u, u, u, g, g, g, dy, dy, dy, par)


def _conv_gate(t, name):
    @jax.custom_vjp
    def op(u, g, par):
        return _conv_gate_fwd(u, g, par, t, name)

    def fwd(u, g, par):
        return _conv_gate_fwd(u, g, par, t, name), (u, g, par)

    def bwd(res, dy):
        u, g, par = res
        return _conv_gate_bwd(u, g, par, dy, t, name)

    op.defvjp(fwd, bwd)
    return op


class _AttnCfg:
    def __init__(self, name, s, t, heads, kv_heads, dk, dv, blk, band, scale, has_sink):
        self.name, self.s, self.t = name, s, t
        self.heads, self.kv_heads, self.group = heads, kv_heads, heads // kv_heads
        self.dk, self.dv, self.blk, self.band, self.scale, self.has_sink = dk, dv, blk, band, scale, has_sink
        self.nq = s // blk
        self.n_ctx = t // blk if band else 0
        self.ks = self.n_ctx + 3 if band else s // blk
        assert s % blk == 0 and (not band or (t % blk == 0 and blk >= SWA_WINDOW))

    def kblock(self, i, st):
        if not self.band:
            return st
        kb = jnp.clip(i + st - self.n_ctx - 1, self.n_ctx, self.nq - 1)
        return jnp.where(st < self.n_ctx, st, kb)

    def valid(self, i, st):
        if not self.band:
            return st >= 0
        kb = i + st - self.n_ctx - 1
        return (st < self.n_ctx) | ((i >= self.n_ctx) & (kb >= self.n_ctx) & (kb <= self.nq - 1))

    def masked(self, i, st):
        if self.band:
            return st >= self.n_ctx
        return i * self.blk < self.t

    def visible(self, i, kb, keys_first=False):
        b = self.blk
        qpos = i * b + lax.broadcasted_iota(jnp.int32, (b, b), 1 if keys_first else 0)
        kpos = kb * b + lax.broadcasted_iota(jnp.int32, (b, b), 0 if keys_first else 1)
        if self.band:
            return jnp.abs(qpos - kpos) <= SWA_WINDOW
        return (kpos < self.t) | (qpos >= self.t)


def _attn_fwd(cfg, q, k, vt, sink):
    b, dk, dv, g = cfg.blk, cfg.dk, cfg.dv, cfg.group

    def body(q_ref, k_ref, vt_ref, sink_ref, o_ref, lse_ref, m_sc, l_sc, acc_sc):
        i, st = pl.program_id(1), pl.program_id(2)

        @pl.when(st == 0)
        def _():
            if cfg.has_sink:
                for gg in range(g):
                    m_sc[gg] = jnp.broadcast_to(sink_ref[gg, 0:1, 0:1], (8, b))
                l_sc[...] = jnp.ones_like(l_sc)
            else:
                m_sc[...] = jnp.full_like(m_sc, NEG_INF)
                l_sc[...] = jnp.zeros_like(l_sc)
            acc_sc[...] = jnp.zeros_like(acc_sc)

        def step(use_mask):
            vis = cfg.visible(i, cfg.kblock(i, st), keys_first=True) if use_mask else None
            kv, vtv = k_ref[...], vt_ref[...]
            for gg in range(g):
                sc = lax.dot_general(kv, q_ref[:, gg * dk:(gg + 1) * dk], _NT,
                                     preferred_element_type=F32)
                if use_mask:
                    sc = jnp.where(vis, sc, NEG_INF)
                m_prev = m_sc[gg, 0:1, :]
                m_new = jnp.maximum(m_prev, jnp.max(sc, axis=0, keepdims=True))
                alpha = jnp.exp2(m_prev - m_new)
                p = jnp.exp2(sc - m_new)
                l_new = alpha * l_sc[gg, 0:1, :] + jnp.sum(p, axis=0, keepdims=True)
                acc_sc[gg] = acc_sc[gg] * alpha + jnp.dot(vtv, p.astype(_MXU_DTYPE), preferred_element_type=F32)
                m_sc[gg] = jnp.broadcast_to(m_new, (8, b))
                l_sc[gg] = jnp.broadcast_to(l_new, (8, b))

        ok = cfg.valid(i, st)
        msk = cfg.masked(i, st)
        pl.when(ok & msk)(lambda: step(True))
        pl.when(ok & jnp.logical_not(msk))(lambda: step(False))

        @pl.when(st == cfg.ks - 1)
        def _():
            for gg in range(g):
                o_ref[gg * dv:(gg + 1) * dv, :] = acc_sc[gg] / l_sc[gg, 0:1, :]
            lse_ref[...] = m_sc[...] + jnp.log2(l_sc[...])

    return pl.pallas_call(
        body, name=cfg.name + "_fwd",
        grid=(cfg.kv_heads, cfg.nq, cfg.ks),
        in_specs=[pl.BlockSpec((b, g * dk), lambda h, i, st: (i, h)),
                  pl.BlockSpec((b, dk), lambda h, i, st: (cfg.kblock(i, st), h)),
                  pl.BlockSpec((dv, b), lambda h, i, st: (h, cfg.kblock(i, st))),
                  pl.BlockSpec((g, 8, _LANES), lambda h, i, st: (h, 0, 0))],
        out_specs=[pl.BlockSpec((g * dv, b), lambda h, i, st: (h, i)),
                   pl.BlockSpec((g, 8, b), lambda h, i, st: (h, 0, i))],
        out_shape=[jax.ShapeDtypeStruct((cfg.heads * dv, cfg.s), F32),
                   jax.ShapeDtypeStruct((cfg.heads, 8, cfg.s), F32)],
        scratch_shapes=[pltpu.VMEM((g, 8, b), F32), pltpu.VMEM((g, 8, b), F32), pltpu.VMEM((g, dv, b), F32)],
        compiler_params=_cparams(("parallel", "parallel", "arbitrary")),
    )(q, k, vt, sink)


def _attn_bwd(cfg, q, k, v, sink, o, lse, do):
    b, dk, dv, g = cfg.blk, cfg.dk, cfg.dv, cfg.group

    def body(q_ref, k_ref, v_ref, sink_ref, o_ref, lse_ref, do_ref, dq_ref, dk_ref, dv_ref, dsink_ref,
             dq_sc, delta_sc):
        i, st = pl.program_id(1), pl.program_id(2)

        @pl.when((i == 0) & (st == 0))
        def _():
            dk_ref[...] = jnp.zeros_like(dk_ref)
            dv_ref[...] = jnp.zeros_like(dv_ref)

        @pl.when(st == 0)
        def _():
            dq_sc[...] = jnp.zeros_like(dq_sc)
            for gg in range(g):
                vs = slice(gg * dv, (gg + 1) * dv)
                delta = jnp.sum(do_ref[:, vs] * o_ref[:, vs], axis=1, keepdims=True)
                delta_sc[gg] = jnp.broadcast_to(delta, (b, _LANES))
                if cfg.has_sink:
                    ps = jnp.exp2(sink_ref[gg, 0:1, :] - lse_ref[gg]) * delta_sc[gg]
                    dsink_ref[gg] = jnp.broadcast_to(-jnp.sum(ps, axis=0, keepdims=True), (8, _LANES))
                else:
                    dsink_ref[gg] = jnp.zeros((8, _LANES), F32)

        def step(use_mask):
            kb = cfg.kblock(i, st)
            vis = cfg.visible(i, kb) if use_mask else None
            kbv = k_ref[...]
            vbv = v_ref[...]
            rows = pl.ds(pl.multiple_of(kb * b, b), b)
            dk_acc = None
            dv_acc = None
            for gg in range(g):
                ks, vs = slice(gg * dk, (gg + 1) * dk), slice(gg * dv, (gg + 1) * dv)
                qb = q_ref[:, ks]
                dob = do_ref[:, vs].astype(_MXU_DTYPE)
                sc = lax.dot_general(qb, kbv, _NT, preferred_element_type=F32)
                if use_mask:
                    sc = jnp.where(vis, sc, NEG_INF)
                p = jnp.exp2(sc - lse_ref[gg, :, 0:1])
                dp = lax.dot_general(dob, vbv, _NT, preferred_element_type=F32)
                ds = p * (dp - delta_sc[gg, :, 0:1])
                dq_sc[:, ks] += jnp.dot(ds.astype(_MXU_DTYPE), kbv, preferred_element_type=F32)
                dk_h = jnp.dot(ds.T.astype(_MXU_DTYPE), qb, preferred_element_type=F32)
                dv_h = jnp.dot(p.T.astype(_MXU_DTYPE), dob, preferred_element_type=F32)
                dk_acc = dk_h if dk_acc is None else dk_acc + dk_h
                dv_acc = dv_h if dv_acc is None else dv_acc + dv_h
            dk_ref[rows, :] += dk_acc
            dv_ref[rows, :] += dv_acc

        ok = cfg.valid(i, st)
        msk = cfg.masked(i, st)
        pl.when(ok & msk)(lambda: step(True))
        pl.when(ok & jnp.logical_not(msk))(lambda: step(False))

        @pl.when(st == cfg.ks - 1)
        def _():
            dq_ref[...] = (dq_sc[...] * (1.0 / _LOG2E)).astype(dq_ref.dtype)

    return pl.pallas_call(
        body, name=cfg.name + "_bwd",
        grid=(cfg.kv_heads, cfg.nq, cfg.ks),
        in_specs=[pl.BlockSpec((b, g * dk), lambda h, i, st: (i, h)),
                  pl.BlockSpec((b, dk), lambda h, i, st: (cfg.kblock(i, st), h)),
                  pl.BlockSpec((b, dv), lambda h, i, st: (cfg.kblock(i, st), h)),
                  pl.BlockSpec((g, 8, _LANES), lambda h, i, st: (h, 0, 0)),
                  pl.BlockSpec((b, g * dv), lambda h, i, st: (i, h)),
                  pl.BlockSpec((g, b, _LANES), lambda h, i, st: (h, i, 0)),
                  pl.BlockSpec((b, g * dv), lambda h, i, st: (i, h))],
        out_specs=[pl.BlockSpec((b, g * dk), lambda h, i, st: (i, h)),
                   pl.BlockSpec((cfg.s, dk), lambda h, i, st: (0, h)),
                   pl.BlockSpec((cfg.s, dv), lambda h, i, st: (0, h)),
                   pl.BlockSpec((g, None, 8, _LANES), lambda h, i, st: (h, i, 0, 0))],
        out_shape=[jax.ShapeDtypeStruct((cfg.s, cfg.heads * dk), q.dtype),
                   jax.ShapeDtypeStruct((cfg.s, cfg.kv_heads * dk), F32),
                   jax.ShapeDtypeStruct((cfg.s, cfg.kv_heads * dv), F32),
                   jax.ShapeDtypeStruct((cfg.heads, cfg.nq, 8, _LANES), F32)],
        scratch_shapes=[pltpu.VMEM((b, g * dk), F32), pltpu.VMEM((g, b, _LANES), F32)],
        compiler_params=_cparams(("parallel", "arbitrary", "arbitrary")),
    )(q, k, v, sink, o, lse, do)


def _attention(cfg):
    def run(q, k, v, sink):
        sr = jnp.broadcast_to((sink.astype(F32) * _LOG2E)[:, None, None], (cfg.heads, 8, _LANES))
        ot, lse = _attn_fwd(cfg, q, k, v.T, sr)
        return ot.T, (sr, lse)

    @jax.custom_vjp
    def op(q, k, v, sink):
        return run(q, k, v, sink)[0]

    def fwd(q, k, v, sink):
        o, (sr, lse) = run(q, k, v, sink)
        return o, (q, k, v, sr, o, lse)

    def bwd(res, do):
        q, k, v, sr, o, lse = res
        lse_rows = jnp.broadcast_to(lse[:, 0, :, None], (cfg.heads, cfg.s, _LANES))
        dq, dk, dv, dsink = _attn_bwd(cfg, q, k, v, sr, o, lse_rows, do)
        return dq, (dk * (1.0 / _LOG2E)).astype(k.dtype), dv.astype(v.dtype), dsink[:, :, 0, 0].sum(1)

    op.defvjp(fwd, bwd)
    return op


def _ret_chunk_order(n, d, n_ctx, n_all):
    fwd = n
    bwd = jnp.where(n < n_ctx, n_ctx - 1 - n, n_all - 1 - (n - n_ctx))
    return jnp.where(d == 0, fwd, bwd)


def _ret_decays(lg, d):
    c = RET_CHUNK
    i = lax.broadcasted_iota(jnp.int32, (c, c), 0)
    j = lax.broadcasted_iota(jnp.int32, (c, c), 1)
    sign = (1 - 2 * d)
    diff = ((i - j) * sign).astype(F32)
    intra = jnp.where(diff >= 0, jnp.exp(lg * jnp.maximum(diff, 0.0)), 0.0)
    pos = lax.broadcasted_iota(jnp.int32, (c, 1), 0)
    r = (pos + d * (c - 1 - 2 * pos)).astype(F32)
    qd = jnp.exp(lg * (r + 1.0))
    kd = jnp.exp(lg * (c - 1.0 - r))
    cd = jnp.exp(lg * c)
    return intra, qd, kd, cd, diff, r


def _mxu_dot(a, b, dims=None):
    a, b = a.astype(_MXU_DTYPE), b.astype(_MXU_DTYPE)
    if dims is None:
        return jnp.dot(a, b, preferred_element_type=F32)
    return lax.dot_general(a, b, dims, preferred_element_type=F32)


_NT = (((1,), (1,)), ((), ()))


def _ret_fwd(q, k, v, lg, n_ctx, name):
    s = q.shape[0]
    c, hd = RET_CHUNK, HEAD_DIM
    n_all = s // c

    def body(lg_ref, q_ref, k_ref, v_ref, o_ref, st_ref, state_sc):
        d, n = pl.program_id(0), pl.program_id(1)

        @pl.when(n == 0)
        def _():
            state_sc[...] = jnp.zeros_like(state_sc)

        for h in range(RET_HEADS):
            cols = slice(h * hd, (h + 1) * hd)
            intra, qd, kd, cd, _, _ = _ret_decays(lg_ref[d, h], d)
            qv, kv, vv = q_ref[:, cols], k_ref[:, cols], v_ref[:, cols]
            s_in = state_sc[h]
            st_ref[h] = s_in
            p = _mxu_dot(qv, kv, _NT) * intra
            o_ref[:, cols] = _mxu_dot(p, vv) + _mxu_dot(qv * qd, s_in)
            state_sc[h] = cd * s_in + _mxu_dot((kv * kd).T, vv)

    def chunk_spec():
        return pl.BlockSpec((c, RET_DIM), lambda d, n: (_ret_chunk_order(n, d, n_ctx, n_all), 0))

    return pl.pallas_call(
        body, name=name + "_fwd",
        grid=(2, n_all),
        in_specs=[pl.BlockSpec(memory_space=pltpu.SMEM), chunk_spec(), chunk_spec(), chunk_spec()],
        out_specs=[pl.BlockSpec((None, c, RET_DIM), lambda d, n: (d, _ret_chunk_order(n, d, n_ctx, n_all), 0)),
                   pl.BlockSpec((None, RET_HEADS, None, hd, hd), lambda d, n: (d, 0, n, 0, 0))],
        out_shape=[jax.ShapeDtypeStruct((2, s, RET_DIM), F32),
                   jax.ShapeDtypeStruct((2, RET_HEADS, n_all, hd, hd), F32)],
        scratch_shapes=[pltpu.VMEM((RET_HEADS, hd, hd), F32)],
        compiler_params=_cparams(("arbitrary", "arbitrary")),
    )(lg, q, k, v)


def _ret_bwd(q, k, v, lg, states, dout, n_ctx, name):
    s = q.shape[0]
    c, hd = RET_CHUNK, HEAD_DIM
    n_all = s // c

    def body(lg_ref, q_ref, k_ref, v_ref, st_ref, do_ref, dq_ref, dk_ref, dv_ref, dlg_ref, ds_sc):
        d, n = pl.program_id(0), pl.program_id(1)

        @pl.when(n == 0)
        def _():
            ds_sc[...] = jnp.zeros_like(ds_sc)
            dlg_ref[...] = jnp.zeros_like(dlg_ref)

        for h in range(RET_HEADS):
            cols = slice(h * hd, (h + 1) * hd)
            intra, qd, kd, cd, diff, r = _ret_decays(lg_ref[d, h], d)
            qv, kv, vv, do = q_ref[:, cols], k_ref[:, cols], v_ref[:, cols], do_ref[:, cols]
            s_in = st_ref[h]
            ds_out = ds_sc[h]
            sc = _mxu_dot(qv, kv, _NT)
            p = sc * intra
            dp = _mxu_dot(do, vv, _NT)
            dsc = dp * intra
            dqs = _mxu_dot(do, s_in, _NT)
            dkk = _mxu_dot(vv, ds_out, _NT)
            dq_ref[:, cols] = _mxu_dot(dsc, kv) + dqs * qd
            dk_ref[:, cols] = _mxu_dot(dsc.T, qv) + dkk * kd
            dv_ref[:, cols] = _mxu_dot(p.T, do) + _mxu_dot(kv * kd, ds_out)
            ds_sc[h] = cd * ds_out + _mxu_dot((qv * qd).T, do)
            dlg = (_sum_all(dp * p * diff)
                   + _sum_all(jnp.sum(dqs * qv, axis=1, keepdims=True) * qd * (r + 1.0))
                   + _sum_all(jnp.sum(dkk * kv, axis=1, keepdims=True) * kd * (c - 1.0 - r))
                   + _sum_all(ds_out * s_in) * (cd * c))
            dlg_ref[h] += jnp.broadcast_to(dlg, (8, _LANES))

    def order(n, d):
        return _ret_chunk_order(n_all - 1 - n, d, n_ctx, n_all)

    def chunk_spec():
        return pl.BlockSpec((c, RET_DIM), lambda d, n: (order(n, d), 0))

    def dir_spec():
        return pl.BlockSpec((None, c, RET_DIM), lambda d, n: (d, order(n, d), 0))

    return pl.pallas_call(
        body, name=name + "_bwd",
        grid=(2, n_all),
        in_specs=[pl.BlockSpec(memory_space=pltpu.SMEM), chunk_spec(), chunk_spec(), chunk_spec(),
                  pl.BlockSpec((None, RET_HEADS, None, hd, hd), lambda d, n: (d, 0, n_all - 1 - n, 0, 0)),
                  dir_spec()],
        out_specs=[dir_spec(), dir_spec(), dir_spec(),
                   pl.BlockSpec((None, RET_HEADS, 8, _LANES), lambda d, n: (d, 0, 0, 0))],
        out_shape=[jax.ShapeDtypeStruct((2, s, RET_DIM), F32)] * 3
        + [jax.ShapeDtypeStruct((2, RET_HEADS, 8, _LANES), F32)],
        scratch_shapes=[pltpu.VMEM((RET_HEADS, hd, hd), F32)],
        compiler_params=_cparams(("arbitrary", "arbitrary")),
    )(lg, q, k, v, states, dout)


def _retention(n_ctx, name):
    @jax.custom_vjp
    def op(q, k, v, lg):
        return _ret_fwd(q, k, v, lg, n_ctx, name)[0]

    def fwd(q, k, v, lg):
        out, states = _ret_fwd(q, k, v, lg, n_ctx, name)
        return out, (q, k, v, lg, states)

    def bwd(res, dout):
        q, k, v, lg, states = res
        dq, dk, dv, dlg = _ret_bwd(q, k, v, lg, states, dout, n_ctx, name)
        return dq[0] + dq[1], dk[0] + dk[1], dv[0] + dv[1], dlg[:, :, 0, 0]

    op.defvjp(fwd, bwd)
    return op


def _loss_call(y, target, n_ctx_tiles, tile, name):
    s, dm = y.shape

    def body(y_ref, t_ref, loss_ref, dy_ref):
        i = pl.program_id(0)

        @pl.when(i == 0)
        def _():
            loss_ref[...] = jnp.zeros_like(loss_ref)

        @pl.when(i < n_ctx_tiles)
        def _():
            dy_ref[...] = jnp.zeros_like(dy_ref)

        @pl.when(i >= n_ctx_tiles)
        def _():
            err = y_ref[...] - t_ref[...]
            dy_ref[...] = err * (1.0 / dm)
            loss_ref[...] += jnp.broadcast_to(_sum_all(err * err) * (0.5 / dm), loss_ref.shape)

    return pl.pallas_call(
        body, name=name,
        grid=(s // tile,),
        in_specs=[pl.BlockSpec((tile, dm), lambda i: (i, 0)),
                  pl.BlockSpec((tile, dm), lambda i: (jnp.maximum(i - n_ctx_tiles, 0), 0))],
        out_specs=[pl.BlockSpec((8, _LANES), lambda i: (0, 0)),
                   pl.BlockSpec((tile, dm), lambda i: (i, 0))],
        out_shape=[jax.ShapeDtypeStruct((8, _LANES), F32), jax.ShapeDtypeStruct((s, dm), F32)],
        compiler_params=_cparams(("arbitrary",)),
    )(y, target)


def _loss_op(n_ctx_tiles, tile):
    @jax.custom_vjp
    def op(y, target):
        return _loss_call(y, target, n_ctx_tiles, tile, "loss_head")[0][0, 0]

    def fwd(y, target):
        loss, dy = _loss_call(y, target, n_ctx_tiles, tile, "loss_head")
        return loss[0, 0], (dy, target)

    def bwd(res, g):
        dy, target = res
        return dy * g, jnp.zeros_like(target)

    op.defvjp(fwd, bwd)
    return op


def _exchange(x, gather, name):
    blk_shape = x.shape if gather else x.shape[1:]

    def body(x_ref, o_ref, send_sems, recv_sems, local_sem):
        mx, my, mc = lax.axis_index("x"), lax.axis_index("y"), lax.axis_index("c")
        me = 4 * mx + 2 * my + mc
        copies = []
        for rel in range(1, _N_DEV):
            px = mx ^ ((rel >> 2) & 1)
            py = my ^ ((rel >> 1) & 1)
            pc = mc ^ (rel & 1)
            src = x_ref if gather else x_ref.at[4 * px + 2 * py + pc]
            cp = pltpu.make_async_remote_copy(
                src_ref=src, dst_ref=o_ref.at[me],
                send_sem=send_sems.at[rel - 1], recv_sem=recv_sems.at[rel - 1],
                device_id=(px, py, pc), device_id_type=pl.DeviceIdType.MESH)
            cp.start()
            copies.append(cp)
        mine = pltpu.make_async_copy(x_ref if gather else x_ref.at[me], o_ref.at[me], local_sem)
        mine.start()
        for cp in copies:
            cp.wait()
        mine.wait()

    return pl.pallas_call(
        body, name=name,
        in_specs=[pl.BlockSpec(memory_space=pltpu.HBM)],
        out_specs=pl.BlockSpec(memory_space=pltpu.HBM),
        out_shape=jax.ShapeDtypeStruct((_N_DEV,) + tuple(blk_shape), x.dtype),
        scratch_shapes=[pltpu.SemaphoreType.DMA((_N_DEV - 1,)), pltpu.SemaphoreType.DMA((_N_DEV - 1,)),
                        pltpu.SemaphoreType.DMA(())],
    )(x)


def _sum_parts(parts, name, out_dtype=F32):
    n, r, w = parts.shape
    tile = _pick(r, max(8, (1 << 19) // (w * n) // 16 * 16), 16)

    def body(p_ref, o_ref):
        acc = p_ref[0].astype(F32)
        for j in range(1, n):
            acc = acc + p_ref[j].astype(F32)
        o_ref[...] = acc.astype(o_ref.dtype)

    return pl.pallas_call(
        body, name=name, grid=(r // tile,),
        in_specs=[pl.BlockSpec((n, tile, w), lambda i: (0, i, 0))],
        out_specs=pl.BlockSpec((tile, w), lambda i: (i, 0)),
        out_shape=jax.ShapeDtypeStruct((r, w), out_dtype),
        compiler_params=_cparams(("arbitrary",)),
    )(parts)


def _chip_peers(mx, my):
    return [(1 - mx, my), (mx, 1 - my), (1 - mx, 1 - my)]


def _gather_two_level(xs, name):
    n_ops = len(xs)

    def body(*refs):
        x_refs, o_refs = refs[:n_ops], refs[n_ops:2 * n_ops]
        send_sems, recv_sems, local_sems = refs[2 * n_ops:]
        mx, my, mc = lax.axis_index("x"), lax.axis_index("y"), lax.axis_index("c")
        sibling = (mx, my, 1 - mc)
        chips = _chip_peers(mx, my)

        def copy(op, k, block, to, src=None):
            idx = 4 * block[0] + 2 * block[1] + block[2]
            dst = o_refs[op].at[idx]
            return pltpu.make_async_remote_copy(
                src_ref=dst if src is None else src, dst_ref=dst,
                send_sem=send_sems.at[op * 7 + k], recv_sem=recv_sems.at[op * 7 + k],
                device_id=to, device_id_type=pl.DeviceIdType.MESH)

        me = (mx, my, mc)
        mine, first, passed = [], [], []
        for op in range(n_ops):
            cp = pltpu.make_async_copy(x_refs[op], o_refs[op].at[4 * mx + 2 * my + mc], local_sems.at[op])
            cp.start()
            mine.append(cp)
            first.append(copy(op, 0, me, sibling, src=x_refs[op]))
            for j, chip in enumerate(chips):
                first.append(copy(op, 1 + j, me, (*chip, mc), src=x_refs[op]))
        for cp in first:
            cp.start()
        for j, chip in enumerate(chips):
            for op in range(n_ops):
                copy(op, 1 + j, (*chip, mc), me).wait_recv()
                cp = copy(op, 4 + j, (*chip, mc), sibling)
                cp.start()
                passed.append(cp)
        for op in range(n_ops):
            copy(op, 0, sibling, me).wait_recv()
            for j, chip in enumerate(chips):
                copy(op, 4 + j, (*chip, 1 - mc), me).wait_recv()
        for cp in first + passed:
            cp.wait_send()
        for cp in mine:
            cp.wait()

    hbm = pl.BlockSpec(memory_space=pltpu.HBM)
    return pl.pallas_call(
        body, name=name,
        in_specs=[hbm] * n_ops, out_specs=[hbm] * n_ops,
        out_shape=[jax.ShapeDtypeStruct((_N_DEV,) + tuple(x.shape), x.dtype) for x in xs],
        scratch_shapes=[pltpu.SemaphoreType.DMA((7 * n_ops,)), pltpu.SemaphoreType.DMA((7 * n_ops,)),
                        pltpu.SemaphoreType.DMA((n_ops,))],
    )(*xs)


def _swap_with_sibling(xs, name):
    n_ops = len(xs)

    def body(*refs):
        x_refs, o_refs = refs[:n_ops], refs[n_ops:2 * n_ops]
        send_sems, recv_sems = refs[2 * n_ops:]
        mx, my, mc = lax.axis_index("x"), lax.axis_index("y"), lax.axis_index("c")
        copies = []
        for op in range(n_ops):
            for q in range(4):
                rem = pltpu.make_async_remote_copy(
                    src_ref=x_refs[op].at[q], dst_ref=o_refs[op].at[q],
                    send_sem=send_sems.at[op * 4 + q], recv_sem=recv_sems.at[op * 4 + q],
                    device_id=(mx, my, 1 - mc), device_id_type=pl.DeviceIdType.MESH)
                rem.start()
                copies.append(rem)
        for cp in copies:
            cp.wait()

    hbm = pl.BlockSpec(memory_space=pltpu.HBM)
    return pl.pallas_call(
        body, name=name,
        in_specs=[hbm] * n_ops, out_specs=[hbm] * n_ops,
        out_shape=[jax.ShapeDtypeStruct(x.shape, x.dtype) for x in xs],
        scratch_shapes=[pltpu.SemaphoreType.DMA((4 * n_ops,)), pltpu.SemaphoreType.DMA((4 * n_ops,))],
    )(*xs)


def _add2(a, b, name, out_dtype):
    shape = a.shape
    a2, b2 = a.reshape(-1, shape[-1]), b.reshape(-1, shape[-1])
    r, w = a2.shape
    tile = _pick(r, max(16, (1 << 19) // w // 16 * 16), 16)

    def body(a_ref, b_ref, o_ref):
        o_ref[...] = (a_ref[...].astype(F32) + b_ref[...].astype(F32)).astype(o_ref.dtype)

    spec = pl.BlockSpec((tile, w), lambda i: (i, 0))
    return pl.pallas_call(
        body, name=name, grid=(r // tile,), in_specs=[spec, spec], out_specs=spec,
        out_shape=jax.ShapeDtypeStruct((r, w), out_dtype),
        compiler_params=_cparams(("arbitrary",)),
    )(a2, b2).reshape(shape)


def _scatter_chips(xs, name):
    n_ops = len(xs)

    def body(*refs):
        x_refs, o_refs = refs[:n_ops], refs[n_ops:2 * n_ops]
        send_sems, recv_sems, local_sems = refs[2 * n_ops:]
        mx, my, mc = lax.axis_index("x"), lax.axis_index("y"), lax.axis_index("c")
        my_chip = 2 * mx + my
        copies = []
        for op in range(n_ops):
            loc = pltpu.make_async_copy(x_refs[op].at[my_chip], o_refs[op].at[my_chip], local_sems.at[op])
            loc.start()
            copies.append(loc)
            for j, (px, py) in enumerate(_chip_peers(mx, my)):
                rem = pltpu.make_async_remote_copy(
                    src_ref=x_refs[op].at[2 * px + py], dst_ref=o_refs[op].at[my_chip],
                    send_sem=send_sems.at[op * 3 + j], recv_sem=recv_sems.at[op * 3 + j],
                    device_id=(px, py, mc), device_id_type=pl.DeviceIdType.MESH)
                rem.start()
                copies.append(rem)
        for cp in copies:
            cp.wait()

    hbm = pl.BlockSpec(memory_space=pltpu.HBM)
    return pl.pallas_call(
        body, name=name,
        in_specs=[hbm] * n_ops, out_specs=[hbm] * n_ops,
        out_shape=[jax.ShapeDtypeStruct(x.shape, x.dtype) for x in xs],
        scratch_shapes=[pltpu.SemaphoreType.DMA((3 * n_ops,)), pltpu.SemaphoreType.DMA((3 * n_ops,)),
                        pltpu.SemaphoreType.DMA((n_ops,))],
    )(*xs)


def _weights_gather_op(name):
    def impl(shards):
        got = _gather_two_level([s.astype(_MXU_DTYPE) for s in shards], name + "_gather")
        return tuple(g.astype(F32) for g in got)

    @jax.custom_vjp
    def op(shards):
        return impl(shards)

    def fwd(shards):
        return impl(shards), None

    def bwd(_, cts):
        mc = lax.axis_index("c")
        by_core = [jnp.swapaxes(g.astype(_MXU_DTYPE).reshape((4, 2) + g.shape[1:]), 0, 1) for g in cts]
        mine = [lax.dynamic_index_in_dim(p, mc, 0, keepdims=False) for p in by_core]
        theirs = [lax.dynamic_index_in_dim(p, 1 - mc, 0, keepdims=False) for p in by_core]
        got = _swap_with_sibling(theirs, name + "_scatter_pair")
        chip_sums = [_add2(a, b, "%s_pair_sum%d" % (name, k), _MXU_DTYPE) for k, (a, b) in enumerate(zip(mine, got))]
        crossed = _scatter_chips(chip_sums, name + "_scatter_chips")
        out = []
        for k, q in enumerate(crossed):
            flat = q.reshape(4, -1, q.shape[-1])
            out.append(_sum_parts(flat, "%s_chip_sum%d" % (name, k), F32).reshape(q.shape[1:]))
        return (tuple(out),)

    op.defvjp(fwd, bwd)
    return op


def _all_gather_op(name, payload_dtype):
    def impl(x):
        return _exchange(x.astype(payload_dtype), True, name + "_gather").astype(F32)

    @jax.custom_vjp
    def op(x):
        return impl(x)

    def fwd(x):
        return impl(x), None

    def bwd(_, g):
        return (_sum_parts(_exchange(g, False, name + "_scatter"), name + "_sum"),)

    op.defvjp(fwd, bwd)
    return op


def _adamw(w, g, m, v, partial, name):
    r, wd = w.shape
    tile = _pick(r, max(8, (1 << 20) // (4 * wd) // 8 * 8), 8)
    c1 = 1.0 / (1.0 - ADAM_B1 ** ADAM_STEP)
    c2 = 1.0 / (1.0 - ADAM_B2 ** ADAM_STEP)

    def body(w_ref, g_ref, m_ref, v_ref, go_ref, d_ref, mo_ref, vo_ref):
        if partial:
            g = g_ref[0]
            for j in range(1, _N_DEV):
                g = g + g_ref[j]
        else:
            g = g_ref[...]
        m_new = ADAM_B1 * m_ref[...] + (1.0 - ADAM_B1) * g
        v_new = ADAM_B2 * v_ref[...] + (1.0 - ADAM_B2) * (g * g)
        m_hat = m_new * c1
        v_hat = v_new * c2
        go_ref[...] = g
        d_ref[...] = -ADAM_LR * (m_hat / (jnp.sqrt(v_hat) + ADAM_EPS) + ADAM_WD * w_ref[...])
        mo_ref[...] = m_new
        vo_ref[...] = v_new

    spec = pl.BlockSpec((tile, wd), lambda i: (i, 0))
    g_spec = pl.BlockSpec((_N_DEV, tile, wd), lambda i: (0, i, 0)) if partial else spec
    return pl.pallas_call(
        body, name=name, grid=(r // tile,),
        in_specs=[spec, g_spec, spec, spec],
        out_specs=[spec] * 4,
        out_shape=[jax.ShapeDtypeStruct((r, wd), F32)] * 4,
        compiler_params=_cparams(("arbitrary",)),
    )(w, g, m, v)


def _pack(arrays):
    flat, meta, off = [], [], 0
    for a in arrays:
        n = int(np.prod(a.shape))
        pad = (-n) % _LANES
        flat.append(a.reshape(-1))
        if pad:
            flat.append(jnp.zeros((pad,), a.dtype))
        meta.append((off, a.shape))
        off += n + pad
    pad = (-off) % (8 * _LANES)
    if pad:
        flat.append(jnp.zeros((pad,), arrays[0].dtype))
    return jnp.concatenate(flat).reshape(-1, _LANES), meta


def _unpack(packed, meta):
    flat = packed.reshape(-1)
    return [flat[off:off + int(np.prod(shape))].reshape(shape) for off, shape in meta]


def _rope_tables(t, l, dim, width):
    rows = l // GRID_W
    r = np.repeat(np.arange(rows, dtype=np.float32), GRID_W)
    cc = np.tile(np.arange(GRID_W, dtype=np.float32), rows)
    n_freq = dim // 4
    inv = jnp.asarray(ROPE_THETA, F32) ** (-jnp.arange(n_freq, dtype=F32) / n_freq)
    ang_r = jnp.asarray(r)[:, None] * inv
    ang_c = jnp.asarray(cc)[:, None] * inv
    ang = jnp.concatenate([ang_r, ang_r, ang_c, ang_c], axis=-1)
    cos, sin = jnp.cos(ang), jnp.sin(ang)
    if width > dim:
        cos = jnp.concatenate([cos, jnp.ones((l, width - dim), F32)], axis=1)
        sin = jnp.concatenate([sin, jnp.zeros((l, width - dim), F32)], axis=1)
    cos = jnp.concatenate([jnp.ones((t, width), F32), cos], axis=0)
    sin = jnp.concatenate([jnp.zeros((t, width), F32), sin], axis=0)
    return cos, sin


def _full_weights(gathered):
    full = {}
    for n, g in zip(GATHERED, gathered):
        if GATHER_AXIS[n] == 0:
            full[n] = g.reshape(g.shape[0] * g.shape[1], g.shape[2])
        else:
            full[n] = jnp.transpose(g, (1, 0, 2)).reshape(g.shape[1], g.shape[0] * g.shape[2])
    return full


def _layer(l, stream, mod, wts, small, tables, dims, last):
    s, t, dm, dff = dims
    tile = min(256, t)
    n_ctx_tiles = t // tile
    tile_w = min(128, t)
    n_ctx_tiles_w = t // tile_w
    cos_h, sin_h, cos_m, sin_m = tables
    tag = "l%d_" % l

    def mrow(k):
        return mod[:, k:k + 1, :]

    def one(a):
        return a.reshape(1, 1, -1)

    w_in = jnp.concatenate([wts['w_in'], jnp.zeros((dm, IN_WIDTH_PAD - IN_WIDTH), F32)], axis=1)
    modulate = _rowwise(_fn_modulate, tag + "modulate", 1, [[(0, dm)]], [True], [True, True], [dm],
                        tile, n_ctx_tiles, mxu_outs=(0,))
    (h,) = modulate((stream,), (mrow(0), mrow(1)))
    proj = _linear(tag + "w_in")(h, w_in)

    in_parts = [(0, 512), (512, 512), (1024, 512), (1536, 512), (2048, 768), (2816, 256), (3072, 256),
                (3328, 512), (3840, 256), (4096, 128)]
    postproj = _rowwise(_fn_postproj, tag + "postproj", 5,
                        [in_parts, [(0, 128)], [(0, 128)], [(0, 128)], [(0, 128)]],
                        [True, False, False, False, False], [True, True],
                        [512, 512, 512, 512, 768, 256, 256, 512, 256, 128], tile_w, n_ctx_tiles_w,
                        mxu_outs=(4, 5, 6, 7, 8))
    (rq, rk, rv, rg, sq, sk, sv, cqn, ckvn, kr) = postproj(
        (proj, cos_h, sin_h, cos_m, sin_m), (one(small['mla_q_norm'][l]), one(small['mla_kv_norm'][l])))

    lg = jnp.stack([jax.nn.log_sigmoid(small['ret_decay_fwd'][l]), jax.nn.log_sigmoid(small['ret_decay_bwd'][l])])
    ret = _retention(t // RET_CHUNK, tag + "retention")(rq, rk, rv, lg)

    swa_blk = 256 if (t % 256 == 0 and s % 256 == 0) else 128
    swa_cfg = _AttnCfg(tag + "swa", s, t, SWA_HEADS, SWA_KV_HEADS, HEAD_DIM, HEAD_DIM, swa_blk, True,
                       HEAD_DIM ** -0.5, True)
    y_swa = _attention(swa_cfg)(sq, sk, sv, small['swa_sink'][l])

    w_uq = wts['mla_w_uq'].reshape(MLA_Q_RANK, MLA_HEADS, MLA_NOPE + MLA_ROPE)
    w_uq = jnp.concatenate([w_uq, jnp.zeros((MLA_Q_RANK, MLA_HEADS, 256 - MLA_NOPE - MLA_ROPE), F32)], axis=2)
    w_uq = w_uq.reshape(MLA_Q_RANK, MLA_HEADS * 256)
    w_ukv = wts['mla_w_ukv'].reshape(MLA_KV_RANK, MLA_HEADS, MLA_NOPE + MLA_V)
    w_ukv = jnp.concatenate([w_ukv[:, :, :MLA_NOPE].reshape(MLA_KV_RANK, -1),
                             w_ukv[:, :, MLA_NOPE:].reshape(MLA_KV_RANK, -1)], axis=1)
    q_lin = _linear(tag + "w_uq")(cqn, w_uq)
    kv_lin = _linear(tag + "w_ukv")(ckvn, w_ukv)
    kv_parts = [(hh * 128, 128) for hh in range(MLA_HEADS)] + [(MLA_HEADS * 128, MLA_HEADS * 128)]
    assemble = _rowwise(_fn_mla_assemble, tag + "mla_assemble", 5,
                        [[(0, MLA_HEADS * 256)], kv_parts, [(0, 128)], [(0, 128)], [(0, 128)]],
                        [True, True, True, False, False], [],
                        [MLA_HEADS * 256, MLA_HEADS * 256, MLA_HEADS * 128], tile_w, n_ctx_tiles_w,
                        mxu_outs=(0, 1, 2))
    (q_full, k_full, v_mla) = assemble((q_lin, kv_lin, kr, cos_m, sin_m), ())
    mla_cfg = _AttnCfg(tag + "mla", s, t, MLA_HEADS, MLA_HEADS, 256, MLA_V, _pick(s, 768, 128), False,
                       MLA_SCALE, False)
    y_mla = _attention(mla_cfg)(q_full, k_full, v_mla, jnp.zeros((MLA_HEADS,), F32))

    hparts = [(hh * 128, 128) for hh in range(RET_HEADS)]
    mix_op = _rowwise(_fn_mix, tag + "mix", 5, [hparts, hparts, hparts, [(0, 768)], [(0, 768)]],
                      [True] * 5, [], [dm_mix()], tile, n_ctx_tiles, mxu_outs=(0,))
    (mix_in,) = mix_op((ret[0], ret[1], rg, y_swa, y_mla), ())
    mix = _linear(tag + "w_o")(mix_in, wts['w_o'])
    ln1 = _rowwise(_fn_ln1, tag + "ln1", 2, [[(0, dm)], [(0, dm)]], [True, True], [True] * 5, [dm, dm],
                   tile, n_ctx_tiles, mxu_outs=(1,))
    x_a, h2 = ln1((stream, mix), (mrow(2), one(small['ln1_g'][l]), one(small['ln1_b'][l]), mrow(3), mrow(4)))
    u_lin = _linear(tag + "w_up_u")(h2, wts['ffn_w_up'][:, :dff])
    g_lin = _linear(tag + "w_up_g")(h2, wts['ffn_w_up'][:, dff:])
    conv_par = jnp.concatenate([wts['ffn_conv_w'], small['ffn_conv_b'][l][None, :], jnp.zeros((4, dff), F32)], axis=0)
    y = _conv_gate(t, tag + "conv_gate")(u_lin, g_lin, conv_par)
    f = _linear(tag + "w_down")(y, wts['ffn_w_down'])
    ln2 = _rowwise(_fn_ln2, tag + "ln2", 2, [[(0, dm)], [(0, dm)]], [True, True], [True] * 3, [dm],
                   tile, n_ctx_tiles)
    (out,) = ln2((x_a, f), (mrow(5), one(small['ln2_g'][l]), one(small['ln2_b'][l])))
    return out


def dm_mix():
    return RET_DIM + SWA_HEADS * HEAD_DIM + MLA_HEADS * MLA_V


def kernel(x, c, ctx, c_ctx, ada_w, ada_b, w_in, ret_decay_fwd, ret_decay_bwd, swa_sink, mla_q_norm, mla_w_uq, mla_kv_norm, mla_w_ukv, w_o, ln1_g, ln1_b, ffn_w_up, ffn_conv_w, ffn_conv_b, ffn_w_down, ln2_g, ln2_b, loss_target, m_c_ctx, m_ada_w, m_ada_b, m_w_in, m_ret_decay_fwd, m_ret_decay_bwd, m_swa_sink, m_mla_q_norm, m_mla_w_uq, m_mla_kv_norm, m_mla_w_ukv, m_w_o, m_ln1_g, m_ln1_b, m_ffn_w_up, m_ffn_conv_w, m_ffn_conv_b, m_ffn_w_down, m_ln2_g, m_ln2_b, v_c_ctx, v_ada_w, v_ada_b, v_w_in, v_ret_decay_fwd, v_ret_decay_bwd, v_swa_sink, v_mla_q_norm, v_mla_w_uq, v_mla_kv_norm, v_mla_w_ukv, v_w_o, v_ln1_g, v_ln1_b, v_ffn_w_up, v_ffn_conv_w, v_ffn_conv_b, v_ffn_w_down, v_ln2_g, v_ln2_b):
    weights = dict(c_ctx=c_ctx, ada_w=ada_w, ada_b=ada_b, w_in=w_in, ret_decay_fwd=ret_decay_fwd,
                   ret_decay_bwd=ret_decay_bwd, swa_sink=swa_sink, mla_q_norm=mla_q_norm, mla_w_uq=mla_w_uq,
                   mla_kv_norm=mla_kv_norm, mla_w_ukv=mla_w_ukv, w_o=w_o, ln1_g=ln1_g, ln1_b=ln1_b,
                   ffn_w_up=ffn_w_up, ffn_conv_w=ffn_conv_w, ffn_conv_b=ffn_conv_b, ffn_w_down=ffn_w_down,
                   ln2_g=ln2_g, ln2_b=ln2_b)
    m_in = dict(c_ctx=m_c_ctx, ada_w=m_ada_w, ada_b=m_ada_b, w_in=m_w_in, ret_decay_fwd=m_ret_decay_fwd,
                ret_decay_bwd=m_ret_decay_bwd, swa_sink=m_swa_sink, mla_q_norm=m_mla_q_norm, mla_w_uq=m_mla_w_uq,
                mla_kv_norm=m_mla_kv_norm, mla_w_ukv=m_mla_w_ukv, w_o=m_w_o, ln1_g=m_ln1_g, ln1_b=m_ln1_b,
                ffn_w_up=m_ffn_w_up, ffn_conv_w=m_ffn_conv_w, ffn_conv_b=m_ffn_conv_b, ffn_w_down=m_ffn_w_down,
                ln2_g=m_ln2_g, ln2_b=m_ln2_b)
    v_in = dict(c_ctx=v_c_ctx, ada_w=v_ada_w, ada_b=v_ada_b, w_in=v_w_in, ret_decay_fwd=v_ret_decay_fwd,
                ret_decay_bwd=v_ret_decay_bwd, swa_sink=v_swa_sink, mla_q_norm=v_mla_q_norm, mla_w_uq=v_mla_w_uq,
                mla_kv_norm=v_mla_kv_norm, mla_w_ukv=v_mla_w_ukv, w_o=v_w_o, ln1_g=v_ln1_g, ln1_b=v_ln1_b,
                ffn_w_up=v_ffn_w_up, ffn_conv_w=v_ffn_conv_w, ffn_conv_b=v_ffn_conv_b, ffn_w_down=v_ffn_w_down,
                ln2_g=v_ln2_g, ln2_b=v_ln2_b)

    l_tok, dm = x.shape[1], x.shape[2]
    t = ctx.shape[1]
    s = t + l_tok
    dff = ffn_w_down.shape[1] * _N_DEV
    dims = (s, t, dm, dff)
    me = 4 * lax.axis_index("x") + 2 * lax.axis_index("y") + lax.axis_index("c")
    cos_h, sin_h = _rope_tables(t, l_tok, HEAD_DIM, HEAD_DIM)
    cos_m, sin_m = _rope_tables(t, l_tok, MLA_ROPE, _LANES)
    tables = (cos_h, sin_h, cos_m, sin_m)
    c_all = _exchange(c, True, "gather_cond").reshape(_N_DEV, dm)
    tile = min(256, t)

    def loss_fn(wd, xin):
        mod_rows = jnp.concatenate([jax.nn.silu(c_all), jax.nn.silu(wd['c_ctx'])[None, :],
                                  jnp.zeros((_MOD_ROWS - _N_DEV - 1, dm), F32)], axis=0)
        mods_shard = jnp.stack([_linear("ada_l%d" % l)(mod_rows, wd['ada_w'][l]) for l in range(DEPTH)])
        n_sh = mods_shard.shape[-1]
        mods_all = _all_gather_op("mods", F32)(mods_shard.reshape(DEPTH * _MOD_ROWS, n_sh))
        mods_all = mods_all.reshape(_N_DEV, DEPTH, _MOD_ROWS, n_sh).transpose(1, 2, 0, 3).reshape(DEPTH, _MOD_ROWS, _N_DEV * n_sh)
        mods_all = mods_all + wd['ada_b'][:, None, :]
        mod_x = lax.dynamic_slice_in_dim(mods_all, me, 1, axis=1)[:, 0]
        mod_c = mods_all[:, _N_DEV]
        stream = jnp.concatenate([ctx[0], xin[0]], axis=0)
        for l in range(DEPTH):
            gathered = _weights_gather_op("weights_l%d" % l)(tuple(wd[n][l] for n in GATHERED))
            full = _full_weights(gathered)
            mod = jnp.stack([mod_c[l].reshape(N_MOD, dm), mod_x[l].reshape(N_MOD, dm)])
            stream = _layer(l, stream, mod, full, wd, tables, dims, l == DEPTH - 1)
        return _loss_op(t // tile, tile)(stream, loss_target[0])

    loss_local, (gw, gx) = jax.value_and_grad(loss_fn, argnums=(0, 1))(weights, x)
    loss = lax.psum(loss_local, ("x", "y", "c"))

    grads, deltas, new_m, new_v = {}, {}, {}, {}

    def as2d(a):
        return a.reshape(-1, a.shape[-1])

    for n in ['ada_w'] + GATHERED:
        g2, d2, m2, v2 = _adamw(as2d(weights[n]), as2d(gw[n]), as2d(m_in[n]), as2d(v_in[n]), False, "adamw_" + n)
        shp = weights[n].shape
        grads[n], deltas[n], new_m[n], new_v[n] = g2.reshape(shp), d2.reshape(shp), m2.reshape(shp), v2.reshape(shp)

    w_pack, meta = _pack([weights[n] for n in REPLICATED])
    g_pack, _ = _pack([gw[n] for n in REPLICATED])
    m_pack, _ = _pack([m_in[n] for n in REPLICATED])
    v_pack, _ = _pack([v_in[n] for n in REPLICATED])
    g_parts = _exchange(g_pack, True, "gather_small_grads")
    outs = _adamw(w_pack, g_parts, m_pack, v_pack, True, "adamw_replicated")
    for dst, packed in zip((grads, deltas, new_m, new_v), outs):
        for n, a in zip(REPLICATED, _unpack(packed, meta)):
            dst[n] = a

    return (loss, gx, *[grads[n] for n in WEIGHTS], *[deltas[n] for n in WEIGHTS],
            *[new_m[n] for n in WEIGHTS], *[new_v[n] for n in WEIGHTS])
```

```python
import functools
import math

import numpy as np
import jax
import jax.numpy as jnp
from jax import lax
from jax.experimental import pallas as pl
from jax.experimental.pallas import tpu as pltpu

F32 = jnp.float32
_MXU_DTYPE = jnp.bfloat16
_VMEM_LIMIT_BYTES = 56 * 1024 * 1024
_LANES = 128
_N_DEV = 8
_MOD_ROWS = 128

DEPTH = 4
HEAD_DIM = 128
ROPE_THETA = 10000.0
GRID_W = 64
RET_HEADS = 4
RET_DIM = RET_HEADS * HEAD_DIM
RET_CHUNK = 128
SWA_HEADS = 6
SWA_KV_HEADS = 2
SWA_WINDOW = 128
MLA_HEADS = 6
MLA_Q_RANK = 512
MLA_KV_RANK = 256
MLA_NOPE = 128
MLA_ROPE = 64
MLA_V = 128
MLA_SCALE = (MLA_NOPE + MLA_ROPE) ** -0.5
_LOG2E = math.log2(math.e)
N_MOD = 6
LN_EPS = 1e-5
RMS_EPS = 1e-6
NEG_INF = -1e30
ALPHA = (2 * DEPTH) ** 0.25
IN_WIDTH = 4160
IN_WIDTH_PAD = 4224

ADAM_LR = 0.001
ADAM_B1 = 0.9
ADAM_B2 = 0.999
ADAM_EPS = 1e-08
ADAM_WD = 0.01
ADAM_STEP = 10

WEIGHTS = ['c_ctx', 'ada_w', 'ada_b', 'w_in', 'ret_decay_fwd', 'ret_decay_bwd', 'swa_sink', 'mla_q_norm',
           'mla_w_uq', 'mla_kv_norm', 'mla_w_ukv', 'w_o', 'ln1_g', 'ln1_b', 'ffn_w_up', 'ffn_conv_w',
           'ffn_conv_b', 'ffn_w_down', 'ln2_g', 'ln2_b']
GATHERED = ['w_in', 'mla_w_uq', 'mla_w_ukv', 'w_o', 'ffn_w_up', 'ffn_conv_w', 'ffn_w_down']
GATHER_AXIS = {'w_in': 1, 'mla_w_uq': 1, 'mla_w_ukv': 1, 'w_o': 0, 'ffn_w_up': 1, 'ffn_conv_w': 1, 'ffn_w_down': 0}
REPLICATED = ['c_ctx', 'ada_b', 'ret_decay_fwd', 'ret_decay_bwd', 'swa_sink', 'mla_q_norm', 'mla_kv_norm',
              'ln1_g', 'ln1_b', 'ffn_conv_b', 'ln2_g', 'ln2_b']


def _cparams(semantics):
    return pltpu.CompilerParams(dimension_semantics=semantics, vmem_limit_bytes=_VMEM_LIMIT_BYTES)


def _pick(n, target, align):
    best = None
    d = align
    while d <= min(n, target):
        if n % d == 0:
            best = d
        d += align
    return n if best is None else best


def _matmul(a, b, mode, name, out_dtype=F32):
    if mode == 'nn':
        (m, k), (k2, n) = a.shape, b.shape
    elif mode == 'nt':
        (m, k), (n, k2) = a.shape, b.shape
    else:
        (k, m), (k2, n) = a.shape, b.shape
    assert k == k2, (a.shape, b.shape, mode)
    tm = _pick(m, 1024, 128)
    tn = _pick(n, 1408, 128)
    tk = _pick(k, 1408 if mode == 'tn' else 2048, 128)
    nk = k // tk

    def body(a_ref, b_ref, o_ref, *scratch):
        kk = pl.program_id(2)
        if mode == 'nn':
            part = jnp.dot(a_ref[...].astype(_MXU_DTYPE), b_ref[...].astype(_MXU_DTYPE),
                           preferred_element_type=F32)
        elif mode == 'nt':
            part = lax.dot_general(a_ref[...].astype(_MXU_DTYPE), b_ref[...].astype(_MXU_DTYPE),
                                   (((1,), (1,)), ((), ())), preferred_element_type=F32)
        else:
            part = lax.dot_general(a_ref[...].astype(_MXU_DTYPE), b_ref[...].astype(_MXU_DTYPE),
                                   (((0,), (0,)), ((), ())), preferred_element_type=F32)
        if nk == 1:
            o_ref[...] = part.astype(o_ref.dtype)
            return
        acc_ref, = scratch

        @pl.when(kk == 0)
        def _():
            acc_ref[...] = part

        @pl.when((kk > 0) & (kk < nk - 1))
        def _():
            acc_ref[...] += part

        @pl.when(kk == nk - 1)
        def _():
            o_ref[...] = (acc_ref[...] + part).astype(o_ref.dtype)

    if mode == 'nn':
        a_spec = pl.BlockSpec((tm, tk), lambda i, j, kk: (i, kk))
        b_spec = pl.BlockSpec((tk, tn), lambda i, j, kk: (kk, j))
    elif mode == 'nt':
        a_spec = pl.BlockSpec((tm, tk), lambda i, j, kk: (i, kk))
        b_spec = pl.BlockSpec((tn, tk), lambda i, j, kk: (j, kk))
    else:
        a_spec = pl.BlockSpec((tk, tm), lambda i, j, kk: (kk, i))
        b_spec = pl.BlockSpec((tk, tn), lambda i, j, kk: (kk, j))
    return pl.pallas_call(
        body, name=name,
        grid=(m // tm, n // tn, nk),
        in_specs=[a_spec, b_spec],
        out_specs=pl.BlockSpec((tm, tn), lambda i, j, kk: (i, j)),
        out_shape=jax.ShapeDtypeStruct((m, n), out_dtype),
        scratch_shapes=[pltpu.VMEM((tm, tn), F32)] if nk > 1 else [],
        compiler_params=_cparams(("parallel", "parallel", "arbitrary")),
    )(a, b)


def _linear(name):
    @jax.custom_vjp
    def op(a, w):
        return _matmul(a.astype(_MXU_DTYPE), w.astype(_MXU_DTYPE), 'nn', name + "_fwd")

    def fwd(a, w):
        ab, wb = a.astype(_MXU_DTYPE), w.astype(_MXU_DTYPE)
        return _matmul(ab, wb, 'nn', name + "_fwd"), (ab, wb.T, jnp.zeros((0,), a.dtype))

    def bwd(res, g):
        ab, wbt, a_like = res
        gb = g.astype(_MXU_DTYPE)
        da = _matmul(gb, wbt, 'nn', name + "_da", a_like.dtype)
        dw = _matmul(ab, gb, 'tn', name + "_dw")
        return da, dw

    op.defvjp(fwd, bwd)
    return op


def _pieces(parts):
    out = []
    for p, (start, width) in enumerate(parts):
        pw = math.gcd(start, width) if start else width
        assert pw % _LANES == 0, (start, width)
        for t in range(width // pw):
            out.append((p, pw, start // pw + t))
    return out


def _rowwise(fn, name, rows, parts, diff, pdiff, out_widths, tile, n_ctx_tiles, col_tile=None, mxu_outs=()):
    pieces = [_pieces(p) for p in parts]

    def sel_of(i, n_sel):
        return jnp.where(i >= n_ctx_tiles, n_sel - 1, 0)

    def in_specs_for(row_arrays, params):
        specs, operands = [], []
        for r in range(rows):
            for (_, pw, blk) in pieces[r]:
                if col_tile is None:
                    specs.append(pl.BlockSpec((tile, pw), lambda j, i, blk=blk: (i, blk)))
                else:
                    nb = pw // col_tile
                    specs.append(pl.BlockSpec((tile, col_tile), lambda j, i, blk=blk, nb=nb: (i, blk * nb + j)))
                operands.append(row_arrays[r])
        for p in params:
            n_sel, _, w = p.shape
            cw = w if col_tile is None else col_tile
            if col_tile is None:
                specs.append(pl.BlockSpec((None, 1, cw), lambda j, i, n_sel=n_sel: (sel_of(i, n_sel), 0, 0)))
            else:
                specs.append(pl.BlockSpec((None, 1, cw), lambda j, i, n_sel=n_sel: (sel_of(i, n_sel), 0, j)))
            operands.append(p)
        return specs, operands

    def load_inputs(refs):
        k = 0
        vals = []
        for r in range(rows):
            got = [[] for _ in parts[r]]
            for (p, _, _) in pieces[r]:
                got[p].append(refs[k][...].astype(F32))
                k += 1
            vals.append([g[0] if len(g) == 1 else jnp.concatenate(g, axis=1) for g in got])
        return vals, k

    def forward(row_arrays, params):
        s = row_arrays[0].shape[0]
        ncol = 1 if col_tile is None else out_widths[0] // col_tile
        n_par = len(params)

        def body(*refs):
            vals, k = load_inputs(refs)
            pvals = [refs[k + q][...].astype(F32) for q in range(n_par)]
            outs = fn(vals, pvals)
            for o_ref, o in zip(refs[k + n_par:], outs):
                o_ref[...] = o.astype(o_ref.dtype)

        specs, operands = in_specs_for(row_arrays, params)
        if col_tile is None:
            out_specs = [pl.BlockSpec((tile, w), lambda j, i: (i, 0)) for w in out_widths]
        else:
            out_specs = [pl.BlockSpec((tile, col_tile), lambda j, i: (i, j)) for _ in out_widths]
        return pl.pallas_call(
            body, name=name + "_fwd",
            grid=(ncol, s // tile),
            in_specs=specs, out_specs=out_specs,
            out_shape=[jax.ShapeDtypeStruct((s, w), _MXU_DTYPE if o in mxu_outs else F32)
                       for o, w in enumerate(out_widths)],
            compiler_params=_cparams(("arbitrary", "arbitrary")),
        )(*operands)

    def backward(row_arrays, params, cts):
        s = row_arrays[0].shape[0]
        ncol = 1 if col_tile is None else out_widths[0] // col_tile
        n_par = len(params)
        n_out = len(out_widths)
        d_rows = [r for r in range(rows) if diff[r]]
        d_pars = [q for q in range(n_par) if pdiff[q]]

        def body(*refs):
            i = pl.program_id(1)
            vals, k = load_inputs(refs)
            pvals = [refs[k + q][...].astype(F32) for q in range(n_par)]
            k += n_par
            ct_vals = [refs[k + o][...].astype(F32) for o in range(n_out)]
            k += n_out
            drow_refs = refs[k:k + len(d_rows)]
            dpar_refs = refs[k + len(d_rows):]

            def f(dv, dp):
                full_v = list(vals)
                for r, v in zip(d_rows, dv):
                    full_v[r] = v
                full_p = list(pvals)
                for q, v in zip(d_pars, dp):
                    full_p[q] = v
                return fn(full_v, full_p)

            _, vjp = jax.vjp(f, [vals[r] for r in d_rows], [pvals[q] for q in d_pars])
            g_rows, g_pars = vjp(ct_vals)
            for ref, r, g in zip(drow_refs, d_rows, g_rows):
                covered = sum(w for (_, w) in parts[r])
                if col_tile is None:
                    if covered != ref.shape[1]:
                        ref[...] = jnp.zeros_like(ref)
                    for (start, width), gp in zip(parts[r], g):
                        ref[:, start:start + width] = gp
                else:
                    ref[...] = g[0]
            for ref, q, g in zip(dpar_refs, d_pars, g_pars):
                n_sel = params[q].shape[0]
                first = (i == 0) if n_sel == 1 else ((i == 0) | (i == n_ctx_tiles))

                @pl.when(first)
                def _(ref=ref):
                    ref[...] = jnp.zeros_like(ref)

                ref[...] += g

        specs, operands = in_specs_for(row_arrays, params)
        for o, w in enumerate(out_widths):
            if col_tile is None:
                specs.append(pl.BlockSpec((tile, w), lambda j, i: (i, 0)))
            else:
                specs.append(pl.BlockSpec((tile, col_tile), lambda j, i: (i, j)))
            operands.append(cts[o])
        out_specs, out_shape = [], []
        for r in d_rows:
            w = row_arrays[r].shape[1]
            if col_tile is None:
                out_specs.append(pl.BlockSpec((tile, w), lambda j, i: (i, 0)))
            else:
                assert len(parts[r]) == 1 and parts[r][0] == (0, w)
                out_specs.append(pl.BlockSpec((tile, col_tile), lambda j, i: (i, j)))
            out_shape.append(jax.ShapeDtypeStruct((s, w), F32))
        for q in d_pars:
            n_sel, _, w = params[q].shape
            cw = w if col_tile is None else col_tile
            if col_tile is None:
                out_specs.append(pl.BlockSpec((None, 1, cw), lambda j, i, n_sel=n_sel: (sel_of(i, n_sel), 0, 0)))
            else:
                out_specs.append(pl.BlockSpec((None, 1, cw), lambda j, i, n_sel=n_sel: (sel_of(i, n_sel), 0, j)))
            out_shape.append(jax.ShapeDtypeStruct((n_sel, 1, w), F32))
        res = pl.pallas_call(
            body, name=name + "_bwd",
            grid=(ncol, s // tile),
            in_specs=specs, out_specs=out_specs, out_shape=out_shape,
            compiler_params=_cparams(("arbitrary", "arbitrary")),
        )(*operands)
        g_rows = [None] * rows
        for r, g in zip(d_rows, res[:len(d_rows)]):
            g_rows[r] = g
        g_pars = [None] * n_par
        for q, g in zip(d_pars, res[len(d_rows):]):
            g_pars[q] = g
        return g_rows, g_pars

    @jax.custom_vjp
    def op(row_arrays, params):
        return tuple(forward(list(row_arrays), list(params)))

    def op_fwd(row_arrays, params):
        return tuple(forward(list(row_arrays), list(params))), (row_arrays, params)

    def op_bwd(res, cts):
        row_arrays, params = res
        g_rows, g_pars = backward(list(row_arrays), list(params), list(cts))
        g_rows = tuple(jnp.zeros_like(a) if g is None else g for a, g in zip(row_arrays, g_rows))
        g_pars = tuple(jnp.zeros_like(a) if g is None else g for a, g in zip(params, g_pars))
        return g_rows, g_pars

    op.defvjp(op_fwd, op_bwd)
    return op


def _rot_impl(x, quarter):
    lane = lax.broadcasted_iota(jnp.int32, (x.shape[0], _LANES), 1)
    even = ((lane // quarter) % 2) == 0
    outs = []
    for k in range(x.shape[1] // _LANES):
        xs = x[:, k * _LANES:(k + 1) * _LANES]
        left = pltpu.roll(xs, _LANES - quarter, 1)
        right = pltpu.roll(xs, quarter, 1)
        outs.append(jnp.where(even, -left, right))
    return outs[0] if len(outs) == 1 else jnp.concatenate(outs, axis=1)


def _make_rot(quarter):
    @jax.custom_vjp
    def rot(x):
        return _rot_impl(x, quarter)

    rot.defvjp(lambda x: (_rot_impl(x, quarter), None), lambda _, g: (-_rot_impl(g, quarter),))
    return rot


_rot32 = _make_rot(32)
_rot16 = _make_rot(16)


def _tile_lanes(t, n):
    return t if n == 1 else jnp.concatenate([t] * n, axis=1)


def _rope(x, cos, sin, rot):
    n = x.shape[1] // _LANES
    return x * _tile_lanes(cos, n) + rot(x) * _tile_lanes(sin, n)


def _rms(x):
    return x * lax.rsqrt(jnp.mean(x * x, axis=-1, keepdims=True) + RMS_EPS)


def _ln(x):
    mu = jnp.mean(x, axis=-1, keepdims=True)
    xc = x - mu
    var = jnp.mean(xc * xc, axis=-1, keepdims=True)
    return xc * lax.rsqrt(var + LN_EPS)


def _sum_all(x):
    return jnp.sum(jnp.sum(x, axis=1, keepdims=True), axis=0, keepdims=True)


def _silu(x):
    return x * (1.0 / (1.0 + jnp.exp(-x)))


def _fn_modulate(vals, pars):
    (s,), = vals
    shift, scale = pars
    return [s * (1.0 + scale) + shift]


def _fn_postproj(vals, pars):
    (rq, rk, rv, rg, sq, sk, sv, mcq, mckv, mkr), (cos_h,), (sin_h,), (cos_m,), (sin_m,) = vals
    q_norm, kv_norm = pars
    k_scale = HEAD_DIM ** -0.5
    return [_rope(rq, cos_h, sin_h, _rot32), _rope(rk, cos_h, sin_h, _rot32) * k_scale, rv, rg,
            _rope(sq, cos_h, sin_h, _rot32) * (k_scale * _LOG2E), _rope(sk, cos_h, sin_h, _rot32), sv,
            _rms(mcq) * q_norm, _rms(mckv) * kv_norm, _rope(mkr, cos_m, sin_m, _rot16)]


def _fn_mla_assemble(vals, pars):
    (q_lin,), kn_v, (kr,), (cos_m,), (sin_m,) = vals
    kn, vv = kn_v[:MLA_HEADS], kn_v[MLA_HEADS]
    ones, zeros = jnp.ones_like(cos_m), jnp.zeros_like(sin_m)
    cos_q = jnp.concatenate([ones, cos_m] * MLA_HEADS, axis=1)
    sin_q = jnp.concatenate([zeros, sin_m] * MLA_HEADS, axis=1)
    q_full = (q_lin * cos_q + _rot16(q_lin) * sin_q) * (MLA_SCALE * _LOG2E)
    k_full = jnp.concatenate([t for h in range(MLA_HEADS) for t in (kn[h], kr)], axis=1)
    return [q_full, k_full, vv]


def _fn_mix(vals, pars):
    ret_f, ret_b, rg, (y_swa,), (y_mla,) = vals
    heads = [_silu(rg[h]) * _rms(ret_f[h] + ret_b[h]) for h in range(RET_HEADS)]
    return [jnp.concatenate(heads + [y_swa, y_mla], axis=1)]


def _fn_ln1(vals, pars):
    (s,), (mix,) = vals
    gate, g, b, shift_f, scale_f = pars
    x_a = _ln(ALPHA * s + (1.0 + gate) * mix) * g + b
    return [x_a, x_a * (1.0 + scale_f) + shift_f]


def _fn_ln2(vals, pars):
    (x_a,), (f,) = vals
    gate, g, b = pars
    return [_ln(ALPHA * x_a + (1.0 + gate) * f) * g + b]


def _conv_tiles(s, t, f):
    r = min(256, t)
    assert t % r == 0 and s % r == 0 and r % 8 == 0
    return r, _pick(f, 1408, _LANES)


def _conv_gate_fwd(u, g, par, t, name):
    s, f = u.shape
    r, cw = _conv_tiles(s, t, f)
    n_ctx, n_tiles, per = t // r, s // r, r // 8

    def body(u_ref, g_ref, gp_ref, gn_ref, p_ref, y_ref):
        i = pl.program_id(1)
        gv = g_ref[...]
        row = lax.broadcasted_iota(jnp.int32, (r, 1), 0)
        seg_start = (i == 0) | (i == n_ctx)
        seg_end = (i == n_ctx - 1) | (i == n_tiles - 1)
        prev_row = jnp.where(seg_start, 0.0, gp_ref[7:8, :])
        next_row = jnp.where(seg_end, 0.0, gn_ref[0:1, :])
        gp = jnp.where(row == 0, prev_row, pltpu.roll(gv, 1, 0))
        gn = jnp.where(row == r - 1, next_row, pltpu.roll(gv, r - 1, 0))
        gc = p_ref[0:1, :] * gp + p_ref[1:2, :] * gv + p_ref[2:3, :] * gn + p_ref[3:4, :]
        y_ref[...] = (_silu(gc) * u_ref[...]).astype(y_ref.dtype)

    tile = pl.BlockSpec((r, cw), lambda j, i: (i, j))
    return pl.pallas_call(
        body, name=name + "_fwd",
        grid=(f // cw, n_tiles),
        in_specs=[tile, tile,
                  pl.BlockSpec((8, cw), lambda j, i: (jnp.maximum(i * per - 1, 0), j)),
                  pl.BlockSpec((8, cw), lambda j, i: (jnp.minimum((i + 1) * per, s // 8 - 1), j)),
                  pl.BlockSpec((8, cw), lambda j, i: (0, j))],
        out_specs=tile,
        out_shape=jax.ShapeDtypeStruct((s, f), _MXU_DTYPE),
        compiler_params=_cparams(("arbitrary", "arbitrary")),
    )(u, g, g, g, par)


def _conv_gate_bwd(u, g, par, dy, t, name):
    s, f = u.shape
    r, cw = _conv_tiles(s, t, f)
    halo = 16
    assert r % halo == 0
    n_tiles, per = s // r, r // halo
    re = r + 2 * halo

    def body(u_ref, up_ref, un_ref, g_ref, gp_ref, gn_ref, dy_ref, dyp_ref, dyn_ref, p_ref,
             du_ref, dg_ref, dp_ref):
        i = pl.program_id(1)

        def ext(prev, cur, nxt):
            return jnp.concatenate([prev[...].astype(F32), cur[...].astype(F32), nxt[...].astype(F32)], axis=0)

        ge, ue, dye = ext(gp_ref, g_ref, gn_ref), ext(up_ref, u_ref, un_ref), ext(dyp_ref, dy_ref, dyn_ref)
        grow = i * r - halo + lax.broadcasted_iota(jnp.int32, (re, 1), 0)
        is_start = (grow == 0) | (grow == t)
        is_end = (grow == t - 1) | (grow == s - 1)
        inside = (grow >= 0) & (grow < s)
        w0, w1, w2, bias = p_ref[0:1, :], p_ref[1:2, :], p_ref[2:3, :], p_ref[3:4, :]
        gpe = jnp.where(is_start, 0.0, pltpu.roll(ge, 1, 0))
        gne = jnp.where(is_end, 0.0, pltpu.roll(ge, re - 1, 0))
        gce = w0 * gpe + w1 * ge + w2 * gne + bias
        sig = 1.0 / (1.0 + jnp.exp(-gce))
        dgce = jnp.where(inside, dye * ue * (sig * (1.0 + gce * (1.0 - sig))), 0.0)
        dge = (w1 * dgce + w0 * jnp.where(is_end, 0.0, pltpu.roll(dgce, re - 1, 0))
               + w2 * jnp.where(is_start, 0.0, pltpu.roll(dgce, 1, 0)))
        mid = slice(halo, r + halo)
        dg_ref[...] = dge[mid]
        du_ref[...] = (dye * gce * sig)[mid]
        dgc = dgce[mid]

        @pl.when(i == 0)
        def _():
            dp_ref[...] = jnp.zeros_like(dp_ref)

        dp_ref[0:1, :] += jnp.sum(dgc * gpe[mid], axis=0, keepdims=True)
        dp_ref[1:2, :] += jnp.sum(dgc * ge[mid], axis=0, keepdims=True)
        dp_ref[2:3, :] += jnp.sum(dgc * gne[mid], axis=0, keepdims=True)
        dp_ref[3:4, :] += jnp.sum(dgc, axis=0, keepdims=True)

    tile = pl.BlockSpec((r, cw), lambda j, i: (i, j))
    prev = pl.BlockSpec((halo, cw), lambda j, i: (jnp.maximum(i * per - 1, 0), j))
    nxt = pl.BlockSpec((halo, cw), lambda j, i: (jnp.minimum((i + 1) * per, s // halo - 1), j))
    par_spec = pl.BlockSpec((8, cw), lambda j, i: (0, j))
    return pl.pallas_call(
        body, name=name + "_bwd",
        grid=(f // cw, n_tiles),
        in_specs=[tile, prev, nxt, tile, prev, nxt, tile, prev, nxt, par_spec],
        out_specs=[tile, tile, par_spec],
        out_shape=[jax.ShapeDtypeStruct((s, f), F32), jax.ShapeDtypeStruct((s, f), F32),
                   jax.ShapeDtypeStruct((8, f), F32)],
        compiler_params=_cparams(("arbitrary", "arbitrary")),
    )(u, u, u, g, g, g, dy, dy, dy, par)


def _conv_gate(t, name):
    @jax.custom_vjp
    def op(u, g, par):
        return _conv_gate_fwd(u, g, par, t, name)

    def fwd(u, g, par):
        return _conv_gate_fwd(u, g, par, t, name), (u, g, par)

    def bwd(res, dy):
        u, g, par = res
        return _conv_gate_bwd(u, g, par, dy, t, name)

    op.defvjp(fwd, bwd)
    return op


class _AttnCfg:
    def __init__(self, name, s, t, heads, kv_heads, dk, dv, blk, band, scale, has_sink):
        self.name, self.s, self.t = name, s, t
        self.heads, self.kv_heads, self.group = heads, kv_heads, heads // kv_heads
        self.dk, self.dv, self.blk, self.band, self.scale, self.has_sink = dk, dv, blk, band, scale, has_sink
        self.nq = s // blk
        self.n_ctx = t // blk if band else 0
        self.ks = self.n_ctx + 3 if band else s // blk
        assert s % blk == 0 and (not band or (t % blk == 0 and blk >= SWA_WINDOW))

    def kblock(self, i, st):
        if not self.band:
            return st
        kb = jnp.clip(i + st - self.n_ctx - 1, self.n_ctx, self.nq - 1)
        return jnp.where(st < self.n_ctx, st, kb)

    def valid(self, i, st):
        if not self.band:
            return st >= 0
        kb = i + st - self.n_ctx - 1
        return (st < self.n_ctx) | ((i >= self.n_ctx) & (kb >= self.n_ctx) & (kb <= self.nq - 1))

    def masked(self, i, st):
        if self.band:
            return st >= self.n_ctx
        return i * self.blk < self.t

    def visible(self, i, kb, keys_first=False):
        b = self.blk
        qpos = i * b + lax.broadcasted_iota(jnp.int32, (b, b), 1 if keys_first else 0)
        kpos = kb * b + lax.broadcasted_iota(jnp.int32, (b, b), 0 if keys_first else 1)
        if self.band:
            return jnp.abs(qpos - kpos) <= SWA_WINDOW
        return (kpos < self.t) | (qpos >= self.t)


def _attn_fwd(cfg, q, k, vt, sink):
    b, dk, dv, g = cfg.blk, cfg.dk, cfg.dv, cfg.group

    def body(q_ref, k_ref, vt_ref, sink_ref, o_ref, lse_ref, m_sc, l_sc, acc_sc):
        i, st = pl.program_id(1), pl.program_id(2)

        @pl.when(st == 0)
        def _():
            if cfg.has_sink:
                for gg in range(g):
                    m_sc[gg] = jnp.broadcast_to(sink_ref[gg, 0:1, 0:1], (8, b))
                l_sc[...] = jnp.ones_like(l_sc)
            else:
                m_sc[...] = jnp.full_like(m_sc, NEG_INF)
                l_sc[...] = jnp.zeros_like(l_sc)
            acc_sc[...] = jnp.zeros_like(acc_sc)

        def step(use_mask):
            vis = cfg.visible(i, cfg.kblock(i, st), keys_first=True) if use_mask else None
            kv, vtv = k_ref[...], vt_ref[...]
            for gg in range(g):
                sc = lax.dot_general(kv, q_ref[:, gg * dk:(gg + 1) * dk], _NT,
                                     preferred_element_type=F32)
                if use_mask:
                    sc = jnp.where(vis, sc, NEG_INF)
                m_prev = m_sc[gg, 0:1, :]
                m_new = jnp.maximum(m_prev, jnp.max(sc, axis=0, keepdims=True))
                alpha = jnp.exp2(m_prev - m_new)
                p = jnp.exp2(sc - m_new)
                l_new = alpha * l_sc[gg, 0:1, :] + jnp.sum(p, axis=0, keepdims=True)
                acc_sc[gg] = acc_sc[gg] * alpha + jnp.dot(vtv, p.astype(_MXU_DTYPE), preferred_element_type=F32)
                m_sc[gg] = jnp.broadcast_to(m_new, (8, b))
                l_sc[gg] = jnp.broadcast_to(l_new, (8, b))

        ok = cfg.valid(i, st)
        msk = cfg.masked(i, st)
        pl.when(ok & msk)(lambda: step(True))
        pl.when(ok & jnp.logical_not(msk))(lambda: step(False))

        @pl.when(st == cfg.ks - 1)
        def _():
            for gg in range(g):
                o_ref[gg * dv:(gg + 1) * dv, :] = acc_sc[gg] / l_sc[gg, 0:1, :]
            lse_ref[...] = m_sc[...] + jnp.log2(l_sc[...])

    return pl.pallas_call(
        body, name=cfg.name + "_fwd",
        grid=(cfg.kv_heads, cfg.nq, cfg.ks),
        in_specs=[pl.BlockSpec((b, g * dk), lambda h, i, st: (i, h)),
                  pl.BlockSpec((b, dk), lambda h, i, st: (cfg.kblock(i, st), h)),
                  pl.BlockSpec((dv, b), lambda h, i, st: (h, cfg.kblock(i, st))),
                  pl.BlockSpec((g, 8, _LANES), lambda h, i, st: (h, 0, 0))],
        out_specs=[pl.BlockSpec((g * dv, b), lambda h, i, st: (h, i)),
                   pl.BlockSpec((g, 8, b), lambda h, i, st: (h, 0, i))],
        out_shape=[jax.ShapeDtypeStruct((cfg.heads * dv, cfg.s), F32),
                   jax.ShapeDtypeStruct((cfg.heads, 8, cfg.s), F32)],
        scratch_shapes=[pltpu.VMEM((g, 8, b), F32), pltpu.VMEM((g, 8, b), F32), pltpu.VMEM((g, dv, b), F32)],
        compiler_params=_cparams(("parallel", "parallel", "arbitrary")),
    )(q, k, vt, sink)


def _attn_bwd(cfg, q, k, kt, v, sink, ot, lse, do, dot_):
    b, dk, dv, g = cfg.blk, cfg.dk, cfg.dv, cfg.group

    def body(q_ref, k_ref, kt_ref, v_ref, sink_ref, ot_ref, lse_ref, do_ref, dot_ref,
             dqt_ref, dk_ref, dv_ref, dsink_ref, dqt_sc, delta_sc):
        i, st = pl.program_id(1), pl.program_id(2)

        @pl.when((i == 0) & (st == 0))
        def _():
            dk_ref[...] = jnp.zeros_like(dk_ref)
            dv_ref[...] = jnp.zeros_like(dv_ref)

        @pl.when(st == 0)
        def _():
            dqt_sc[...] = jnp.zeros_like(dqt_sc)
            for gg in range(g):
                vs = slice(gg * dv, (gg + 1) * dv)
                delta = jnp.sum(dot_ref[vs, :] * ot_ref[vs, :], axis=0, keepdims=True)
                delta_sc[gg] = jnp.broadcast_to(delta, (8, b))
                if cfg.has_sink:
                    ps = jnp.exp2(sink_ref[gg, 0:1, 0:1] - lse_ref[gg, 0:1, :]) * delta
                    dsink_ref[gg] = jnp.broadcast_to(-jnp.sum(ps, axis=1, keepdims=True), (8, _LANES))
                else:
                    dsink_ref[gg] = jnp.zeros((8, _LANES), F32)

        def step(use_mask):
            kb = cfg.kblock(i, st)
            vis = cfg.visible(i, kb, keys_first=True) if use_mask else None
            kbv, ktv, vbv = k_ref[...], kt_ref[...], v_ref[...]
            rows = pl.ds(pl.multiple_of(kb * b, b), b)
            dk_acc = None
            dv_acc = None
            for gg in range(g):
                ks, vs = slice(gg * dk, (gg + 1) * dk), slice(gg * dv, (gg + 1) * dv)
                qb = q_ref[:, ks]
                sc = lax.dot_general(kbv, qb, _NT, preferred_element_type=F32)
                if use_mask:
                    sc = jnp.where(vis, sc, NEG_INF)
                p = jnp.exp2(sc - lse_ref[gg, 0:1, :])
                dp = jnp.dot(vbv, dot_ref[vs, :].astype(_MXU_DTYPE), preferred_element_type=F32)
                ds = (p * (dp - delta_sc[gg, 0:1, :])).astype(_MXU_DTYPE)
                dqt_sc[gg] += jnp.dot(ktv, ds, preferred_element_type=F32)
                dk_h = jnp.dot(ds, qb, preferred_element_type=F32)
                dv_h = jnp.dot(p.astype(_MXU_DTYPE), do_ref[:, vs].astype(_MXU_DTYPE), preferred_element_type=F32)
                dk_acc = dk_h if dk_acc is None else dk_acc + dk_h
                dv_acc = dv_h if dv_acc is None else dv_acc + dv_h
            dk_ref[rows, :] += dk_acc
            dv_ref[rows, :] += dv_acc

        ok = cfg.valid(i, st)
        msk = cfg.masked(i, st)
        pl.when(ok & msk)(lambda: step(True))
        pl.when(ok & jnp.logical_not(msk))(lambda: step(False))

        @pl.when(st == cfg.ks - 1)
        def _():
            for gg in range(g):
                dqt_ref[gg * dk:(gg + 1) * dk, :] = (dqt_sc[gg] * (1.0 / _LOG2E)).astype(dqt_ref.dtype)

    return pl.pallas_call(
        body, name=cfg.name + "_bwd",
        grid=(cfg.kv_heads, cfg.nq, cfg.ks),
        in_specs=[pl.BlockSpec((b, g * dk), lambda h, i, st: (i, h)),
                  pl.BlockSpec((b, dk), lambda h, i, st: (cfg.kblock(i, st), h)),
                  pl.BlockSpec((dk, b), lambda h, i, st: (h, cfg.kblock(i, st))),
                  pl.BlockSpec((b, dv), lambda h, i, st: (cfg.kblock(i, st), h)),
                  pl.BlockSpec((g, 8, _LANES), lambda h, i, st: (h, 0, 0)),
                  pl.BlockSpec((g * dv, b), lambda h, i, st: (h, i)),
                  pl.BlockSpec((g, 8, b), lambda h, i, st: (h, 0, i)),
                  pl.BlockSpec((b, g * dv), lambda h, i, st: (i, h)),
                  pl.BlockSpec((g * dv, b), lambda h, i, st: (h, i))],
        out_specs=[pl.BlockSpec((g * dk, b), lambda h, i, st: (h, i)),
                   pl.BlockSpec((cfg.s, dk), lambda h, i, st: (0, h)),
                   pl.BlockSpec((cfg.s, dv), lambda h, i, st: (0, h)),
                   pl.BlockSpec((g, None, 8, _LANES), lambda h, i, st: (h, i, 0, 0))],
        out_shape=[jax.ShapeDtypeStruct((cfg.heads * dk, cfg.s), q.dtype),
                   jax.ShapeDtypeStruct((cfg.s, cfg.kv_heads * dk), F32),
                   jax.ShapeDtypeStruct((cfg.s, cfg.kv_heads * dv), F32),
                   jax.ShapeDtypeStruct((cfg.heads, cfg.nq, 8, _LANES), F32)],
        scratch_shapes=[pltpu.VMEM((g, dk, b), F32), pltpu.VMEM((g, 8, b), F32)],
        compiler_params=_cparams(("parallel", "arbitrary", "arbitrary")),
    )(q, k, kt, v, sink, ot, lse, do, dot_)


def _attention(cfg):
    def run(q, k, v, sink):
        sr = jnp.broadcast_to((sink.astype(F32) * _LOG2E)[:, None, None], (cfg.heads, 8, _LANES))
        ot, lse = _attn_fwd(cfg, q, k, v.T, sr)
        return ot.T, (sr, ot, lse)

    @jax.custom_vjp
    def op(q, k, v, sink):
        return run(q, k, v, sink)[0]

    def fwd(q, k, v, sink):
        o, (sr, ot, lse) = run(q, k, v, sink)
        return o, (q, k, v, sr, ot, lse)

    def bwd(res, do):
        q, k, v, sr, ot, lse = res
        dqt, dk, dv, dsink = _attn_bwd(cfg, q, k, k.T, v, sr, ot, lse, do, do.T)
        return dqt.T, (dk * (1.0 / _LOG2E)).astype(k.dtype), dv.astype(v.dtype), dsink[:, :, 0, 0].sum(1)

    op.defvjp(fwd, bwd)
    return op


def _ret_chunk_order(n, d, n_ctx, n_all):
    fwd = n
    bwd = jnp.where(n < n_ctx, n_ctx - 1 - n, n_all - 1 - (n - n_ctx))
    return jnp.where(d == 0, fwd, bwd)


def _ret_decays(lg, d):
    c = RET_CHUNK
    i = lax.broadcasted_iota(jnp.int32, (c, c), 0)
    j = lax.broadcasted_iota(jnp.int32, (c, c), 1)
    sign = (1 - 2 * d)
    diff = ((i - j) * sign).astype(F32)
    intra = jnp.where(diff >= 0, jnp.exp(lg * jnp.maximum(diff, 0.0)), 0.0)
    pos = lax.broadcasted_iota(jnp.int32, (c, 1), 0)
    r = (pos + d * (c - 1 - 2 * pos)).astype(F32)
    qd = jnp.exp(lg * (r + 1.0))
    kd = jnp.exp(lg * (c - 1.0 - r))
    cd = jnp.exp(lg * c)
    return intra, qd, kd, cd, diff, r


def _mxu_dot(a, b, dims=None):
    a, b = a.astype(_MXU_DTYPE), b.astype(_MXU_DTYPE)
    if dims is None:
        return jnp.dot(a, b, preferred_element_type=F32)
    return lax.dot_general(a, b, dims, preferred_element_type=F32)


_NT = (((1,), (1,)), ((), ()))


def _ret_fwd(q, k, v, lg, n_ctx, name):
    s = q.shape[0]
    c, hd = RET_CHUNK, HEAD_DIM
    n_all = s // c

    def body(lg_ref, q_ref, k_ref, v_ref, o_ref, st_ref, state_sc):
        d, n = pl.program_id(0), pl.program_id(1)

        @pl.when(n == 0)
        def _():
            state_sc[...] = jnp.zeros_like(state_sc)

        for h in range(RET_HEADS):
            cols = slice(h * hd, (h + 1) * hd)
            intra, qd, kd, cd, _, _ = _ret_decays(lg_ref[d, h], d)
            qv, kv, vv = q_ref[:, cols], k_ref[:, cols], v_ref[:, cols]
            s_in = state_sc[h]
            st_ref[h] = s_in
            p = _mxu_dot(qv, kv, _NT) * intra
            o_ref[:, cols] = _mxu_dot(p, vv) + _mxu_dot(qv * qd, s_in)
            state_sc[h] = cd * s_in + _mxu_dot((kv * kd).T, vv)

    def chunk_spec():
        return pl.BlockSpec((c, RET_DIM), lambda d, n: (_ret_chunk_order(n, d, n_ctx, n_all), 0))

    return pl.pallas_call(
        body, name=name + "_fwd",
        grid=(2, n_all),
        in_specs=[pl.BlockSpec(memory_space=pltpu.SMEM), chunk_spec(), chunk_spec(), chunk_spec()],
        out_specs=[pl.BlockSpec((None, c, RET_DIM), lambda d, n: (d, _ret_chunk_order(n, d, n_ctx, n_all), 0)),
                   pl.BlockSpec((None, RET_HEADS, None, hd, hd), lambda d, n: (d, 0, n, 0, 0))],
        out_shape=[jax.ShapeDtypeStruct((2, s, RET_DIM), F32),
                   jax.ShapeDtypeStruct((2, RET_HEADS, n_all, hd, hd), F32)],
        scratch_shapes=[pltpu.VMEM((RET_HEADS, hd, hd), F32)],
        compiler_params=_cparams(("arbitrary", "arbitrary")),
    )(lg, q, k, v)


def _ret_bwd(q, k, v, lg, states, dout, n_ctx, name):
    s = q.shape[0]
    c, hd = RET_CHUNK, HEAD_DIM
    n_all = s // c

    def body(lg_ref, q_ref, k_ref, v_ref, st_ref, do_ref, dq_ref, dk_ref, dv_ref, dlg_ref, ds_sc):
        d, n = pl.program_id(0), pl.program_id(1)

        @pl.when(n == 0)
        def _():
            ds_sc[...] = jnp.zeros_like(ds_sc)
            dlg_ref[...] = jnp.zeros_like(dlg_ref)

        for h in range(RET_HEADS):
            cols = slice(h * hd, (h + 1) * hd)
            intra, qd, kd, cd, diff, r = _ret_decays(lg_ref[d, h], d)
            qv, kv, vv, do = q_ref[:, cols], k_ref[:, cols], v_ref[:, cols], do_ref[:, cols]
            s_in = st_ref[h]
            ds_out = ds_sc[h]
            sc = _mxu_dot(qv, kv, _NT)
            p = sc * intra
            dp = _mxu_dot(do, vv, _NT)
            dsc = dp * intra
            dqs = _mxu_dot(do, s_in, _NT)
            dkk = _mxu_dot(vv, ds_out, _NT)
            dq_ref[:, cols] = _mxu_dot(dsc, kv) + dqs * qd
            dk_ref[:, cols] = _mxu_dot(dsc.T, qv) + dkk * kd
            dv_ref[:, cols] = _mxu_dot(p.T, do) + _mxu_dot(kv * kd, ds_out)
            ds_sc[h] = cd * ds_out + _mxu_dot((qv * qd).T, do)
            dlg = (_sum_all(dp * p * diff)
                   + _sum_all(jnp.sum(dqs * qv, axis=1, keepdims=True) * qd * (r + 1.0))
                   + _sum_all(jnp.sum(dkk * kv, axis=1, keepdims=True) * kd * (c - 1.0 - r))
                   + _sum_all(ds_out * s_in) * (cd * c))
            dlg_ref[h] += jnp.broadcast_to(dlg, (8, _LANES))

    def order(n, d):
        return _ret_chunk_order(n_all - 1 - n, d, n_ctx, n_all)

    def chunk_spec():
        return pl.BlockSpec((c, RET_DIM), lambda d, n: (order(n, d), 0))

    def dir_spec():
        return pl.BlockSpec((None, c, RET_DIM), lambda d, n: (d, order(n, d), 0))

    return pl.pallas_call(
        body, name=name + "_bwd",
        grid=(2, n_all),
        in_specs=[pl.BlockSpec(memory_space=pltpu.SMEM), chunk_spec(), chunk_spec(), chunk_spec(),
                  pl.BlockSpec((None, RET_HEADS, None, hd, hd), lambda d, n: (d, 0, n_all - 1 - n, 0, 0)),
                  dir_spec()],
        out_specs=[dir_spec(), dir_spec(), dir_spec(),
                   pl.BlockSpec((None, RET_HEADS, 8, _LANES), lambda d, n: (d, 0, 0, 0))],
        out_shape=[jax.ShapeDtypeStruct((2, s, RET_DIM), F32)] * 3
        + [jax.ShapeDtypeStruct((2, RET_HEADS, 8, _LANES), F32)],
        scratch_shapes=[pltpu.VMEM((RET_HEADS, hd, hd), F32)],
        compiler_params=_cparams(("arbitrary", "arbitrary")),
    )(lg, q, k, v, states, dout)


def _retention(n_ctx, name):
    @jax.custom_vjp
    def op(q, k, v, lg):
        return _ret_fwd(q, k, v, lg, n_ctx, name)[0]

    def fwd(q, k, v, lg):
        out, states = _ret_fwd(q, k, v, lg, n_ctx, name)
        return out, (q, k, v, lg, states)

    def bwd(res, dout):
        q, k, v, lg, states = res
        dq, dk, dv, dlg = _ret_bwd(q, k, v, lg, states, dout, n_ctx, name)
        return dq[0] + dq[1], dk[0] + dk[1], dv[0] + dv[1], dlg[:, :, 0, 0]

    op.defvjp(fwd, bwd)
    return op


def _loss_call(y, target, n_ctx_tiles, tile, name):
    s, dm = y.shape

    def body(y_ref, t_ref, loss_ref, dy_ref):
        i = pl.program_id(0)

        @pl.when(i == 0)
        def _():
            loss_ref[...] = jnp.zeros_like(loss_ref)

        @pl.when(i < n_ctx_tiles)
        def _():
            dy_ref[...] = jnp.zeros_like(dy_ref)

        @pl.when(i >= n_ctx_tiles)
        def _():
            err = y_ref[...] - t_ref[...]
            dy_ref[...] = err * (1.0 / dm)
            loss_ref[...] += jnp.broadcast_to(_sum_all(err * err) * (0.5 / dm), loss_ref.shape)

    return pl.pallas_call(
        body, name=name,
        grid=(s // tile,),
        in_specs=[pl.BlockSpec((tile, dm), lambda i: (i, 0)),
                  pl.BlockSpec((tile, dm), lambda i: (jnp.maximum(i - n_ctx_tiles, 0), 0))],
        out_specs=[pl.BlockSpec((8, _LANES), lambda i: (0, 0)),
                   pl.BlockSpec((tile, dm), lambda i: (i, 0))],
        out_shape=[jax.ShapeDtypeStruct((8, _LANES), F32), jax.ShapeDtypeStruct((s, dm), F32)],
        compiler_params=_cparams(("arbitrary",)),
    )(y, target)


def _loss_op(n_ctx_tiles, tile):
    @jax.custom_vjp
    def op(y, target):
        return _loss_call(y, target, n_ctx_tiles, tile, "loss_head")[0][0, 0]

    def fwd(y, target):
        loss, dy = _loss_call(y, target, n_ctx_tiles, tile, "loss_head")
        return loss[0, 0], (dy, target)

    def bwd(res, g):
        dy, target = res
        return dy * g, jnp.zeros_like(target)

    op.defvjp(fwd, bwd)
    return op


def _exchange(x, gather, name):
    blk_shape = x.shape if gather else x.shape[1:]

    def body(x_ref, o_ref, send_sems, recv_sems, local_sem):
        mx, my, mc = lax.axis_index("x"), lax.axis_index("y"), lax.axis_index("c")
        me = 4 * mx + 2 * my + mc
        copies = []
        for rel in range(1, _N_DEV):
            px = mx ^ ((rel >> 2) & 1)
            py = my ^ ((rel >> 1) & 1)
            pc = mc ^ (rel & 1)
            src = x_ref if gather else x_ref.at[4 * px + 2 * py + pc]
            cp = pltpu.make_async_remote_copy(
                src_ref=src, dst_ref=o_ref.at[me],
                send_sem=send_sems.at[rel - 1], recv_sem=recv_sems.at[rel - 1],
                device_id=(px, py, pc), device_id_type=pl.DeviceIdType.MESH)
            cp.start()
            copies.append(cp)
        mine = pltpu.make_async_copy(x_ref if gather else x_ref.at[me], o_ref.at[me], local_sem)
        mine.start()
        for cp in copies:
            cp.wait()
        mine.wait()

    return pl.pallas_call(
        body, name=name,
        in_specs=[pl.BlockSpec(memory_space=pltpu.HBM)],
        out_specs=pl.BlockSpec(memory_space=pltpu.HBM),
        out_shape=jax.ShapeDtypeStruct((_N_DEV,) + tuple(blk_shape), x.dtype),
        scratch_shapes=[pltpu.SemaphoreType.DMA((_N_DEV - 1,)), pltpu.SemaphoreType.DMA((_N_DEV - 1,)),
                        pltpu.SemaphoreType.DMA(())],
    )(x)


def _sum_parts(parts, name, out_dtype=F32):
    n, r, w = parts.shape
    tile = _pick(r, max(8, (1 << 19) // (w * n) // 16 * 16), 16)

    def body(p_ref, o_ref):
        acc = p_ref[0].astype(F32)
        for j in range(1, n):
            acc = acc + p_ref[j].astype(F32)
        o_ref[...] = acc.astype(o_ref.dtype)

    return pl.pallas_call(
        body, name=name, grid=(r // tile,),
        in_specs=[pl.BlockSpec((n, tile, w), lambda i: (0, i, 0))],
        out_specs=pl.BlockSpec((tile, w), lambda i: (i, 0)),
        out_shape=jax.ShapeDtypeStruct((r, w), out_dtype),
        compiler_params=_cparams(("arbitrary",)),
    )(parts)


def _chip_peers(mx, my):
    return [(1 - mx, my), (mx, 1 - my), (1 - mx, 1 - my)]


def _gather_two_level(xs, name):
    n_ops = len(xs)

    def body(*refs):
        x_refs, o_refs = refs[:n_ops], refs[n_ops:2 * n_ops]
        send_sems, recv_sems, local_sems = refs[2 * n_ops:]
        mx, my, mc = lax.axis_index("x"), lax.axis_index("y"), lax.axis_index("c")
        sibling = (mx, my, 1 - mc)
        chips = _chip_peers(mx, my)

        def copy(op, k, block, to, src=None):
            idx = 4 * block[0] + 2 * block[1] + block[2]
            dst = o_refs[op].at[idx]
            return pltpu.make_async_remote_copy(
                src_ref=dst if src is None else src, dst_ref=dst,
                send_sem=send_sems.at[op * 7 + k], recv_sem=recv_sems.at[op * 7 + k],
                device_id=to, device_id_type=pl.DeviceIdType.MESH)

        me = (mx, my, mc)
        mine, first, passed = [], [], []
        for op in range(n_ops):
            cp = pltpu.make_async_copy(x_refs[op], o_refs[op].at[4 * mx + 2 * my + mc], local_sems.at[op])
            cp.start()
            mine.append(cp)
            first.append(copy(op, 0, me, sibling, src=x_refs[op]))
            for j, chip in enumerate(chips):
                first.append(copy(op, 1 + j, me, (*chip, mc), src=x_refs[op]))
        for cp in first:
            cp.start()
        for j, chip in enumerate(chips):
            for op in range(n_ops):
                copy(op, 1 + j, (*chip, mc), me).wait_recv()
                cp = copy(op, 4 + j, (*chip, mc), sibling)
                cp.start()
                passed.append(cp)
        for op in range(n_ops):
            copy(op, 0, sibling, me).wait_recv()
            for j, chip in enumerate(chips):
                copy(op, 4 + j, (*chip, 1 - mc), me).wait_recv()
        for cp in first + passed:
            cp.wait_send()
        for cp in mine:
            cp.wait()

    hbm = pl.BlockSpec(memory_space=pltpu.HBM)
    return pl.pallas_call(
        body, name=name,
        in_specs=[hbm] * n_ops, out_specs=[hbm] * n_ops,
        out_shape=[jax.ShapeDtypeStruct((_N_DEV,) + tuple(x.shape), x.dtype) for x in xs],
        scratch_shapes=[pltpu.SemaphoreType.DMA((7 * n_ops,)), pltpu.SemaphoreType.DMA((7 * n_ops,)),
                        pltpu.SemaphoreType.DMA((n_ops,))],
    )(*xs)


def _swap_with_sibling(xs, name):
    n_ops = len(xs)

    def body(*refs):
        x_refs, o_refs = refs[:n_ops], refs[n_ops:2 * n_ops]
        send_sems, recv_sems = refs[2 * n_ops:]
        mx, my, mc = lax.axis_index("x"), lax.axis_index("y"), lax.axis_index("c")
        copies = []
        for op in range(n_ops):
            for q in range(4):
                rem = pltpu.make_async_remote_copy(
                    src_ref=x_refs[op].at[q], dst_ref=o_refs[op].at[q],
                    send_sem=send_sems.at[op * 4 + q], recv_sem=recv_sems.at[op * 4 + q],
                    device_id=(mx, my, 1 - mc), device_id_type=pl.DeviceIdType.MESH)
                rem.start()
                copies.append(rem)
        for cp in copies:
            cp.wait()

    hbm = pl.BlockSpec(memory_space=pltpu.HBM)
    return pl.pallas_call(
        body, name=name,
        in_specs=[hbm] * n_ops, out_specs=[hbm] * n_ops,
        out_shape=[jax.ShapeDtypeStruct(x.shape, x.dtype) for x in xs],
        scratch_shapes=[pltpu.SemaphoreType.DMA((4 * n_ops,)), pltpu.SemaphoreType.DMA((4 * n_ops,))],
    )(*xs)


def _add2(a, b, name, out_dtype):
    shape = a.shape
    a2, b2 = a.reshape(-1, shape[-1]), b.reshape(-1, shape[-1])
    r, w = a2.shape
    tile = _pick(r, max(16, (1 << 19) // w // 16 * 16), 16)

    def body(a_ref, b_ref, o_ref):
        o_ref[...] = (a_ref[...].astype(F32) + b_ref[...].astype(F32)).astype(o_ref.dtype)

    spec = pl.BlockSpec((tile, w), lambda i: (i, 0))
    return pl.pallas_call(
        body, name=name, grid=(r // tile,), in_specs=[spec, spec], out_specs=spec,
        out_shape=jax.ShapeDtypeStruct((r, w), out_dtype),
        compiler_params=_cparams(("arbitrary",)),
    )(a2, b2).reshape(shape)


def _scatter_chips(xs, name):
    n_ops = len(xs)

    def body(*refs):
        x_refs, o_refs = refs[:n_ops], refs[n_ops:2 * n_ops]
        send_sems, recv_sems, local_sems = refs[2 * n_ops:]
        mx, my, mc = lax.axis_index("x"), lax.axis_index("y"), lax.axis_index("c")
        my_chip = 2 * mx + my
        copies = []
        for op in range(n_ops):
            loc = pltpu.make_async_copy(x_refs[op].at[my_chip], o_refs[op].at[my_chip], local_sems.at[op])
            loc.start()
            copies.append(loc)
            for j, (px, py) in enumerate(_chip_peers(mx, my)):
                rem = pltpu.make_async_remote_copy(
                    src_ref=x_refs[op].at[2 * px + py], dst_ref=o_refs[op].at[my_chip],
                    send_sem=send_sems.at[op * 3 + j], recv_sem=recv_sems.at[op * 3 + j],
                    device_id=(px, py, mc), device_id_type=pl.DeviceIdType.MESH)
                rem.start()
                copies.append(rem)
        for cp in copies:
            cp.wait()

    hbm = pl.BlockSpec(memory_space=pltpu.HBM)
    return pl.pallas_call(
        body, name=name,
        in_specs=[hbm] * n_ops, out_specs=[hbm] * n_ops,
        out_shape=[jax.ShapeDtypeStruct(x.shape, x.dtype) for x in xs],
        scratch_shapes=[pltpu.SemaphoreType.DMA((3 * n_ops,)), pltpu.SemaphoreType.DMA((3 * n_ops,)),
                        pltpu.SemaphoreType.DMA((n_ops,))],
    )(*xs)


def _weights_gather_op(name):
    def impl(shards):
        got = _gather_two_level([s.astype(_MXU_DTYPE) for s in shards], name + "_gather")
        return tuple(g.astype(F32) for g in got)

    @jax.custom_vjp
    def op(shards):
        return impl(shards)

    def fwd(shards):
        return impl(shards), None

    def bwd(_, cts):
        mc = lax.axis_index("c")
        by_core = [jnp.swapaxes(g.astype(_MXU_DTYPE).reshape((4, 2) + g.shape[1:]), 0, 1) for g in cts]
        mine = [lax.dynamic_index_in_dim(p, mc, 0, keepdims=False) for p in by_core]
        theirs = [lax.dynamic_index_in_dim(p, 1 - mc, 0, keepdims=False) for p in by_core]
        got = _swap_with_sibling(theirs, name + "_scatter_pair")
        chip_sums = [_add2(a, b, "%s_pair_sum%d" % (name, k), _MXU_DTYPE) for k, (a, b) in enumerate(zip(mine, got))]
        crossed = _scatter_chips(chip_sums, name + "_scatter_chips")
        out = []
        for k, q in enumerate(crossed):
            flat = q.reshape(4, -1, q.shape[-1])
            out.append(_sum_parts(flat, "%s_chip_sum%d" % (name, k), F32).reshape(q.shape[1:]))
        return (tuple(out),)

    op.defvjp(fwd, bwd)
    return op


def _all_gather_op(name, payload_dtype):
    def impl(x):
        return _exchange(x.astype(payload_dtype), True, name + "_gather").astype(F32)

    @jax.custom_vjp
    def op(x):
        return impl(x)

    def fwd(x):
        return impl(x), None

    def bwd(_, g):
        return (_sum_parts(_exchange(g, False, name + "_scatter"), name + "_sum"),)

    op.defvjp(fwd, bwd)
    return op


def _adamw(w, g, m, v, partial, name):
    r, wd = w.shape
    tile = _pick(r, max(8, (1 << 20) // (4 * wd) // 8 * 8), 8)
    c1 = 1.0 / (1.0 - ADAM_B1 ** ADAM_STEP)
    c2 = 1.0 / (1.0 - ADAM_B2 ** ADAM_STEP)

    def body(w_ref, g_ref, m_ref, v_ref, go_ref, d_ref, mo_ref, vo_ref):
        if partial:
            g = g_ref[0]
            for j in range(1, _N_DEV):
                g = g + g_ref[j]
        else:
            g = g_ref[...]
        m_new = ADAM_B1 * m_ref[...] + (1.0 - ADAM_B1) * g
        v_new = ADAM_B2 * v_ref[...] + (1.0 - ADAM_B2) * (g * g)
        m_hat = m_new * c1
        v_hat = v_new * c2
        go_ref[...] = g
        d_ref[...] = -ADAM_LR * (m_hat / (jnp.sqrt(v_hat) + ADAM_EPS) + ADAM_WD * w_ref[...])
        mo_ref[...] = m_new
        vo_ref[...] = v_new

    spec = pl.BlockSpec((tile, wd), lambda i: (i, 0))
    g_spec = pl.BlockSpec((_N_DEV, tile, wd), lambda i: (0, i, 0)) if partial else spec
    return pl.pallas_call(
        body, name=name, grid=(r // tile,),
        in_specs=[spec, g_spec, spec, spec],
        out_specs=[spec] * 4,
        out_shape=[jax.ShapeDtypeStruct((r, wd), F32)] * 4,
        compiler_params=_cparams(("arbitrary",)),
    )(w, g, m, v)


def _pack(arrays):
    flat, meta, off = [], [], 0
    for a in arrays:
        n = int(np.prod(a.shape))
        pad = (-n) % _LANES
        flat.append(a.reshape(-1))
        if pad:
            flat.append(jnp.zeros((pad,), a.dtype))
        meta.append((off, a.shape))
        off += n + pad
    pad = (-off) % (8 * _LANES)
    if pad:
        flat.append(jnp.zeros((pad,), arrays[0].dtype))
    return jnp.concatenate(flat).reshape(-1, _LANES), meta


def _unpack(packed, meta):
    flat = packed.reshape(-1)
    return [flat[off:off + int(np.prod(shape))].reshape(shape) for off, shape in meta]


def _rope_tables(t, l, dim, width):
    rows = l // GRID_W
    r = np.repeat(np.arange(rows, dtype=np.float32), GRID_W)
    cc = np.tile(np.arange(GRID_W, dtype=np.float32), rows)
    n_freq = dim // 4
    inv = jnp.asarray(ROPE_THETA, F32) ** (-jnp.arange(n_freq, dtype=F32) / n_freq)
    ang_r = jnp.asarray(r)[:, None] * inv
    ang_c = jnp.asarray(cc)[:, None] * inv
    ang = jnp.concatenate([ang_r, ang_r, ang_c, ang_c], axis=-1)
    cos, sin = jnp.cos(ang), jnp.sin(ang)
    if width > dim:
        cos = jnp.concatenate([cos, jnp.ones((l, width - dim), F32)], axis=1)
        sin = jnp.concatenate([sin, jnp.zeros((l, width - dim), F32)], axis=1)
    cos = jnp.concatenate([jnp.ones((t, width), F32), cos], axis=0)
    sin = jnp.concatenate([jnp.zeros((t, width), F32), sin], axis=0)
    return cos, sin


def _full_weights(gathered):
    full = {}
    for n, g in zip(GATHERED, gathered):
        if GATHER_AXIS[n] == 0:
            full[n] = g.reshape(g.shape[0] * g.shape[1], g.shape[2])
        else:
            full[n] = jnp.transpose(g, (1, 0, 2)).reshape(g.shape[1], g.shape[0] * g.shape[2])
    return full


def _layer(l, stream, mod, wts, small, tables, dims, last):
    s, t, dm, dff = dims
    tile = min(256, t)
    n_ctx_tiles = t // tile
    tile_w = min(128, t)
    n_ctx_tiles_w = t // tile_w
    cos_h, sin_h, cos_m, sin_m = tables
    tag = "l%d_" % l

    def mrow(k):
        return mod[:, k:k + 1, :]

    def one(a):
        return a.reshape(1, 1, -1)

    w_in = jnp.concatenate([wts['w_in'], jnp.zeros((dm, IN_WIDTH_PAD - IN_WIDTH), F32)], axis=1)
    modulate = _rowwise(_fn_modulate, tag + "modulate", 1, [[(0, dm)]], [True], [True, True], [dm],
                        tile, n_ctx_tiles, mxu_outs=(0,))
    (h,) = modulate((stream,), (mrow(0), mrow(1)))
    proj = _linear(tag + "w_in")(h, w_in)

    in_parts = [(0, 512), (512, 512), (1024, 512), (1536, 512), (2048, 768), (2816, 256), (3072, 256),
                (3328, 512), (3840, 256), (4096, 128)]
    postproj = _rowwise(_fn_postproj, tag + "postproj", 5,
                        [in_parts, [(0, 128)], [(0, 128)], [(0, 128)], [(0, 128)]],
                        [True, False, False, False, False], [True, True],
                        [512, 512, 512, 512, 768, 256, 256, 512, 256, 128], tile_w, n_ctx_tiles_w,
                        mxu_outs=(4, 5, 6, 7, 8))
    (rq, rk, rv, rg, sq, sk, sv, cqn, ckvn, kr) = postproj(
        (proj, cos_h, sin_h, cos_m, sin_m), (one(small['mla_q_norm'][l]), one(small['mla_kv_norm'][l])))

    lg = jnp.stack([jax.nn.log_sigmoid(small['ret_decay_fwd'][l]), jax.nn.log_sigmoid(small['ret_decay_bwd'][l])])
    ret = _retention(t // RET_CHUNK, tag + "retention")(rq, rk, rv, lg)

    swa_blk = 256 if (t % 256 == 0 and s % 256 == 0) else 128
    swa_cfg = _AttnCfg(tag + "swa", s, t, SWA_HEADS, SWA_KV_HEADS, HEAD_DIM, HEAD_DIM, swa_blk, True,
                       HEAD_DIM ** -0.5, True)
    y_swa = _attention(swa_cfg)(sq, sk, sv, small['swa_sink'][l])

    w_uq = wts['mla_w_uq'].reshape(MLA_Q_RANK, MLA_HEADS, MLA_NOPE + MLA_ROPE)
    w_uq = jnp.concatenate([w_uq, jnp.zeros((MLA_Q_RANK, MLA_HEADS, 256 - MLA_NOPE - MLA_ROPE), F32)], axis=2)
    w_uq = w_uq.reshape(MLA_Q_RANK, MLA_HEADS * 256)
    w_ukv = wts['mla_w_ukv'].reshape(MLA_KV_RANK, MLA_HEADS, MLA_NOPE + MLA_V)
    w_ukv = jnp.concatenate([w_ukv[:, :, :MLA_NOPE].reshape(MLA_KV_RANK, -1),
                             w_ukv[:, :, MLA_NOPE:].reshape(MLA_KV_RANK, -1)], axis=1)
    q_lin = _linear(tag + "w_uq")(cqn, w_uq)
    kv_lin = _linear(tag + "w_ukv")(ckvn, w_ukv)
    kv_parts = [(hh * 128, 128) for hh in range(MLA_HEADS)] + [(MLA_HEADS * 128, MLA_HEADS * 128)]
    assemble = _rowwise(_fn_mla_assemble, tag + "mla_assemble", 5,
                        [[(0, MLA_HEADS * 256)], kv_parts, [(0, 128)], [(0, 128)], [(0, 128)]],
                        [True, True, True, False, False], [],
                        [MLA_HEADS * 256, MLA_HEADS * 256, MLA_HEADS * 128], tile_w, n_ctx_tiles_w,
                        mxu_outs=(0, 1, 2))
    (q_full, k_full, v_mla) = assemble((q_lin, kv_lin, kr, cos_m, sin_m), ())
    mla_cfg = _AttnCfg(tag + "mla", s, t, MLA_HEADS, MLA_HEADS, 256, MLA_V, _pick(s, 768, 128), False,
                       MLA_SCALE, False)
    y_mla = _attention(mla_cfg)(q_full, k_full, v_mla, jnp.zeros((MLA_HEADS,), F32))

    hparts = [(hh * 128, 128) for hh in range(RET_HEADS)]
    mix_op = _rowwise(_fn_mix, tag + "mix", 5, [hparts, hparts, hparts, [(0, 768)], [(0, 768)]],
                      [True] * 5, [], [dm_mix()], tile, n_ctx_tiles, mxu_outs=(0,))
    (mix_in,) = mix_op((ret[0], ret[1], rg, y_swa, y_mla), ())
    mix = _linear(tag + "w_o")(mix_in, wts['w_o'])
    ln1 = _rowwise(_fn_ln1, tag + "ln1", 2, [[(0, dm)], [(0, dm)]], [True, True], [True] * 5, [dm, dm],
                   tile, n_ctx_tiles, mxu_outs=(1,))
    x_a, h2 = ln1((stream, mix), (mrow(2), one(small['ln1_g'][l]), one(small['ln1_b'][l]), mrow(3), mrow(4)))
    u_lin = _linear(tag + "w_up_u")(h2, wts['ffn_w_up'][:, :dff])
    g_lin = _linear(tag + "w_up_g")(h2, wts['ffn_w_up'][:, dff:])
    conv_par = jnp.concatenate([wts['ffn_conv_w'], small['ffn_conv_b'][l][None, :], jnp.zeros((4, dff), F32)], axis=0)
    y = _conv_gate(t, tag + "conv_gate")(u_lin, g_lin, conv_par)
    f = _linear(tag + "w_down")(y, wts['ffn_w_down'])
    ln2 = _rowwise(_fn_ln2, tag + "ln2", 2, [[(0, dm)], [(0, dm)]], [True, True], [True] * 3, [dm],
                   tile, n_ctx_tiles)
    (out,) = ln2((x_a, f), (mrow(5), one(small['ln2_g'][l]), one(small['ln2_b'][l])))
    return out


def dm_mix():
    return RET_DIM + SWA_HEADS * HEAD_DIM + MLA_HEADS * MLA_V


def kernel(x, c, ctx, c_ctx, ada_w, ada_b, w_in, ret_decay_fwd, ret_decay_bwd, swa_sink, mla_q_norm, mla_w_uq, mla_kv_norm, mla_w_ukv, w_o, ln1_g, ln1_b, ffn_w_up, ffn_conv_w, ffn_conv_b, ffn_w_down, ln2_g, ln2_b, loss_target, m_c_ctx, m_ada_w, m_ada_b, m_w_in, m_ret_decay_fwd, m_ret_decay_bwd, m_swa_sink, m_mla_q_norm, m_mla_w_uq, m_mla_kv_norm, m_mla_w_ukv, m_w_o, m_ln1_g, m_ln1_b, m_ffn_w_up, m_ffn_conv_w, m_ffn_conv_b, m_ffn_w_down, m_ln2_g, m_ln2_b, v_c_ctx, v_ada_w, v_ada_b, v_w_in, v_ret_decay_fwd, v_ret_decay_bwd, v_swa_sink, v_mla_q_norm, v_mla_w_uq, v_mla_kv_norm, v_mla_w_ukv, v_w_o, v_ln1_g, v_ln1_b, v_ffn_w_up, v_ffn_conv_w, v_ffn_conv_b, v_ffn_w_down, v_ln2_g, v_ln2_b):
    weights = dict(c_ctx=c_ctx, ada_w=ada_w, ada_b=ada_b, w_in=w_in, ret_decay_fwd=ret_decay_fwd,
                   ret_decay_bwd=ret_decay_bwd, swa_sink=swa_sink, mla_q_norm=mla_q_norm, mla_w_uq=mla_w_uq,
                   mla_kv_norm=mla_kv_norm, mla_w_ukv=mla_w_ukv, w_o=w_o, ln1_g=ln1_g, ln1_b=ln1_b,
                   ffn_w_up=ffn_w_up, ffn_conv_w=ffn_conv_w, ffn_conv_b=ffn_conv_b, ffn_w_down=ffn_w_down,
                   ln2_g=ln2_g, ln2_b=ln2_b)
    m_in = dict(c_ctx=m_c_ctx, ada_w=m_ada_w, ada_b=m_ada_b, w_in=m_w_in, ret_decay_fwd=m_ret_decay_fwd,
                ret_decay_bwd=m_ret_decay_bwd, swa_sink=m_swa_sink, mla_q_norm=m_mla_q_norm, mla_w_uq=m_mla_w_uq,
                mla_kv_norm=m_mla_kv_norm, mla_w_ukv=m_mla_w_ukv, w_o=m_w_o, ln1_g=m_ln1_g, ln1_b=m_ln1_b,
                ffn_w_up=m_ffn_w_up, ffn_conv_w=m_ffn_conv_w, ffn_conv_b=m_ffn_conv_b, ffn_w_down=m_ffn_w_down,
                ln2_g=m_ln2_g, ln2_b=m_ln2_b)
    v_in = dict(c_ctx=v_c_ctx, ada_w=v_ada_w, ada_b=v_ada_b, w_in=v_w_in, ret_decay_fwd=v_ret_decay_fwd,
                ret_decay_bwd=v_ret_decay_bwd, swa_sink=v_swa_sink, mla_q_norm=v_mla_q_norm, mla_w_uq=v_mla_w_uq,
                mla_kv_norm=v_mla_kv_norm, mla_w_ukv=v_mla_w_ukv, w_o=v_w_o, ln1_g=v_ln1_g, ln1_b=v_ln1_b,
                ffn_w_up=v_ffn_w_up, ffn_conv_w=v_ffn_conv_w, ffn_conv_b=v_ffn_conv_b, ffn_w_down=v_ffn_w_down,
                ln2_g=v_ln2_g, ln2_b=v_ln2_b)

    l_tok, dm = x.shape[1], x.shape[2]
    t = ctx.shape[1]
    s = t + l_tok
    dff = ffn_w_down.shape[1] * _N_DEV
    dims = (s, t, dm, dff)
    me = 4 * lax.axis_index("x") + 2 * lax.axis_index("y") + lax.axis_index("c")
    cos_h, sin_h = _rope_tables(t, l_tok, HEAD_DIM, HEAD_DIM)
    cos_m, sin_m = _rope_tables(t, l_tok, MLA_ROPE, _LANES)
    tables = (cos_h, sin_h, cos_m, sin_m)
    c_all = _exchange(c, True, "gather_cond").reshape(_N_DEV, dm)
    tile = min(256, t)

    def loss_fn(wd, xin):
        mod_rows = jnp.concatenate([jax.nn.silu(c_all), jax.nn.silu(wd['c_ctx'])[None, :],
                                  jnp.zeros((_MOD_ROWS - _N_DEV - 1, dm), F32)], axis=0)
        used = 16
        mods_shard = jnp.stack([_linear("ada_l%d" % l)(mod_rows, wd['ada_w'][l])[:used] for l in range(DEPTH)])
        n_sh = mods_shard.shape[-1]
        mods_all = _all_gather_op("mods", F32)(mods_shard.reshape(DEPTH * used, n_sh))
        mods_all = mods_all.reshape(_N_DEV, DEPTH, used, n_sh).transpose(1, 2, 0, 3).reshape(DEPTH, used, _N_DEV * n_sh)
        mods_all = mods_all + wd['ada_b'][:, None, :]
        mod_x = lax.dynamic_slice_in_dim(mods_all, me, 1, axis=1)[:, 0]
        mod_c = mods_all[:, _N_DEV]
        stream = jnp.concatenate([ctx[0], xin[0]], axis=0)
        for l in range(DEPTH):
            gathered = _weights_gather_op("weights_l%d" % l)(tuple(wd[n][l] for n in GATHERED))
            full = _full_weights(gathered)
            mod = jnp.stack([mod_c[l].reshape(N_MOD, dm), mod_x[l].reshape(N_MOD, dm)])
            stream = _layer(l, stream, mod, full, wd, tables, dims, l == DEPTH - 1)
        return _loss_op(t // tile, tile)(stream, loss_target[0])

    loss_local, (gw, gx) = jax.value_and_grad(loss_fn, argnums=(0, 1))(weights, x)
    loss = lax.psum(loss_local, ("x", "y", "c"))

    grads, deltas, new_m, new_v = {}, {}, {}, {}

    def as2d(a):
        return a.reshape(-1, a.shape[-1])

    for n in ['ada_w'] + GATHERED:
        g2, d2, m2, v2 = _adamw(as2d(weights[n]), as2d(gw[n]), as2d(m_in[n]), as2d(v_in[n]), False, "adamw_" + n)
        shp = weights[n].shape
        grads[n], deltas[n], new_m[n], new_v[n] = g2.reshape(shp), d2.reshape(shp), m2.reshape(shp), v2.reshape(shp)

    w_pack, meta = _pack([weights[n] for n in REPLICATED])
    g_pack, _ = _pack([gw[n] for n in REPLICATED])
    m_pack, _ = _pack([m_in[n] for n in REPLICATED])
    v_pack, _ = _pack([v_in[n] for n in REPLICATED])
    g_parts = _exchange(g_pack, True, "gather_small_grads")
    outs = _adamw(w_pack, g_parts, m_pack, v_pack, True, "adamw_replicated")
    for dst, packed in zip((grads, deltas, new_m, new_v), outs):
        for n, a in zip(REPLICATED, _unpack(packed, meta)):
            dst[n] = a

    return (loss, gx, *[grads[n] for n in WEIGHTS], *[deltas[n] for n in WEIGHTS],
            *[new_m[n] for n in WEIGHTS], *[new_v[n] for n in WEIGHTS])
```

```python
import functools
import math

import numpy as np
import jax
import jax.numpy as jnp
from jax import lax
from jax.experimental import pallas as pl
from jax.experimental.pallas import tpu as pltpu

F32 = jnp.float32
_MXU_DTYPE = jnp.bfloat16
_VMEM_LIMIT_BYTES = 56 * 1024 * 1024
_LANES = 128
_N_DEV = 8
_MOD_ROWS = 128

DEPTH = 4
HEAD_DIM = 128
ROPE_THETA = 10000.0
GRID_W = 64
RET_HEADS = 4
RET_DIM = RET_HEADS * HEAD_DIM
RET_CHUNK = 128
SWA_HEADS = 6
SWA_KV_HEADS = 2
SWA_WINDOW = 128
MLA_HEADS = 6
MLA_Q_RANK = 512
MLA_KV_RANK = 256
MLA_NOPE = 128
MLA_ROPE = 64
MLA_V = 128
MLA_SCALE = (MLA_NOPE + MLA_ROPE) ** -0.5
_LOG2E = math.log2(math.e)
N_MOD = 6
LN_EPS = 1e-5
RMS_EPS = 1e-6
NEG_INF = -1e30
ALPHA = (2 * DEPTH) ** 0.25
IN_WIDTH = 4160
IN_WIDTH_PAD = 4224

ADAM_LR = 0.001
ADAM_B1 = 0.9
ADAM_B2 = 0.999
ADAM_EPS = 1e-08
ADAM_WD = 0.01
ADAM_STEP = 10

WEIGHTS = ['c_ctx', 'ada_w', 'ada_b', 'w_in', 'ret_decay_fwd', 'ret_decay_bwd', 'swa_sink', 'mla_q_norm',
           'mla_w_uq', 'mla_kv_norm', 'mla_w_ukv', 'w_o', 'ln1_g', 'ln1_b', 'ffn_w_up', 'ffn_conv_w',
           'ffn_conv_b', 'ffn_w_down', 'ln2_g', 'ln2_b']
GATHERED = ['w_in', 'mla_w_uq', 'mla_w_ukv', 'w_o', 'ffn_w_up', 'ffn_conv_w', 'ffn_w_down']
GATHER_AXIS = {'w_in': 1, 'mla_w_uq': 1, 'mla_w_ukv': 1, 'w_o': 0, 'ffn_w_up': 1, 'ffn_conv_w': 1, 'ffn_w_down': 0}
REPLICATED = ['c_ctx', 'ada_b', 'ret_decay_fwd', 'ret_decay_bwd', 'swa_sink', 'mla_q_norm', 'mla_kv_norm',
              'ln1_g', 'ln1_b', 'ffn_conv_b', 'ln2_g', 'ln2_b']


def _cparams(semantics):
    return pltpu.CompilerParams(dimension_semantics=semantics, vmem_limit_bytes=_VMEM_LIMIT_BYTES)


def _pick(n, target, align):
    best = None
    d = align
    while d <= min(n, target):
        if n % d == 0:
            best = d
        d += align
    return n if best is None else best


def _matmul(a, b, mode, name, out_dtype=F32):
    if mode == 'nn':
        (m, k), (k2, n) = a.shape, b.shape
    elif mode == 'nt':
        (m, k), (n, k2) = a.shape, b.shape
    else:
        (k, m), (k2, n) = a.shape, b.shape
    assert k == k2, (a.shape, b.shape, mode)
    tm = _pick(m, 1024, 128)
    tn = _pick(n, 1408, 128)
    tk = _pick(k, 1408 if mode == 'tn' else 2048, 128)
    nk = k // tk

    def body(a_ref, b_ref, o_ref, *scratch):
        kk = pl.program_id(2)
        if mode == 'nn':
            part = jnp.dot(a_ref[...].astype(_MXU_DTYPE), b_ref[...].astype(_MXU_DTYPE),
                           preferred_element_type=F32)
        elif mode == 'nt':
            part = lax.dot_general(a_ref[...].astype(_MXU_DTYPE), b_ref[...].astype(_MXU_DTYPE),
                                   (((1,), (1,)), ((), ())), preferred_element_type=F32)
        else:
            part = lax.dot_general(a_ref[...].astype(_MXU_DTYPE), b_ref[...].astype(_MXU_DTYPE),
                                   (((0,), (0,)), ((), ())), preferred_element_type=F32)
        if nk == 1:
            o_ref[...] = part.astype(o_ref.dtype)
            return
        acc_ref, = scratch

        @pl.when(kk == 0)
        def _():
            acc_ref[...] = part

        @pl.when((kk > 0) & (kk < nk - 1))
        def _():
            acc_ref[...] += part

        @pl.when(kk == nk - 1)
        def _():
            o_ref[...] = (acc_ref[...] + part).astype(o_ref.dtype)

    if mode == 'nn':
        a_spec = pl.BlockSpec((tm, tk), lambda i, j, kk: (i, kk))
        b_spec = pl.BlockSpec((tk, tn), lambda i, j, kk: (kk, j))
    elif mode == 'nt':
        a_spec = pl.BlockSpec((tm, tk), lambda i, j, kk: (i, kk))
        b_spec = pl.BlockSpec((tn, tk), lambda i, j, kk: (j, kk))
    else:
        a_spec = pl.BlockSpec((tk, tm), lambda i, j, kk: (kk, i))
        b_spec = pl.BlockSpec((tk, tn), lambda i, j, kk: (kk, j))
    return pl.pallas_call(
        body, name=name,
        grid=(m // tm, n // tn, nk),
        in_specs=[a_spec, b_spec],
        out_specs=pl.BlockSpec((tm, tn), lambda i, j, kk: (i, j)),
        out_shape=jax.ShapeDtypeStruct((m, n), out_dtype),
        scratch_shapes=[pltpu.VMEM((tm, tn), F32)] if nk > 1 else [],
        compiler_params=_cparams(("parallel", "parallel", "arbitrary")),
    )(a, b)


def _linear(name, out_dtype=F32):
    @jax.custom_vjp
    def op(a, w):
        return _matmul(a.astype(_MXU_DTYPE), w.astype(_MXU_DTYPE), 'nn', name + "_fwd", out_dtype)

    def fwd(a, w):
        ab, wb = a.astype(_MXU_DTYPE), w.astype(_MXU_DTYPE)
        return _matmul(ab, wb, 'nn', name + "_fwd", out_dtype), (ab, wb.T, jnp.zeros((0,), a.dtype))

    def bwd(res, g):
        ab, wbt, a_like = res
        gb = g.astype(_MXU_DTYPE)
        da = _matmul(gb, wbt, 'nn', name + "_da", a_like.dtype)
        dw = _matmul(ab, gb, 'tn', name + "_dw")
        return da, dw

    op.defvjp(fwd, bwd)
    return op


def _pieces(parts):
    out = []
    for p, (start, width) in enumerate(parts):
        pw = math.gcd(start, width) if start else width
        assert pw % _LANES == 0, (start, width)
        for t in range(width // pw):
            out.append((p, pw, start // pw + t))
    return out


def _rowwise(fn, name, rows, parts, diff, pdiff, out_widths, tile, n_ctx_tiles, col_tile=None, mxu_outs=()):
    pieces = [_pieces(p) for p in parts]

    def sel_of(i, n_sel):
        return jnp.where(i >= n_ctx_tiles, n_sel - 1, 0)

    def in_specs_for(row_arrays, params):
        specs, operands = [], []
        for r in range(rows):
            for (_, pw, blk) in pieces[r]:
                if col_tile is None:
                    specs.append(pl.BlockSpec((tile, pw), lambda j, i, blk=blk: (i, blk)))
                else:
                    nb = pw // col_tile
                    specs.append(pl.BlockSpec((tile, col_tile), lambda j, i, blk=blk, nb=nb: (i, blk * nb + j)))
                operands.append(row_arrays[r])
        for p in params:
            n_sel, _, w = p.shape
            cw = w if col_tile is None else col_tile
            if col_tile is None:
                specs.append(pl.BlockSpec((None, 1, cw), lambda j, i, n_sel=n_sel: (sel_of(i, n_sel), 0, 0)))
            else:
                specs.append(pl.BlockSpec((None, 1, cw), lambda j, i, n_sel=n_sel: (sel_of(i, n_sel), 0, j)))
            operands.append(p)
        return specs, operands

    def load_inputs(refs):
        k = 0
        vals = []
        for r in range(rows):
            got = [[] for _ in parts[r]]
            for (p, _, _) in pieces[r]:
                got[p].append(refs[k][...].astype(F32))
                k += 1
            vals.append([g[0] if len(g) == 1 else jnp.concatenate(g, axis=1) for g in got])
        return vals, k

    def forward(row_arrays, params):
        s = row_arrays[0].shape[0]
        ncol = 1 if col_tile is None else out_widths[0] // col_tile
        n_par = len(params)

        def body(*refs):
            vals, k = load_inputs(refs)
            pvals = [refs[k + q][...].astype(F32) for q in range(n_par)]
            outs = fn(vals, pvals)
            for o_ref, o in zip(refs[k + n_par:], outs):
                o_ref[...] = o.astype(o_ref.dtype)

        specs, operands = in_specs_for(row_arrays, params)
        if col_tile is None:
            out_specs = [pl.BlockSpec((tile, w), lambda j, i: (i, 0)) for w in out_widths]
        else:
            out_specs = [pl.BlockSpec((tile, col_tile), lambda j, i: (i, j)) for _ in out_widths]
        return pl.pallas_call(
            body, name=name + "_fwd",
            grid=(ncol, s // tile),
            in_specs=specs, out_specs=out_specs,
            out_shape=[jax.ShapeDtypeStruct((s, w), _MXU_DTYPE if o in mxu_outs else F32)
                       for o, w in enumerate(out_widths)],
            compiler_params=_cparams(("arbitrary", "arbitrary")),
        )(*operands)

    def backward(row_arrays, params, cts):
        s = row_arrays[0].shape[0]
        ncol = 1 if col_tile is None else out_widths[0] // col_tile
        n_par = len(params)
        n_out = len(out_widths)
        d_rows = [r for r in range(rows) if diff[r]]
        d_pars = [q for q in range(n_par) if pdiff[q]]

        def body(*refs):
            i = pl.program_id(1)
            vals, k = load_inputs(refs)
            pvals = [refs[k + q][...].astype(F32) for q in range(n_par)]
            k += n_par
            ct_vals = [refs[k + o][...].astype(F32) for o in range(n_out)]
            k += n_out
            drow_refs = refs[k:k + len(d_rows)]
            dpar_refs = refs[k + len(d_rows):]

            def f(dv, dp):
                full_v = list(vals)
                for r, v in zip(d_rows, dv):
                    full_v[r] = v
                full_p = list(pvals)
                for q, v in zip(d_pars, dp):
                    full_p[q] = v
                return fn(full_v, full_p)

            _, vjp = jax.vjp(f, [vals[r] for r in d_rows], [pvals[q] for q in d_pars])
            g_rows, g_pars = vjp(ct_vals)
            for ref, r, g in zip(drow_refs, d_rows, g_rows):
                covered = sum(w for (_, w) in parts[r])
                if col_tile is None:
                    if covered != ref.shape[1]:
                        ref[...] = jnp.zeros_like(ref)
                    for (start, width), gp in zip(parts[r], g):
                        ref[:, start:start + width] = gp
                else:
                    ref[...] = g[0]
            for ref, q, g in zip(dpar_refs, d_pars, g_pars):
                n_sel = params[q].shape[0]
                first = (i == 0) if n_sel == 1 else ((i == 0) | (i == n_ctx_tiles))

                @pl.when(first)
                def _(ref=ref):
                    ref[...] = jnp.zeros_like(ref)

                ref[...] += g

        specs, operands = in_specs_for(row_arrays, params)
        for o, w in enumerate(out_widths):
            if col_tile is None:
                specs.append(pl.BlockSpec((tile, w), lambda j, i: (i, 0)))
            else:
                specs.append(pl.BlockSpec((tile, col_tile), lambda j, i: (i, j)))
            operands.append(cts[o])
        out_specs, out_shape = [], []
        for r in d_rows:
            w = row_arrays[r].shape[1]
            if col_tile is None:
                out_specs.append(pl.BlockSpec((tile, w), lambda j, i: (i, 0)))
            else:
                assert len(parts[r]) == 1 and parts[r][0] == (0, w)
                out_specs.append(pl.BlockSpec((tile, col_tile), lambda j, i: (i, j)))
            out_shape.append(jax.ShapeDtypeStruct((s, w), F32))
        for q in d_pars:
            n_sel, _, w = params[q].shape
            cw = w if col_tile is None else col_tile
            if col_tile is None:
                out_specs.append(pl.BlockSpec((None, 1, cw), lambda j, i, n_sel=n_sel: (sel_of(i, n_sel), 0, 0)))
            else:
                out_specs.append(pl.BlockSpec((None, 1, cw), lambda j, i, n_sel=n_sel: (sel_of(i, n_sel), 0, j)))
            out_shape.append(jax.ShapeDtypeStruct((n_sel, 1, w), F32))
        res = pl.pallas_call(
            body, name=name + "_bwd",
            grid=(ncol, s // tile),
            in_specs=specs, out_specs=out_specs, out_shape=out_shape,
            compiler_params=_cparams(("arbitrary", "arbitrary")),
        )(*operands)
        g_rows = [None] * rows
        for r, g in zip(d_rows, res[:len(d_rows)]):
            g_rows[r] = g
        g_pars = [None] * n_par
        for q, g in zip(d_pars, res[len(d_rows):]):
            g_pars[q] = g
        return g_rows, g_pars

    @jax.custom_vjp
    def op(row_arrays, params):
        return tuple(forward(list(row_arrays), list(params)))

    def op_fwd(row_arrays, params):
        return tuple(forward(list(row_arrays), list(params))), (row_arrays, params)

    def op_bwd(res, cts):
        row_arrays, params = res
        g_rows, g_pars = backward(list(row_arrays), list(params), list(cts))
        g_rows = tuple(jnp.zeros_like(a) if g is None else g for a, g in zip(row_arrays, g_rows))
        g_pars = tuple(jnp.zeros_like(a) if g is None else g for a, g in zip(params, g_pars))
        return g_rows, g_pars

    op.defvjp(op_fwd, op_bwd)
    return op


def _rot_impl(x, quarter):
    lane = lax.broadcasted_iota(jnp.int32, (x.shape[0], _LANES), 1)
    even = ((lane // quarter) % 2) == 0
    outs = []
    for k in range(x.shape[1] // _LANES):
        xs = x[:, k * _LANES:(k + 1) * _LANES]
        left = pltpu.roll(xs, _LANES - quarter, 1)
        right = pltpu.roll(xs, quarter, 1)
        outs.append(jnp.where(even, -left, right))
    return outs[0] if len(outs) == 1 else jnp.concatenate(outs, axis=1)


def _make_rot(quarter):
    @jax.custom_vjp
    def rot(x):
        return _rot_impl(x, quarter)

    rot.defvjp(lambda x: (_rot_impl(x, quarter), None), lambda _, g: (-_rot_impl(g, quarter),))
    return rot


_rot32 = _make_rot(32)
_rot16 = _make_rot(16)


def _tile_lanes(t, n):
    return t if n == 1 else jnp.concatenate([t] * n, axis=1)


def _rope(x, cos, sin, rot):
    n = x.shape[1] // _LANES
    return x * _tile_lanes(cos, n) + rot(x) * _tile_lanes(sin, n)


def _rms(x):
    return x * lax.rsqrt(jnp.mean(x * x, axis=-1, keepdims=True) + RMS_EPS)


def _ln(x):
    mu = jnp.mean(x, axis=-1, keepdims=True)
    xc = x - mu
    var = jnp.mean(xc * xc, axis=-1, keepdims=True)
    return xc * lax.rsqrt(var + LN_EPS)


def _sum_all(x):
    return jnp.sum(jnp.sum(x, axis=1, keepdims=True), axis=0, keepdims=True)


def _silu(x):
    return x * (1.0 / (1.0 + jnp.exp(-x)))


def _fn_modulate(vals, pars):
    (s,), = vals
    shift, scale = pars
    return [s * (1.0 + scale) + shift]


def _fn_postproj(vals, pars):
    (rq, rk, rv, rg, sq, sk, sv, mcq, mckv, mkr), (cos_h,), (sin_h,), (cos_m,), (sin_m,) = vals
    q_norm, kv_norm = pars
    k_scale = HEAD_DIM ** -0.5
    return [_rope(rq, cos_h, sin_h, _rot32), _rope(rk, cos_h, sin_h, _rot32) * k_scale, rv, rg,
            _rope(sq, cos_h, sin_h, _rot32) * (k_scale * _LOG2E), _rope(sk, cos_h, sin_h, _rot32), sv,
            _rms(mcq) * q_norm, _rms(mckv) * kv_norm, _rope(mkr, cos_m, sin_m, _rot16)]


def _fn_mla_assemble(vals, pars):
    (q_lin,), kn_v, (kr,), (cos_m,), (sin_m,) = vals
    kn, vv = kn_v[:MLA_HEADS], kn_v[MLA_HEADS]
    ones, zeros = jnp.ones_like(cos_m), jnp.zeros_like(sin_m)
    cos_q = jnp.concatenate([ones, cos_m] * MLA_HEADS, axis=1)
    sin_q = jnp.concatenate([zeros, sin_m] * MLA_HEADS, axis=1)
    q_full = (q_lin * cos_q + _rot16(q_lin) * sin_q) * (MLA_SCALE * _LOG2E)
    k_full = jnp.concatenate([t for h in range(MLA_HEADS) for t in (kn[h], kr)], axis=1)
    return [q_full, k_full, vv]


def _fn_mix(vals, pars):
    ret_f, ret_b, rg, (y_swa,), (y_mla,) = vals
    heads = [_silu(rg[h]) * _rms(ret_f[h] + ret_b[h]) for h in range(RET_HEADS)]
    return [jnp.concatenate(heads + [y_swa, y_mla], axis=1)]


def _fn_ln1(vals, pars):
    (s,), (mix,) = vals
    gate, g, b, shift_f, scale_f = pars
    x_a = _ln(ALPHA * s + (1.0 + gate) * mix) * g + b
    return [x_a, x_a * (1.0 + scale_f) + shift_f]


def _fn_ln2(vals, pars):
    (x_a,), (f,) = vals
    gate, g, b = pars
    return [_ln(ALPHA * x_a + (1.0 + gate) * f) * g + b]


def _conv_tiles(s, t, f):
    r = min(256, t)
    assert t % r == 0 and s % r == 0 and r % 8 == 0
    return r, _pick(f, 1408, _LANES)


def _conv_gate_fwd(u, g, par, t, name):
    s, f = u.shape
    r, cw = _conv_tiles(s, t, f)
    halo = 16
    assert r % halo == 0
    n_ctx, n_tiles, per = t // r, s // r, r // halo

    def body(u_ref, g_ref, gp_ref, gn_ref, p_ref, y_ref):
        i = pl.program_id(1)
        gv = g_ref[...].astype(F32)
        row = lax.broadcasted_iota(jnp.int32, (r, 1), 0)
        seg_start = (i == 0) | (i == n_ctx)
        seg_end = (i == n_ctx - 1) | (i == n_tiles - 1)
        prev_row = jnp.where(seg_start, 0.0, gp_ref[halo - 1:halo, :].astype(F32))
        next_row = jnp.where(seg_end, 0.0, gn_ref[0:1, :].astype(F32))
        gp = jnp.where(row == 0, prev_row, pltpu.roll(gv, 1, 0))
        gn = jnp.where(row == r - 1, next_row, pltpu.roll(gv, r - 1, 0))
        gc = p_ref[0:1, :] * gp + p_ref[1:2, :] * gv + p_ref[2:3, :] * gn + p_ref[3:4, :]
        y_ref[...] = (_silu(gc) * u_ref[...].astype(F32)).astype(y_ref.dtype)

    tile = pl.BlockSpec((r, cw), lambda j, i: (i, j))
    return pl.pallas_call(
        body, name=name + "_fwd",
        grid=(f // cw, n_tiles),
        in_specs=[tile, tile,
                  pl.BlockSpec((halo, cw), lambda j, i: (jnp.maximum(i * per - 1, 0), j)),
                  pl.BlockSpec((halo, cw), lambda j, i: (jnp.minimum((i + 1) * per, s // halo - 1), j)),
                  pl.BlockSpec((8, cw), lambda j, i: (0, j))],
        out_specs=tile,
        out_shape=jax.ShapeDtypeStruct((s, f), _MXU_DTYPE),
        compiler_params=_cparams(("arbitrary", "arbitrary")),
    )(u, g, g, g, par)


def _conv_gate_bwd(u, g, par, dy, t, name):
    s, f = u.shape
    r, cw = _conv_tiles(s, t, f)
    halo = 16
    assert r % halo == 0
    n_tiles, per = s // r, r // halo
    re = r + 2 * halo

    def body(u_ref, up_ref, un_ref, g_ref, gp_ref, gn_ref, dy_ref, dyp_ref, dyn_ref, p_ref,
             du_ref, dg_ref, dp_ref):
        i = pl.program_id(1)

        def ext(prev, cur, nxt):
            return jnp.concatenate([prev[...].astype(F32), cur[...].astype(F32), nxt[...].astype(F32)], axis=0)

        ge, ue, dye = ext(gp_ref, g_ref, gn_ref), ext(up_ref, u_ref, un_ref), ext(dyp_ref, dy_ref, dyn_ref)
        grow = i * r - halo + lax.broadcasted_iota(jnp.int32, (re, 1), 0)
        is_start = (grow == 0) | (grow == t)
        is_end = (grow == t - 1) | (grow == s - 1)
        inside = (grow >= 0) & (grow < s)
        w0, w1, w2, bias = p_ref[0:1, :], p_ref[1:2, :], p_ref[2:3, :], p_ref[3:4, :]
        gpe = jnp.where(is_start, 0.0, pltpu.roll(ge, 1, 0))
        gne = jnp.where(is_end, 0.0, pltpu.roll(ge, re - 1, 0))
        gce = w0 * gpe + w1 * ge + w2 * gne + bias
        sig = 1.0 / (1.0 + jnp.exp(-gce))
        dgce = jnp.where(inside, dye * ue * (sig * (1.0 + gce * (1.0 - sig))), 0.0)
        dge = (w1 * dgce + w0 * jnp.where(is_end, 0.0, pltpu.roll(dgce, re - 1, 0))
               + w2 * jnp.where(is_start, 0.0, pltpu.roll(dgce, 1, 0)))
        mid = slice(halo, r + halo)
        dg_ref[...] = dge[mid].astype(dg_ref.dtype)
        du_ref[...] = (dye * gce * sig)[mid].astype(du_ref.dtype)
        dgc = dgce[mid]

        @pl.when(i == 0)
        def _():
            dp_ref[...] = jnp.zeros_like(dp_ref)

        dp_ref[0:1, :] += jnp.sum(dgc * gpe[mid], axis=0, keepdims=True)
        dp_ref[1:2, :] += jnp.sum(dgc * ge[mid], axis=0, keepdims=True)
        dp_ref[2:3, :] += jnp.sum(dgc * gne[mid], axis=0, keepdims=True)
        dp_ref[3:4, :] += jnp.sum(dgc, axis=0, keepdims=True)

    tile = pl.BlockSpec((r, cw), lambda j, i: (i, j))
    prev = pl.BlockSpec((halo, cw), lambda j, i: (jnp.maximum(i * per - 1, 0), j))
    nxt = pl.BlockSpec((halo, cw), lambda j, i: (jnp.minimum((i + 1) * per, s // halo - 1), j))
    par_spec = pl.BlockSpec((8, cw), lambda j, i: (0, j))
    return pl.pallas_call(
        body, name=name + "_bwd",
        grid=(f // cw, n_tiles),
        in_specs=[tile, prev, nxt, tile, prev, nxt, tile, prev, nxt, par_spec],
        out_specs=[tile, tile, par_spec],
        out_shape=[jax.ShapeDtypeStruct((s, f), u.dtype), jax.ShapeDtypeStruct((s, f), g.dtype),
                   jax.ShapeDtypeStruct((8, f), F32)],
        compiler_params=_cparams(("arbitrary", "arbitrary")),
    )(u, u, u, g, g, g, dy, dy, dy, par)


def _conv_gate(t, name):
    @jax.custom_vjp
    def op(u, g, par):
        return _conv_gate_fwd(u, g, par, t, name)

    def fwd(u, g, par):
        return _conv_gate_fwd(u, g, par, t, name), (u, g, par)

    def bwd(res, dy):
        u, g, par = res
        return _conv_gate_bwd(u, g, par, dy, t, name)

    op.defvjp(fwd, bwd)
    return op


class _AttnCfg:
    def __init__(self, name, s, t, heads, kv_heads, dk, dv, blk, band, scale, has_sink):
        self.name, self.s, self.t = name, s, t
        self.heads, self.kv_heads, self.group = heads, kv_heads, heads // kv_heads
        self.dk, self.dv, self.blk, self.band, self.scale, self.has_sink = dk, dv, blk, band, scale, has_sink
        self.nq = s // blk
        self.n_ctx = t // blk if band else 0
        self.ks = self.n_ctx + 3 if band else s // blk
        assert s % blk == 0 and (not band or (t % blk == 0 and blk >= SWA_WINDOW))

    def kblock(self, i, st):
        if not self.band:
            return st
        kb = jnp.clip(i + st - self.n_ctx - 1, self.n_ctx, self.nq - 1)
        return jnp.where(st < self.n_ctx, st, kb)

    def valid(self, i, st):
        if not self.band:
            return st >= 0
        kb = i + st - self.n_ctx - 1
        return (st < self.n_ctx) | ((i >= self.n_ctx) & (kb >= self.n_ctx) & (kb <= self.nq - 1))

    def masked(self, i, st):
        if self.band:
            return st >= self.n_ctx
        return i * self.blk < self.t

    def visible(self, i, kb, keys_first=False):
        b = self.blk
        qpos = i * b + lax.broadcasted_iota(jnp.int32, (b, b), 1 if keys_first else 0)
        kpos = kb * b + lax.broadcasted_iota(jnp.int32, (b, b), 0 if keys_first else 1)
        if self.band:
            return jnp.abs(qpos - kpos) <= SWA_WINDOW
        return (kpos < self.t) | (qpos >= self.t)


def _attn_fwd(cfg, q, k, vt, sink):
    b, dk, dv, g = cfg.blk, cfg.dk, cfg.dv, cfg.group

    def body(q_ref, k_ref, vt_ref, sink_ref, o_ref, lse_ref, m_sc, l_sc, acc_sc):
        i, st = pl.program_id(1), pl.program_id(2)

        @pl.when(st == 0)
        def _():
            if cfg.has_sink:
                for gg in range(g):
                    m_sc[gg] = jnp.broadcast_to(sink_ref[gg, 0:1, 0:1], (8, b))
                l_sc[...] = jnp.ones_like(l_sc)
            else:
                m_sc[...] = jnp.full_like(m_sc, NEG_INF)
                l_sc[...] = jnp.zeros_like(l_sc)
            acc_sc[...] = jnp.zeros_like(acc_sc)

        def step(use_mask):
            vis = cfg.visible(i, cfg.kblock(i, st), keys_first=True) if use_mask else None
            kv, vtv = k_ref[...], vt_ref[...]
            for gg in range(g):
                sc = lax.dot_general(kv, q_ref[:, gg * dk:(gg + 1) * dk], _NT,
                                     preferred_element_type=F32)
                if use_mask:
                    sc = jnp.where(vis, sc, NEG_INF)
                m_prev = m_sc[gg, 0:1, :]
                m_new = jnp.maximum(m_prev, jnp.max(sc, axis=0, keepdims=True))
                alpha = jnp.exp2(m_prev - m_new)
                p = jnp.exp2(sc - m_new)
                l_new = alpha * l_sc[gg, 0:1, :] + jnp.sum(p, axis=0, keepdims=True)
                acc_sc[gg] = acc_sc[gg] * alpha + jnp.dot(vtv, p.astype(_MXU_DTYPE), preferred_element_type=F32)
                m_sc[gg] = jnp.broadcast_to(m_new, (8, b))
                l_sc[gg] = jnp.broadcast_to(l_new, (8, b))

        ok = cfg.valid(i, st)
        msk = cfg.masked(i, st)
        pl.when(ok & msk)(lambda: step(True))
        pl.when(ok & jnp.logical_not(msk))(lambda: step(False))

        @pl.when(st == cfg.ks - 1)
        def _():
            for gg in range(g):
                o_ref[gg * dv:(gg + 1) * dv, :] = acc_sc[gg] / l_sc[gg, 0:1, :]
            lse_ref[...] = m_sc[...] + jnp.log2(l_sc[...])

    return pl.pallas_call(
        body, name=cfg.name + "_fwd",
        grid=(cfg.kv_heads, cfg.nq, cfg.ks),
        in_specs=[pl.BlockSpec((b, g * dk), lambda h, i, st: (i, h)),
                  pl.BlockSpec((b, dk), lambda h, i, st: (cfg.kblock(i, st), h)),
                  pl.BlockSpec((dv, b), lambda h, i, st: (h, cfg.kblock(i, st))),
                  pl.BlockSpec((g, 8, _LANES), lambda h, i, st: (h, 0, 0))],
        out_specs=[pl.BlockSpec((g * dv, b), lambda h, i, st: (h, i)),
                   pl.BlockSpec((g, 8, b), lambda h, i, st: (h, 0, i))],
        out_shape=[jax.ShapeDtypeStruct((cfg.heads * dv, cfg.s), F32),
                   jax.ShapeDtypeStruct((cfg.heads, 8, cfg.s), F32)],
        scratch_shapes=[pltpu.VMEM((g, 8, b), F32), pltpu.VMEM((g, 8, b), F32), pltpu.VMEM((g, dv, b), F32)],
        compiler_params=_cparams(("parallel", "parallel", "arbitrary")),
    )(q, k, vt, sink)


def _attn_bwd(cfg, q, k, kt, v, sink, ot, lse, do, dot_):
    b, dk, dv, g = cfg.blk, cfg.dk, cfg.dv, cfg.group

    def body(q_ref, k_ref, kt_ref, v_ref, sink_ref, ot_ref, lse_ref, do_ref, dot_ref,
             dqt_ref, dk_ref, dv_ref, dsink_ref, dqt_sc, delta_sc):
        i, st = pl.program_id(1), pl.program_id(2)

        @pl.when((i == 0) & (st == 0))
        def _():
            dk_ref[...] = jnp.zeros_like(dk_ref)
            dv_ref[...] = jnp.zeros_like(dv_ref)

        @pl.when(st == 0)
        def _():
            dqt_sc[...] = jnp.zeros_like(dqt_sc)
            for gg in range(g):
                vs = slice(gg * dv, (gg + 1) * dv)
                delta = jnp.sum(dot_ref[vs, :] * ot_ref[vs, :], axis=0, keepdims=True)
                delta_sc[gg] = jnp.broadcast_to(delta, (8, b))
                if cfg.has_sink:
                    ps = jnp.exp2(sink_ref[gg, 0:1, 0:1] - lse_ref[gg, 0:1, :]) * delta
                    dsink_ref[gg] = jnp.broadcast_to(-jnp.sum(ps, axis=1, keepdims=True), (8, _LANES))
                else:
                    dsink_ref[gg] = jnp.zeros((8, _LANES), F32)

        def step(use_mask):
            kb = cfg.kblock(i, st)
            vis = cfg.visible(i, kb, keys_first=True) if use_mask else None
            kbv, ktv, vbv = k_ref[...], kt_ref[...], v_ref[...]
            rows = pl.ds(pl.multiple_of(kb * b, b), b)
            dk_acc = None
            dv_acc = None
            for gg in range(g):
                ks, vs = slice(gg * dk, (gg + 1) * dk), slice(gg * dv, (gg + 1) * dv)
                qb = q_ref[:, ks]
                sc = lax.dot_general(kbv, qb, _NT, preferred_element_type=F32)
                if use_mask:
                    sc = jnp.where(vis, sc, NEG_INF)
                p = jnp.exp2(sc - lse_ref[gg, 0:1, :])
                dp = jnp.dot(vbv, dot_ref[vs, :].astype(_MXU_DTYPE), preferred_element_type=F32)
                ds = (p * (dp - delta_sc[gg, 0:1, :])).astype(_MXU_DTYPE)
                dqt_sc[gg] += jnp.dot(ktv, ds, preferred_element_type=F32)
                dk_h = jnp.dot(ds, qb, preferred_element_type=F32)
                dv_h = jnp.dot(p.astype(_MXU_DTYPE), do_ref[:, vs].astype(_MXU_DTYPE), preferred_element_type=F32)
                dk_acc = dk_h if dk_acc is None else dk_acc + dk_h
                dv_acc = dv_h if dv_acc is None else dv_acc + dv_h
            dk_ref[rows, :] += dk_acc
            dv_ref[rows, :] += dv_acc

        ok = cfg.valid(i, st)
        msk = cfg.masked(i, st)
        pl.when(ok & msk)(lambda: step(True))
        pl.when(ok & jnp.logical_not(msk))(lambda: step(False))

        @pl.when(st == cfg.ks - 1)
        def _():
            for gg in range(g):
                dqt_ref[gg * dk:(gg + 1) * dk, :] = (dqt_sc[gg] * (1.0 / _LOG2E)).astype(dqt_ref.dtype)

    return pl.pallas_call(
        body, name=cfg.name + "_bwd",
        grid=(cfg.kv_heads, cfg.nq, cfg.ks),
        in_specs=[pl.BlockSpec((b, g * dk), lambda h, i, st: (i, h)),
                  pl.BlockSpec((b, dk), lambda h, i, st: (cfg.kblock(i, st), h)),
                  pl.BlockSpec((dk, b), lambda h, i, st: (h, cfg.kblock(i, st))),
                  pl.BlockSpec((b, dv), lambda h, i, st: (cfg.kblock(i, st), h)),
                  pl.BlockSpec((g, 8, _LANES), lambda h, i, st: (h, 0, 0)),
                  pl.BlockSpec((g * dv, b), lambda h, i, st: (h, i)),
                  pl.BlockSpec((g, 8, b), lambda h, i, st: (h, 0, i)),
                  pl.BlockSpec((b, g * dv), lambda h, i, st: (i, h)),
                  pl.BlockSpec((g * dv, b), lambda h, i, st: (h, i))],
        out_specs=[pl.BlockSpec((g * dk, b), lambda h, i, st: (h, i)),
                   pl.BlockSpec((cfg.s, dk), lambda h, i, st: (0, h)),
                   pl.BlockSpec((cfg.s, dv), lambda h, i, st: (0, h)),
                   pl.BlockSpec((g, None, 8, _LANES), lambda h, i, st: (h, i, 0, 0))],
        out_shape=[jax.ShapeDtypeStruct((cfg.heads * dk, cfg.s), q.dtype),
                   jax.ShapeDtypeStruct((cfg.s, cfg.kv_heads * dk), F32),
                   jax.ShapeDtypeStruct((cfg.s, cfg.kv_heads * dv), F32),
                   jax.ShapeDtypeStruct((cfg.heads, cfg.nq, 8, _LANES), F32)],
        scratch_shapes=[pltpu.VMEM((g, dk, b), F32), pltpu.VMEM((g, 8, b), F32)],
        compiler_params=_cparams(("parallel", "arbitrary", "arbitrary")),
    )(q, k, kt, v, sink, ot, lse, do, dot_)


def _attention(cfg):
    def run(q, k, v, sink):
        sr = jnp.broadcast_to((sink.astype(F32) * _LOG2E)[:, None, None], (cfg.heads, 8, _LANES))
        ot, lse = _attn_fwd(cfg, q, k, v.T, sr)
        return ot.T, (sr, ot, lse)

    @jax.custom_vjp
    def op(q, k, v, sink):
        return run(q, k, v, sink)[0]

    def fwd(q, k, v, sink):
        o, (sr, ot, lse) = run(q, k, v, sink)
        return o, (q, k, v, sr, ot, lse)

    def bwd(res, do):
        q, k, v, sr, ot, lse = res
        dqt, dk, dv, dsink = _attn_bwd(cfg, q, k, k.T, v, sr, ot, lse, do, do.T)
        return dqt.T, (dk * (1.0 / _LOG2E)).astype(k.dtype), dv.astype(v.dtype), dsink[:, :, 0, 0].sum(1)

    op.defvjp(fwd, bwd)
    return op


def _ret_chunk_order(n, d, n_ctx, n_all):
    fwd = n
    bwd = jnp.where(n < n_ctx, n_ctx - 1 - n, n_all - 1 - (n - n_ctx))
    return jnp.where(d == 0, fwd, bwd)


def _ret_decays(lg, d):
    c = RET_CHUNK
    i = lax.broadcasted_iota(jnp.int32, (c, c), 0)
    j = lax.broadcasted_iota(jnp.int32, (c, c), 1)
    sign = (1 - 2 * d)
    diff = ((i - j) * sign).astype(F32)
    intra = jnp.where(diff >= 0, jnp.exp(lg * jnp.maximum(diff, 0.0)), 0.0)
    pos = lax.broadcasted_iota(jnp.int32, (c, 1), 0)
    r = (pos + d * (c - 1 - 2 * pos)).astype(F32)
    qd = jnp.exp(lg * (r + 1.0))
    kd = jnp.exp(lg * (c - 1.0 - r))
    cd = jnp.exp(lg * c)
    return intra, qd, kd, cd, diff, r


def _mxu_dot(a, b, dims=None):
    a, b = a.astype(_MXU_DTYPE), b.astype(_MXU_DTYPE)
    if dims is None:
        return jnp.dot(a, b, preferred_element_type=F32)
    return lax.dot_general(a, b, dims, preferred_element_type=F32)


_NT = (((1,), (1,)), ((), ()))


def _ret_fwd(q, k, v, lg, n_ctx, name):
    s = q.shape[0]
    c, hd = RET_CHUNK, HEAD_DIM
    n_all = s // c

    def body(lg_ref, q_ref, k_ref, v_ref, o_ref, st_ref, state_sc):
        d, n = pl.program_id(0), pl.program_id(1)

        @pl.when(n == 0)
        def _():
            state_sc[...] = jnp.zeros_like(state_sc)

        for h in range(RET_HEADS):
            cols = slice(h * hd, (h + 1) * hd)
            intra, qd, kd, cd, _, _ = _ret_decays(lg_ref[d, h], d)
            qv, kv, vv = q_ref[:, cols], k_ref[:, cols], v_ref[:, cols]
            s_in = state_sc[h]
            st_ref[h] = s_in
            p = _mxu_dot(qv, kv, _NT) * intra
            o_ref[:, cols] = _mxu_dot(p, vv) + _mxu_dot(qv * qd, s_in)
            state_sc[h] = cd * s_in + _mxu_dot((kv * kd).T, vv)

    def chunk_spec():
        return pl.BlockSpec((c, RET_DIM), lambda d, n: (_ret_chunk_order(n, d, n_ctx, n_all), 0))

    return pl.pallas_call(
        body, name=name + "_fwd",
        grid=(2, n_all),
        in_specs=[pl.BlockSpec(memory_space=pltpu.SMEM), chunk_spec(), chunk_spec(), chunk_spec()],
        out_specs=[pl.BlockSpec((None, c, RET_DIM), lambda d, n: (d, _ret_chunk_order(n, d, n_ctx, n_all), 0)),
                   pl.BlockSpec((None, RET_HEADS, None, hd, hd), lambda d, n: (d, 0, n, 0, 0))],
        out_shape=[jax.ShapeDtypeStruct((2, s, RET_DIM), F32),
                   jax.ShapeDtypeStruct((2, RET_HEADS, n_all, hd, hd), F32)],
        scratch_shapes=[pltpu.VMEM((RET_HEADS, hd, hd), F32)],
        compiler_params=_cparams(("arbitrary", "arbitrary")),
    )(lg, q, k, v)


def _ret_bwd(q, k, v, lg, states, dout, n_ctx, name):
    s = q.shape[0]
    c, hd = RET_CHUNK, HEAD_DIM
    n_all = s // c

    def body(lg_ref, q_ref, k_ref, v_ref, st_ref, do_ref, dq_ref, dk_ref, dv_ref, dlg_ref, ds_sc):
        d, n = pl.program_id(0), pl.program_id(1)

        @pl.when(n == 0)
        def _():
            ds_sc[...] = jnp.zeros_like(ds_sc)
            dlg_ref[...] = jnp.zeros_like(dlg_ref)

        for h in range(RET_HEADS):
            cols = slice(h * hd, (h + 1) * hd)
            intra, qd, kd, cd, diff, r = _ret_decays(lg_ref[d, h], d)
            qv, kv, vv, do = q_ref[:, cols], k_ref[:, cols], v_ref[:, cols], do_ref[:, cols]
            s_in = st_ref[h]
            ds_out = ds_sc[h]
            sc = _mxu_dot(qv, kv, _NT)
            p = sc * intra
            dp = _mxu_dot(do, vv, _NT)
            dsc = dp * intra
            dqs = _mxu_dot(do, s_in, _NT)
            dkk = _mxu_dot(vv, ds_out, _NT)
            dq_ref[:, cols] = _mxu_dot(dsc, kv) + dqs * qd
            dk_ref[:, cols] = _mxu_dot(dsc.T, qv) + dkk * kd
            dv_ref[:, cols] = _mxu_dot(p.T, do) + _mxu_dot(kv * kd, ds_out)
            ds_sc[h] = cd * ds_out + _mxu_dot((qv * qd).T, do)
            dlg = (_sum_all(dp * p * diff)
                   + _sum_all(jnp.sum(dqs * qv, axis=1, keepdims=True) * qd * (r + 1.0))
                   + _sum_all(jnp.sum(dkk * kv, axis=1, keepdims=True) * kd * (c - 1.0 - r))
                   + _sum_all(ds_out * s_in) * (cd * c))
            dlg_ref[h] += jnp.broadcast_to(dlg, (8, _LANES))

    def order(n, d):
        return _ret_chunk_order(n_all - 1 - n, d, n_ctx, n_all)

    def chunk_spec():
        return pl.BlockSpec((c, RET_DIM), lambda d, n: (order(n, d), 0))

    def dir_spec():
        return pl.BlockSpec((None, c, RET_DIM), lambda d, n: (d, order(n, d), 0))

    return pl.pallas_call(
        body, name=name + "_bwd",
        grid=(2, n_all),
        in_specs=[pl.BlockSpec(memory_space=pltpu.SMEM), chunk_spec(), chunk_spec(), chunk_spec(),
                  pl.BlockSpec((None, RET_HEADS, None, hd, hd), lambda d, n: (d, 0, n_all - 1 - n, 0, 0)),
                  dir_spec()],
        out_specs=[dir_spec(), dir_spec(), dir_spec(),
                   pl.BlockSpec((None, RET_HEADS, 8, _LANES), lambda d, n: (d, 0, 0, 0))],
        out_shape=[jax.ShapeDtypeStruct((2, s, RET_DIM), F32)] * 3
        + [jax.ShapeDtypeStruct((2, RET_HEADS, 8, _LANES), F32)],
        scratch_shapes=[pltpu.VMEM((RET_HEADS, hd, hd), F32)],
        compiler_params=_cparams(("arbitrary", "arbitrary")),
    )(lg, q, k, v, states, dout)


def _retention(n_ctx, name):
    @jax.custom_vjp
    def op(q, k, v, lg):
        return _ret_fwd(q, k, v, lg, n_ctx, name)[0]

    def fwd(q, k, v, lg):
        out, states = _ret_fwd(q, k, v, lg, n_ctx, name)
        return out, (q, k, v, lg, states)

    def bwd(res, dout):
        q, k, v, lg, states = res
        dq, dk, dv, dlg = _ret_bwd(q, k, v, lg, states, dout, n_ctx, name)
        return dq[0] + dq[1], dk[0] + dk[1], dv[0] + dv[1], dlg[:, :, 0, 0]

    op.defvjp(fwd, bwd)
    return op


def _loss_call(y, target, n_ctx_tiles, tile, name):
    s, dm = y.shape

    def body(y_ref, t_ref, loss_ref, dy_ref):
        i = pl.program_id(0)

        @pl.when(i == 0)
        def _():
            loss_ref[...] = jnp.zeros_like(loss_ref)

        @pl.when(i < n_ctx_tiles)
        def _():
            dy_ref[...] = jnp.zeros_like(dy_ref)

        @pl.when(i >= n_ctx_tiles)
        def _():
            err = y_ref[...] - t_ref[...]
            dy_ref[...] = err * (1.0 / dm)
            loss_ref[...] += jnp.broadcast_to(_sum_all(err * err) * (0.5 / dm), loss_ref.shape)

    return pl.pallas_call(
        body, name=name,
        grid=(s // tile,),
        in_specs=[pl.BlockSpec((tile, dm), lambda i: (i, 0)),
                  pl.BlockSpec((tile, dm), lambda i: (jnp.maximum(i - n_ctx_tiles, 0), 0))],
        out_specs=[pl.BlockSpec((8, _LANES), lambda i: (0, 0)),
                   pl.BlockSpec((tile, dm), lambda i: (i, 0))],
        out_shape=[jax.ShapeDtypeStruct((8, _LANES), F32), jax.ShapeDtypeStruct((s, dm), F32)],
        compiler_params=_cparams(("arbitrary",)),
    )(y, target)


def _loss_op(n_ctx_tiles, tile):
    @jax.custom_vjp
    def op(y, target):
        return _loss_call(y, target, n_ctx_tiles, tile, "loss_head")[0][0, 0]

    def fwd(y, target):
        loss, dy = _loss_call(y, target, n_ctx_tiles, tile, "loss_head")
        return loss[0, 0], (dy, target)

    def bwd(res, g):
        dy, target = res
        return dy * g, jnp.zeros_like(target)

    op.defvjp(fwd, bwd)
    return op


def _exchange(x, gather, name):
    blk_shape = x.shape if gather else x.shape[1:]

    def body(x_ref, o_ref, send_sems, recv_sems, local_sem):
        mx, my, mc = lax.axis_index("x"), lax.axis_index("y"), lax.axis_index("c")
        me = 4 * mx + 2 * my + mc
        copies = []
        for rel in range(1, _N_DEV):
            px = mx ^ ((rel >> 2) & 1)
            py = my ^ ((rel >> 1) & 1)
            pc = mc ^ (rel & 1)
            src = x_ref if gather else x_ref.at[4 * px + 2 * py + pc]
            cp = pltpu.make_async_remote_copy(
                src_ref=src, dst_ref=o_ref.at[me],
                send_sem=send_sems.at[rel - 1], recv_sem=recv_sems.at[rel - 1],
                device_id=(px, py, pc), device_id_type=pl.DeviceIdType.MESH)
            cp.start()
            copies.append(cp)
        mine = pltpu.make_async_copy(x_ref if gather else x_ref.at[me], o_ref.at[me], local_sem)
        mine.start()
        for cp in copies:
            cp.wait()
        mine.wait()

    return pl.pallas_call(
        body, name=name,
        in_specs=[pl.BlockSpec(memory_space=pltpu.HBM)],
        out_specs=pl.BlockSpec(memory_space=pltpu.HBM),
        out_shape=jax.ShapeDtypeStruct((_N_DEV,) + tuple(blk_shape), x.dtype),
        scratch_shapes=[pltpu.SemaphoreType.DMA((_N_DEV - 1,)), pltpu.SemaphoreType.DMA((_N_DEV - 1,)),
                        pltpu.SemaphoreType.DMA(())],
    )(x)


def _sum_parts(parts, name, out_dtype=F32):
    n, r, w = parts.shape
    tile = _pick(r, max(8, (1 << 19) // (w * n) // 16 * 16), 16)

    def body(p_ref, o_ref):
        acc = p_ref[0].astype(F32)
        for j in range(1, n):
            acc = acc + p_ref[j].astype(F32)
        o_ref[...] = acc.astype(o_ref.dtype)

    return pl.pallas_call(
        body, name=name, grid=(r // tile,),
        in_specs=[pl.BlockSpec((n, tile, w), lambda i: (0, i, 0))],
        out_specs=pl.BlockSpec((tile, w), lambda i: (i, 0)),
        out_shape=jax.ShapeDtypeStruct((r, w), out_dtype),
        compiler_params=_cparams(("arbitrary",)),
    )(parts)


def _chip_peers(mx, my):
    return [(1 - mx, my), (mx, 1 - my), (1 - mx, 1 - my)]


def _gather_two_level(xs, name):
    n_ops = len(xs)

    def body(*refs):
        x_refs, o_refs = refs[:n_ops], refs[n_ops:2 * n_ops]
        send_sems, recv_sems, local_sems = refs[2 * n_ops:]
        mx, my, mc = lax.axis_index("x"), lax.axis_index("y"), lax.axis_index("c")
        sibling = (mx, my, 1 - mc)
        chips = _chip_peers(mx, my)

        def copy(op, k, block, to, src=None):
            idx = 4 * block[0] + 2 * block[1] + block[2]
            dst = o_refs[op].at[idx]
            return pltpu.make_async_remote_copy(
                src_ref=dst if src is None else src, dst_ref=dst,
                send_sem=send_sems.at[op * 7 + k], recv_sem=recv_sems.at[op * 7 + k],
                device_id=to, device_id_type=pl.DeviceIdType.MESH)

        me = (mx, my, mc)
        mine, first, passed = [], [], []
        for op in range(n_ops):
            cp = pltpu.make_async_copy(x_refs[op], o_refs[op].at[4 * mx + 2 * my + mc], local_sems.at[op])
            cp.start()
            mine.append(cp)
            first.append(copy(op, 0, me, sibling, src=x_refs[op]))
            for j, chip in enumerate(chips):
                first.append(copy(op, 1 + j, me, (*chip, mc), src=x_refs[op]))
        for cp in first:
            cp.start()
        for j, chip in enumerate(chips):
            for op in range(n_ops):
                copy(op, 1 + j, (*chip, mc), me).wait_recv()
                cp = copy(op, 4 + j, (*chip, mc), sibling)
                cp.start()
                passed.append(cp)
        for op in range(n_ops):
            copy(op, 0, sibling, me).wait_recv()
            for j, chip in enumerate(chips):
                copy(op, 4 + j, (*chip, 1 - mc), me).wait_recv()
        for cp in first + passed:
            cp.wait_send()
        for cp in mine:
            cp.wait()

    hbm = pl.BlockSpec(memory_space=pltpu.HBM)
    return pl.pallas_call(
        body, name=name,
        in_specs=[hbm] * n_ops, out_specs=[hbm] * n_ops,
        out_shape=[jax.ShapeDtypeStruct((_N_DEV,) + tuple(x.shape), x.dtype) for x in xs],
        scratch_shapes=[pltpu.SemaphoreType.DMA((7 * n_ops,)), pltpu.SemaphoreType.DMA((7 * n_ops,)),
                        pltpu.SemaphoreType.DMA((n_ops,))],
    )(*xs)


def _swap_with_sibling(xs, name):
    n_ops = len(xs)

    def body(*refs):
        x_refs, o_refs = refs[:n_ops], refs[n_ops:2 * n_ops]
        send_sems, recv_sems = refs[2 * n_ops:]
        mx, my, mc = lax.axis_index("x"), lax.axis_index("y"), lax.axis_index("c")
        copies = []
        for op in range(n_ops):
            for q in range(4):
                rem = pltpu.make_async_remote_copy(
                    src_ref=x_refs[op].at[q], dst_ref=o_refs[op].at[q],
                    send_sem=send_sems.at[op * 4 + q], recv_sem=recv_sems.at[op * 4 + q],
                    device_id=(mx, my, 1 - mc), device_id_type=pl.DeviceIdType.MESH)
                rem.start()
                copies.append(rem)
        for cp in copies:
            cp.wait()

    hbm = pl.BlockSpec(memory_space=pltpu.HBM)
    return pl.pallas_call(
        body, name=name,
        in_specs=[hbm] * n_ops, out_specs=[hbm] * n_ops,
        out_shape=[jax.ShapeDtypeStruct(x.shape, x.dtype) for x in xs],
        scratch_shapes=[pltpu.SemaphoreType.DMA((4 * n_ops,)), pltpu.SemaphoreType.DMA((4 * n_ops,))],
    )(*xs)


def _add2(a, b, name, out_dtype):
    shape = a.shape
    a2, b2 = a.reshape(-1, shape[-1]), b.reshape(-1, shape[-1])
    r, w = a2.shape
    tile = _pick(r, max(16, (1 << 19) // w // 16 * 16), 16)

    def body(a_ref, b_ref, o_ref):
        o_ref[...] = (a_ref[...].astype(F32) + b_ref[...].astype(F32)).astype(o_ref.dtype)

    spec = pl.BlockSpec((tile, w), lambda i: (i, 0))
    return pl.pallas_call(
        body, name=name, grid=(r // tile,), in_specs=[spec, spec], out_specs=spec,
        out_shape=jax.ShapeDtypeStruct((r, w), out_dtype),
        compiler_params=_cparams(("arbitrary",)),
    )(a2, b2).reshape(shape)


def _scatter_chips(xs, name):
    n_ops = len(xs)

    def body(*refs):
        x_refs, o_refs = refs[:n_ops], refs[n_ops:2 * n_ops]
        send_sems, recv_sems, local_sems = refs[2 * n_ops:]
        mx, my, mc = lax.axis_index("x"), lax.axis_index("y"), lax.axis_index("c")
        my_chip = 2 * mx + my
        copies = []
        for op in range(n_ops):
            loc = pltpu.make_async_copy(x_refs[op].at[my_chip], o_refs[op].at[my_chip], local_sems.at[op])
            loc.start()
            copies.append(loc)
            for j, (px, py) in enumerate(_chip_peers(mx, my)):
                rem = pltpu.make_async_remote_copy(
                    src_ref=x_refs[op].at[2 * px + py], dst_ref=o_refs[op].at[my_chip],
                    send_sem=send_sems.at[op * 3 + j], recv_sem=recv_sems.at[op * 3 + j],
                    device_id=(px, py, mc), device_id_type=pl.DeviceIdType.MESH)
                rem.start()
                copies.append(rem)
        for cp in copies:
            cp.wait()

    hbm = pl.BlockSpec(memory_space=pltpu.HBM)
    return pl.pallas_call(
        body, name=name,
        in_specs=[hbm] * n_ops, out_specs=[hbm] * n_ops,
        out_shape=[jax.ShapeDtypeStruct(x.shape, x.dtype) for x in xs],
        scratch_shapes=[pltpu.SemaphoreType.DMA((3 * n_ops,)), pltpu.SemaphoreType.DMA((3 * n_ops,)),
                        pltpu.SemaphoreType.DMA((n_ops,))],
    )(*xs)


def _weights_gather_op(name):
    def impl(shards):
        got = _gather_two_level([s.astype(_MXU_DTYPE) for s in shards], name + "_gather")
        return tuple(g.astype(F32) for g in got)

    @jax.custom_vjp
    def op(shards):
        return impl(shards)

    def fwd(shards):
        return impl(shards), None

    def bwd(_, cts):
        mc = lax.axis_index("c")
        by_core = [jnp.swapaxes(g.astype(_MXU_DTYPE).reshape((4, 2) + g.shape[1:]), 0, 1) for g in cts]
        mine = [lax.dynamic_index_in_dim(p, mc, 0, keepdims=False) for p in by_core]
        theirs = [lax.dynamic_index_in_dim(p, 1 - mc, 0, keepdims=False) for p in by_core]
        got = _swap_with_sibling(theirs, name + "_scatter_pair")
        chip_sums = [_add2(a, b, "%s_pair_sum%d" % (name, k), _MXU_DTYPE) for k, (a, b) in enumerate(zip(mine, got))]
        crossed = _scatter_chips(chip_sums, name + "_scatter_chips")
        out = []
        for k, q in enumerate(crossed):
            flat = q.reshape(4, -1, q.shape[-1])
            out.append(_sum_parts(flat, "%s_chip_sum%d" % (name, k), F32).reshape(q.shape[1:]))
        return (tuple(out),)

    op.defvjp(fwd, bwd)
    return op


def _all_gather_op(name, payload_dtype):
    def impl(x):
        return _exchange(x.astype(payload_dtype), True, name + "_gather").astype(F32)

    @jax.custom_vjp
    def op(x):
        return impl(x)

    def fwd(x):
        return impl(x), None

    def bwd(_, g):
        return (_sum_parts(_exchange(g, False, name + "_scatter"), name + "_sum"),)

    op.defvjp(fwd, bwd)
    return op


def _adamw(w, g, m, v, partial, name):
    r, wd = w.shape
    tile = _pick(r, max(8, (1 << 20) // (4 * wd) // 8 * 8), 8)
    c1 = 1.0 / (1.0 - ADAM_B1 ** ADAM_STEP)
    c2 = 1.0 / (1.0 - ADAM_B2 ** ADAM_STEP)

    def body(w_ref, g_ref, m_ref, v_ref, go_ref, d_ref, mo_ref, vo_ref):
        if partial:
            g = g_ref[0]
            for j in range(1, _N_DEV):
                g = g + g_ref[j]
        else:
            g = g_ref[...]
        m_new = ADAM_B1 * m_ref[...] + (1.0 - ADAM_B1) * g
        v_new = ADAM_B2 * v_ref[...] + (1.0 - ADAM_B2) * (g * g)
        m_hat = m_new * c1
        v_hat = v_new * c2
        go_ref[...] = g
        d_ref[...] = -ADAM_LR * (m_hat / (jnp.sqrt(v_hat) + ADAM_EPS) + ADAM_WD * w_ref[...])
        mo_ref[...] = m_new
        vo_ref[...] = v_new

    spec = pl.BlockSpec((tile, wd), lambda i: (i, 0))
    g_spec = pl.BlockSpec((_N_DEV, tile, wd), lambda i: (0, i, 0)) if partial else spec
    return pl.pallas_call(
        body, name=name, grid=(r // tile,),
        in_specs=[spec, g_spec, spec, spec],
        out_specs=[spec] * 4,
        out_shape=[jax.ShapeDtypeStruct((r, wd), F32)] * 4,
        compiler_params=_cparams(("arbitrary",)),
    )(w, g, m, v)


def _pack(arrays):
    flat, meta, off = [], [], 0
    for a in arrays:
        n = int(np.prod(a.shape))
        pad = (-n) % _LANES
        flat.append(a.reshape(-1))
        if pad:
            flat.append(jnp.zeros((pad,), a.dtype))
        meta.append((off, a.shape))
        off += n + pad
    pad = (-off) % (8 * _LANES)
    if pad:
        flat.append(jnp.zeros((pad,), arrays[0].dtype))
    return jnp.concatenate(flat).reshape(-1, _LANES), meta


def _unpack(packed, meta):
    flat = packed.reshape(-1)
    return [flat[off:off + int(np.prod(shape))].reshape(shape) for off, shape in meta]


def _rope_tables(t, l, dim, width):
    rows = l // GRID_W
    r = np.repeat(np.arange(rows, dtype=np.float32), GRID_W)
    cc = np.tile(np.arange(GRID_W, dtype=np.float32), rows)
    n_freq = dim // 4
    inv = jnp.asarray(ROPE_THETA, F32) ** (-jnp.arange(n_freq, dtype=F32) / n_freq)
    ang_r = jnp.asarray(r)[:, None] * inv
    ang_c = jnp.asarray(cc)[:, None] * inv
    ang = jnp.concatenate([ang_r, ang_r, ang_c, ang_c], axis=-1)
    cos, sin = jnp.cos(ang), jnp.sin(ang)
    if width > dim:
        cos = jnp.concatenate([cos, jnp.ones((l, width - dim), F32)], axis=1)
        sin = jnp.concatenate([sin, jnp.zeros((l, width - dim), F32)], axis=1)
    cos = jnp.concatenate([jnp.ones((t, width), F32), cos], axis=0)
    sin = jnp.concatenate([jnp.zeros((t, width), F32), sin], axis=0)
    return cos, sin


def _full_weights(gathered):
    full = {}
    for n, g in zip(GATHERED, gathered):
        if GATHER_AXIS[n] == 0:
            full[n] = g.reshape(g.shape[0] * g.shape[1], g.shape[2])
        else:
            full[n] = jnp.transpose(g, (1, 0, 2)).reshape(g.shape[1], g.shape[0] * g.shape[2])
    return full


def _layer(l, stream, mod, wts, small, tables, dims, last):
    s, t, dm, dff = dims
    tile = min(256, t)
    n_ctx_tiles = t // tile
    tile_w = min(128, t)
    n_ctx_tiles_w = t // tile_w
    cos_h, sin_h, cos_m, sin_m = tables
    tag = "l%d_" % l

    def mrow(k):
        return mod[:, k:k + 1, :]

    def one(a):
        return a.reshape(1, 1, -1)

    w_in = jnp.concatenate([wts['w_in'], jnp.zeros((dm, IN_WIDTH_PAD - IN_WIDTH), F32)], axis=1)
    modulate = _rowwise(_fn_modulate, tag + "modulate", 1, [[(0, dm)]], [True], [True, True], [dm],
                        tile, n_ctx_tiles, mxu_outs=(0,))
    (h,) = modulate((stream,), (mrow(0), mrow(1)))
    proj = _linear(tag + "w_in")(h, w_in)

    in_parts = [(0, 512), (512, 512), (1024, 512), (1536, 512), (2048, 768), (2816, 256), (3072, 256),
                (3328, 512), (3840, 256), (4096, 128)]
    postproj = _rowwise(_fn_postproj, tag + "postproj", 5,
                        [in_parts, [(0, 128)], [(0, 128)], [(0, 128)], [(0, 128)]],
                        [True, False, False, False, False], [True, True],
                        [512, 512, 512, 512, 768, 256, 256, 512, 256, 128], tile_w, n_ctx_tiles_w,
                        mxu_outs=(4, 5, 6, 7, 8))
    (rq, rk, rv, rg, sq, sk, sv, cqn, ckvn, kr) = postproj(
        (proj, cos_h, sin_h, cos_m, sin_m), (one(small['mla_q_norm'][l]), one(small['mla_kv_norm'][l])))

    lg = jnp.stack([jax.nn.log_sigmoid(small['ret_decay_fwd'][l]), jax.nn.log_sigmoid(small['ret_decay_bwd'][l])])
    ret = _retention(t // RET_CHUNK, tag + "retention")(rq, rk, rv, lg)

    swa_blk = 256 if (t % 256 == 0 and s % 256 == 0) else 128
    swa_cfg = _AttnCfg(tag + "swa", s, t, SWA_HEADS, SWA_KV_HEADS, HEAD_DIM, HEAD_DIM, swa_blk, True,
                       HEAD_DIM ** -0.5, True)
    y_swa = _attention(swa_cfg)(sq, sk, sv, small['swa_sink'][l])

    w_uq = wts['mla_w_uq'].reshape(MLA_Q_RANK, MLA_HEADS, MLA_NOPE + MLA_ROPE)
    w_uq = jnp.concatenate([w_uq, jnp.zeros((MLA_Q_RANK, MLA_HEADS, 256 - MLA_NOPE - MLA_ROPE), F32)], axis=2)
    w_uq = w_uq.reshape(MLA_Q_RANK, MLA_HEADS * 256)
    w_ukv = wts['mla_w_ukv'].reshape(MLA_KV_RANK, MLA_HEADS, MLA_NOPE + MLA_V)
    w_ukv = jnp.concatenate([w_ukv[:, :, :MLA_NOPE].reshape(MLA_KV_RANK, -1),
                             w_ukv[:, :, MLA_NOPE:].reshape(MLA_KV_RANK, -1)], axis=1)
    q_lin = _linear(tag + "w_uq")(cqn, w_uq)
    kv_lin = _linear(tag + "w_ukv")(ckvn, w_ukv)
    kv_parts = [(hh * 128, 128) for hh in range(MLA_HEADS)] + [(MLA_HEADS * 128, MLA_HEADS * 128)]
    assemble = _rowwise(_fn_mla_assemble, tag + "mla_assemble", 5,
                        [[(0, MLA_HEADS * 256)], kv_parts, [(0, 128)], [(0, 128)], [(0, 128)]],
                        [True, True, True, False, False], [],
                        [MLA_HEADS * 256, MLA_HEADS * 256, MLA_HEADS * 128], tile_w, n_ctx_tiles_w,
                        mxu_outs=(0, 1, 2))
    (q_full, k_full, v_mla) = assemble((q_lin, kv_lin, kr, cos_m, sin_m), ())
    mla_cfg = _AttnCfg(tag + "mla", s, t, MLA_HEADS, MLA_HEADS, 256, MLA_V, _pick(s, 768, 128), False,
                       MLA_SCALE, False)
    y_mla = _attention(mla_cfg)(q_full, k_full, v_mla, jnp.zeros((MLA_HEADS,), F32))

    hparts = [(hh * 128, 128) for hh in range(RET_HEADS)]
    mix_op = _rowwise(_fn_mix, tag + "mix", 5, [hparts, hparts, hparts, [(0, 768)], [(0, 768)]],
                      [True] * 5, [], [dm_mix()], tile, n_ctx_tiles, mxu_outs=(0,))
    (mix_in,) = mix_op((ret[0], ret[1], rg, y_swa, y_mla), ())
    mix = _linear(tag + "w_o")(mix_in, wts['w_o'])
    ln1 = _rowwise(_fn_ln1, tag + "ln1", 2, [[(0, dm)], [(0, dm)]], [True, True], [True] * 5, [dm, dm],
                   tile, n_ctx_tiles, mxu_outs=(1,))
    x_a, h2 = ln1((stream, mix), (mrow(2), one(small['ln1_g'][l]), one(small['ln1_b'][l]), mrow(3), mrow(4)))
    u_lin = _linear(tag + "w_up_u", _MXU_DTYPE)(h2, wts['ffn_w_up'][:, :dff])
    g_lin = _linear(tag + "w_up_g", _MXU_DTYPE)(h2, wts['ffn_w_up'][:, dff:])
    conv_par = jnp.concatenate([wts['ffn_conv_w'], small['ffn_conv_b'][l][None, :], jnp.zeros((4, dff), F32)], axis=0)
    y = _conv_gate(t, tag + "conv_gate")(u_lin, g_lin, conv_par)
    f = _linear(tag + "w_down")(y, wts['ffn_w_down'])
    ln2 = _rowwise(_fn_ln2, tag + "ln2", 2, [[(0, dm)], [(0, dm)]], [True, True], [True] * 3, [dm],
                   tile, n_ctx_tiles)
    (out,) = ln2((x_a, f), (mrow(5), one(small['ln2_g'][l]), one(small['ln2_b'][l])))
    return out


def dm_mix():
    return RET_DIM + SWA_HEADS * HEAD_DIM + MLA_HEADS * MLA_V


def kernel(x, c, ctx, c_ctx, ada_w, ada_b, w_in, ret_decay_fwd, ret_decay_bwd, swa_sink, mla_q_norm, mla_w_uq, mla_kv_norm, mla_w_ukv, w_o, ln1_g, ln1_b, ffn_w_up, ffn_conv_w, ffn_conv_b, ffn_w_down, ln2_g, ln2_b, loss_target, m_c_ctx, m_ada_w, m_ada_b, m_w_in, m_ret_decay_fwd, m_ret_decay_bwd, m_swa_sink, m_mla_q_norm, m_mla_w_uq, m_mla_kv_norm, m_mla_w_ukv, m_w_o, m_ln1_g, m_ln1_b, m_ffn_w_up, m_ffn_conv_w, m_ffn_conv_b, m_ffn_w_down, m_ln2_g, m_ln2_b, v_c_ctx, v_ada_w, v_ada_b, v_w_in, v_ret_decay_fwd, v_ret_decay_bwd, v_swa_sink, v_mla_q_norm, v_mla_w_uq, v_mla_kv_norm, v_mla_w_ukv, v_w_o, v_ln1_g, v_ln1_b, v_ffn_w_up, v_ffn_conv_w, v_ffn_conv_b, v_ffn_w_down, v_ln2_g, v_ln2_b):
    weights = dict(c_ctx=c_ctx, ada_w=ada_w, ada_b=ada_b, w_in=w_in, ret_decay_fwd=ret_decay_fwd,
                   ret_decay_bwd=ret_decay_bwd, swa_sink=swa_sink, mla_q_norm=mla_q_norm, mla_w_uq=mla_w_uq,
                   mla_kv_norm=mla_kv_norm, mla_w_ukv=mla_w_ukv, w_o=w_o, ln1_g=ln1_g, ln1_b=ln1_b,
                   ffn_w_up=ffn_w_up, ffn_conv_w=ffn_conv_w, ffn_conv_b=ffn_conv_b, ffn_w_down=ffn_w_down,
                   ln2_g=ln2_g, ln2_b=ln2_b)
    m_in = dict(c_ctx=m_c_ctx, ada_w=m_ada_w, ada_b=m_ada_b, w_in=m_w_in, ret_decay_fwd=m_ret_decay_fwd,
                ret_decay_bwd=m_ret_decay_bwd, swa_sink=m_swa_sink, mla_q_norm=m_mla_q_norm, mla_w_uq=m_mla_w_uq,
                mla_kv_norm=m_mla_kv_norm, mla_w_ukv=m_mla_w_ukv, w_o=m_w_o, ln1_g=m_ln1_g, ln1_b=m_ln1_b,
                ffn_w_up=m_ffn_w_up, ffn_conv_w=m_ffn_conv_w, ffn_conv_b=m_ffn_conv_b, ffn_w_down=m_ffn_w_down,
                ln2_g=m_ln2_g, ln2_b=m_ln2_b)
    v_in = dict(c_ctx=v_c_ctx, ada_w=v_ada_w, ada_b=v_ada_b, w_in=v_w_in, ret_decay_fwd=v_ret_decay_fwd,
                ret_decay_bwd=v_ret_decay_bwd, swa_sink=v_swa_sink, mla_q_norm=v_mla_q_norm, mla_w_uq=v_mla_w_uq,
                mla_kv_norm=v_mla_kv_norm, mla_w_ukv=v_mla_w_ukv, w_o=v_w_o, ln1_g=v_ln1_g, ln1_b=v_ln1_b,
                ffn_w_up=v_ffn_w_up, ffn_conv_w=v_ffn_conv_w, ffn_conv_b=v_ffn_conv_b, ffn_w_down=v_ffn_w_down,
                ln2_g=v_ln2_g, ln2_b=v_ln2_b)

    l_tok, dm = x.shape[1], x.shape[2]
    t = ctx.shape[1]
    s = t + l_tok
    dff = ffn_w_down.shape[1] * _N_DEV
    dims = (s, t, dm, dff)
    me = 4 * lax.axis_index("x") + 2 * lax.axis_index("y") + lax.axis_index("c")
    cos_h, sin_h = _rope_tables(t, l_tok, HEAD_DIM, HEAD_DIM)
    cos_m, sin_m = _rope_tables(t, l_tok, MLA_ROPE, _LANES)
    tables = (cos_h, sin_h, cos_m, sin_m)
    c_all = _exchange(c, True, "gather_cond").reshape(_N_DEV, dm)
    tile = min(256, t)

    def loss_fn(wd, xin):
        mod_rows = jnp.concatenate([jax.nn.silu(c_all), jax.nn.silu(wd['c_ctx'])[None, :],
                                  jnp.zeros((_MOD_ROWS - _N_DEV - 1, dm), F32)], axis=0)
        used = 16
        mods_shard = jnp.stack([_linear("ada_l%d" % l)(mod_rows, wd['ada_w'][l])[:used] for l in range(DEPTH)])
        n_sh = mods_shard.shape[-1]
        mods_all = _all_gather_op("mods", F32)(mods_shard.reshape(DEPTH * used, n_sh))
        mods_all = mods_all.reshape(_N_DEV, DEPTH, used, n_sh).transpose(1, 2, 0, 3).reshape(DEPTH, used, _N_DEV * n_sh)
        mods_all = mods_all + wd['ada_b'][:, None, :]
        mod_x = lax.dynamic_slice_in_dim(mods_all, me, 1, axis=1)[:, 0]
        mod_c = mods_all[:, _N_DEV]
        stream = jnp.concatenate([ctx[0], xin[0]], axis=0)
        for l in range(DEPTH):
            gathered = _weights_gather_op("weights_l%d" % l)(tuple(wd[n][l] for n in GATHERED))
            full = _full_weights(gathered)
            mod = jnp.stack([mod_c[l].reshape(N_MOD, dm), mod_x[l].reshape(N_MOD, dm)])
            stream = _layer(l, stream, mod, full, wd, tables, dims, l == DEPTH - 1)
        return _loss_op(t // tile, tile)(stream, loss_target[0])

    loss_local, (gw, gx) = jax.value_and_grad(loss_fn, argnums=(0, 1))(weights, x)
    loss = lax.psum(loss_local, ("x", "y", "c"))

    grads, deltas, new_m, new_v = {}, {}, {}, {}

    def as2d(a):
        return a.reshape(-1, a.shape[-1])

    for n in ['ada_w'] + GATHERED:
        g2, d2, m2, v2 = _adamw(as2d(weights[n]), as2d(gw[n]), as2d(m_in[n]), as2d(v_in[n]), False, "adamw_" + n)
        shp = weights[n].shape
        grads[n], deltas[n], new_m[n], new_v[n] = g2.reshape(shp), d2.reshape(shp), m2.reshape(shp), v2.reshape(shp)

    w_pack, meta = _pack([weights[n] for n in REPLICATED])
    g_pack, _ = _pack([gw[n] for n in REPLICATED])
    m_pack, _ = _pack([m_in[n] for n in REPLICATED])
    v_pack, _ = _pack([v_in[n] for n in REPLICATED])
    g_parts = _exchange(g_pack, True, "gather_small_grads")
    outs = _adamw(w_pack, g_parts, m_pack, v_pack, True, "adamw_replicated")
    for dst, packed in zip((grads, deltas, new_m, new_v), outs):
        for n, a in zip(REPLICATED, _unpack(packed, meta)):
            dst[n] = a

    return (loss, gx, *[grads[n] for n in WEIGHTS], *[deltas[n] for n in WEIGHTS],
            *[new_m[n] for n in WEIGHTS], *[new_v[n] for n in WEIGHTS])
```
